```python
import jax, jax.numpy as jnp
from jax import lax
import numpy as np


D_MODEL = 1024
BATCH = 4
SEQ = 8192
DEPTH = 2

CHUNK = 64
Q_BLOCK = 128
P_DIM = 256
EPS = 1e-6
NEG_INF = -1e30
ROPE_THETA = 10000.0

MIX_WIDTH = D_MODEL
MLA_WIDTH = MIX_WIDTH // 2
FOX_WIDTH = MIX_WIDTH - MLA_WIDTH
MLA_V_DIM = 64
MLA_HEADS = MLA_WIDTH // MLA_V_DIM
MLA_NOPE_DIM = 64
MLA_ROPE_DIM = 32
MLA_QK_DIM = MLA_NOPE_DIM + MLA_ROPE_DIM
Q_LORA = D_MODEL // 4
KV_LORA = D_MODEL // 8
FOX_HEAD_DIM = 64
FOX_HEADS = FOX_WIDTH // FOX_HEAD_DIM

IN_SIZES = (Q_LORA, KV_LORA, MLA_ROPE_DIM, FOX_WIDTH, FOX_WIDTH, FOX_WIDTH, FOX_HEADS)
IN_COLS = sum(IN_SIZES)
IN_SPLITS = tuple(int(v) for v in np.cumsum(IN_SIZES)[:-1])

N_GROUPS = 4
EXPERTS_PER_GROUP = 8
N_EXPERTS = N_GROUPS * EXPERTS_PER_GROUP
TOP_K = 2
EXPERT_FF = D_MODEL // 4
EXPERT_BLOCK = 128

kernel_name = 'hybrid_mla_fox_hmoe_ple'


def rmsnorm(x, g):
    xf = x.astype(jnp.float32)
    y = xf * lax.rsqrt(jnp.mean(xf * xf, axis=-1, keepdims=True) + EPS)
    return (y * g.astype(jnp.float32)).astype(x.dtype)


def rope_tables(positions):
    inv_freq = ROPE_THETA ** (-jnp.arange(0, MLA_ROPE_DIM, 2, dtype=jnp.float32) / MLA_ROPE_DIM)
    ang = positions.astype(jnp.float32)[..., None] * inv_freq
    return jnp.cos(ang)[:, :, None, :], jnp.sin(ang)[:, :, None, :]


def apply_rope(x, cos, sin):
    half = MLA_ROPE_DIM // 2
    x1, x2 = x[..., :half], x[..., half:]
    out = jnp.concatenate([x1 * cos - x2 * sin, x2 * cos + x1 * sin], axis=-1)
    return out.astype(x.dtype)


def chunk_causal(q_idx, k_idx):
    return (k_idx // CHUNK) <= (q_idx // CHUNK)


def frame_causal(q_idx, k_idx):
    return k_idx <= q_idx


def blocked_attention(q, k, v, allowed, log_decay_cum=None):
    B, H, S, Dk = q.shape
    Dv = v.shape[-1]
    n_blocks = S // Q_BLOCK
    scale = Dk ** -0.5
    k_idx = jnp.arange(S, dtype=jnp.int32)

    def one_block(i):
        start = i * Q_BLOCK
        qb = lax.dynamic_slice_in_dim(q, start, Q_BLOCK, axis=2)
        q_idx = start + jnp.arange(Q_BLOCK, dtype=jnp.int32)
        s = jnp.einsum('bhqd,bhkd->bhqk', qb, k, preferred_element_type=jnp.float32) * scale
        if log_decay_cum is not None:
            f_q = lax.dynamic_slice_in_dim(log_decay_cum, start, Q_BLOCK, axis=2)
            s = s + f_q[..., :, None] - log_decay_cum[..., None, :]
        mask = allowed(q_idx[:, None], k_idx[None, :])
        s = jnp.where(mask, s, NEG_INF)
        w = jax.nn.softmax(s, axis=-1).astype(v.dtype)
        return jnp.einsum('bhqk,bhkd->bhqd', w, v)

    out = lax.map(one_block, jnp.arange(n_blocks, dtype=jnp.int32))
    return out.transpose(1, 2, 0, 3, 4).reshape(B, H, S, Dv)


def hybrid_mixer(h, cos, sin, w_in, g_q_lora, w_uq, g_kv_lora, w_ukv, g_mla_q, g_mla_k,
                 g_fox_q, g_fox_k, b_fox_f, g_out_mla, g_out_fox, w_out):
    B, S, _ = h.shape
    proj = h @ w_in
    q_c, kv_c, k_pe, fq, fk, fv, f_logit = jnp.split(proj, IN_SPLITS, axis=-1)

    q = (rmsnorm(q_c, g_q_lora) @ w_uq).reshape(B, S, MLA_HEADS, MLA_QK_DIM)
    kv = (rmsnorm(kv_c, g_kv_lora) @ w_ukv).reshape(B, S, MLA_HEADS, MLA_NOPE_DIM + MLA_V_DIM)
    k_nope, v_mla = kv[..., :MLA_NOPE_DIM], kv[..., MLA_NOPE_DIM:]
    k_pe_b = jnp.broadcast_to(k_pe[:, :, None, :], (B, S, MLA_HEADS, MLA_ROPE_DIM))
    k = jnp.concatenate([k_nope, k_pe_b], axis=-1)
    q = rmsnorm(q, g_mla_q)
    k = rmsnorm(k, g_mla_k)
    q = jnp.concatenate([q[..., :MLA_NOPE_DIM], apply_rope(q[..., MLA_NOPE_DIM:], cos, sin)], axis=-1)
    k = jnp.concatenate([k[..., :MLA_NOPE_DIM], apply_rope(k[..., MLA_NOPE_DIM:], cos, sin)], axis=-1)
    o_mla = blocked_attention(q.transpose(0, 2, 1, 3), k.transpose(0, 2, 1, 3),
                              v_mla.transpose(0, 2, 1, 3), chunk_causal)
    o_mla = o_mla.transpose(0, 2, 1, 3).reshape(B, S, MLA_WIDTH)

    fq = rmsnorm(fq.reshape(B, S, FOX_HEADS, FOX_HEAD_DIM), g_fox_q)
    fk = rmsnorm(fk.reshape(B, S, FOX_HEADS, FOX_HEAD_DIM), g_fox_k)
    fv = fv.reshape(B, S, FOX_HEADS, FOX_HEAD_DIM)
    log_f = jax.nn.log_sigmoid(f_logit.astype(jnp.float32) + b_fox_f.astype(jnp.float32))
    f_cum = jnp.cumsum(log_f, axis=1).transpose(0, 2, 1)
    o_fox = blocked_attention(fq.transpose(0, 2, 1, 3), fk.transpose(0, 2, 1, 3),
                              fv.transpose(0, 2, 1, 3), frame_causal, f_cum)
    o_fox = o_fox.transpose(0, 2, 1, 3).reshape(B, S, FOX_WIDTH)

    y = jnp.concatenate([rmsnorm(o_mla, g_out_mla), rmsnorm(o_fox, g_out_fox)], axis=-1)
    return y @ w_out


def hierarchical_moe(h, w_router_group, w_router_expert, w_exp_gate, w_exp_up, w_exp_down):
    B, S, D = h.shape
    N = B * S
    hf = h.reshape(N, D)
    p_group = jax.nn.softmax((hf @ w_router_group).astype(jnp.float32), axis=-1)
    g_idx = jnp.argmax(p_group, axis=-1).astype(jnp.int32)
    p_g = jnp.max(p_group, axis=-1)
    e_logits = (hf @ w_router_expert).astype(jnp.float32).reshape(N, N_GROUPS, EXPERTS_PER_GROUP)
    e_logits = jnp.take_along_axis(e_logits, g_idx[:, None, None], axis=1)[:, 0]
    p_exp = jax.nn.softmax(e_logits, axis=-1)
    top_p, top_i = lax.top_k(p_exp, TOP_K)
    gate = p_g[:, None] * top_p / jnp.sum(top_p, axis=-1, keepdims=True)
    expert_id = g_idx[:, None] * EXPERTS_PER_GROUP + top_i.astype(jnp.int32)

    A = N * TOP_K
    flat_e = expert_id.reshape(A)
    flat_g = gate.reshape(A)
    flat_tok = jnp.repeat(jnp.arange(N, dtype=jnp.int32), TOP_K)
    order = jnp.argsort(flat_e)
    e_sorted = flat_e[order]
    counts = jnp.zeros((N_EXPERTS,), jnp.int32).at[flat_e].add(1)
    padded = ((counts + EXPERT_BLOCK - 1) // EXPERT_BLOCK) * EXPERT_BLOCK
    pad_end = jnp.cumsum(padded)
    pad_start = pad_end - padded
    start = jnp.cumsum(counts) - counts
    dest = pad_start[e_sorted] + (jnp.arange(A, dtype=jnp.int32) - start[e_sorted])
    P = A + N_EXPERTS * EXPERT_BLOCK
    slot_tok = jnp.full((P,), N, jnp.int32).at[dest].set(flat_tok[order])
    slot_gate = jnp.zeros((P,), h.dtype).at[dest].set(flat_g[order].astype(h.dtype))
    n_blk = P // EXPERT_BLOCK
    blk_start = jnp.arange(n_blk, dtype=jnp.int32) * EXPERT_BLOCK
    blk_e = jnp.minimum(jnp.searchsorted(pad_end, blk_start, side='right'), N_EXPERTS - 1)

    h_pad = jnp.concatenate([hf, jnp.zeros((1, D), hf.dtype)], axis=0)
    xs = h_pad[slot_tok].reshape(n_blk, EXPERT_BLOCK, D)

    def expert_block(args):
        xb, e = args
        return (jax.nn.silu(xb @ w_exp_gate[e]) * (xb @ w_exp_up[e])) @ w_exp_down[e]

    ys = lax.map(expert_block, (xs, blk_e)).reshape(P, D)
    out = jnp.zeros((N + 1, D), h.dtype).at[slot_tok].add(ys * slot_gate[:, None])
    return out[:N].reshape(B, S, D)


def setup_inputs(seed: int = 0) -> dict:
    key = jax.random.key(seed)
    ks = jax.random.split(key, 27)
    f32 = jnp.float32

    def nrm(k, shape, fan_in):
        return jax.random.normal(k, shape, f32) * (fan_in ** -0.5)

    def gain(k, dim):
        return 1.0 + 0.05 * jax.random.normal(k, (DEPTH, dim), f32)

    x = jax.random.normal(ks[0], (BATCH, SEQ, D_MODEL), f32)
    p = jax.random.normal(ks[1], (DEPTH, BATCH, SEQ, P_DIM), f32)
    offsets = jax.random.randint(ks[2], (BATCH,), 0, 64, dtype=jnp.int32) * CHUNK
    positions = offsets[:, None] + jnp.arange(SEQ, dtype=jnp.int32)[None, :]
    return {
        'x': x,
        'p': p,
        'positions': positions,
        'g_attn_norm': gain(ks[3], D_MODEL),
        'w_in': nrm(ks[4], (DEPTH, D_MODEL, IN_COLS), D_MODEL),
        'g_q_lora': gain(ks[5], Q_LORA),
        'w_uq': nrm(ks[6], (DEPTH, Q_LORA, MLA_HEADS * MLA_QK_DIM), Q_LORA),
        'g_kv_lora': gain(ks[7], KV_LORA),
        'w_ukv': nrm(ks[8], (DEPTH, KV_LORA, MLA_HEADS * (MLA_NOPE_DIM + MLA_V_DIM)), KV_LORA),
        'g_mla_q': gain(ks[9], MLA_QK_DIM),
        'g_mla_k': gain(ks[10], MLA_QK_DIM),
        'g_fox_q': gain(ks[11], FOX_HEAD_DIM),
        'g_fox_k': gain(ks[12], FOX_HEAD_DIM),
        'b_fox_f': 2.0 + 0.5 * jax.random.normal(ks[13], (DEPTH, FOX_HEADS), f32),
        'g_out_mla': gain(ks[14], MLA_WIDTH),
        'g_out_fox': gain(ks[15], FOX_WIDTH),
        'w_out': nrm(ks[16], (DEPTH, MIX_WIDTH, D_MODEL), MIX_WIDTH),
        'g_ffn_norm': gain(ks[17], D_MODEL),
        'w_router_group': nrm(ks[18], (DEPTH, D_MODEL, N_GROUPS), D_MODEL),
        'w_router_expert': nrm(ks[19], (DEPTH, D_MODEL, N_EXPERTS), D_MODEL),
        'w_exp_gate': nrm(ks[20], (DEPTH, N_EXPERTS, D_MODEL, EXPERT_FF), D_MODEL),
        'w_exp_up': nrm(ks[21], (DEPTH, N_EXPERTS, D_MODEL, EXPERT_FF), D_MODEL),
        'w_exp_down': nrm(ks[22], (DEPTH, N_EXPERTS, EXPERT_FF, D_MODEL), EXPERT_FF),
        'g_ple_norm': gain(ks[23], D_MODEL),
        'w_ple_gate': nrm(ks[24], (DEPTH, D_MODEL, D_MODEL), D_MODEL),
        'w_ple_proj': nrm(ks[25], (DEPTH, P_DIM, D_MODEL), P_DIM),
        'g_ple_out': gain(ks[26], D_MODEL),
    }


def reference(x, p, positions, g_attn_norm, w_in, g_q_lora, w_uq, g_kv_lora, w_ukv, g_mla_q, g_mla_k,
              g_fox_q, g_fox_k, b_fox_f, g_out_mla, g_out_fox, w_out, g_ffn_norm, w_router_group,
              w_router_expert, w_exp_gate, w_exp_up, w_exp_down, g_ple_norm, w_ple_gate, w_ple_proj,
              g_ple_out):
    cos, sin = rope_tables(positions)
    for i in range(DEPTH):
        h = rmsnorm(x, g_attn_norm[i])
        x = x + hybrid_mixer(h, cos, sin, w_in[i], g_q_lora[i], w_uq[i], g_kv_lora[i], w_ukv[i],
                             g_mla_q[i], g_mla_k[i], g_fox_q[i], g_fox_k[i], b_fox_f[i],
                             g_out_mla[i], g_out_fox[i], w_out[i])
        h = rmsnorm(x, g_ffn_norm[i])
        x = x + hierarchical_moe(h, w_router_group[i], w_router_expert[i], w_exp_gate[i],
                                 w_exp_up[i], w_exp_down[i])
        ple = rmsnorm(p[i] @ w_ple_proj[i], g_ple_out[i])
        ple_gate = jax.nn.sigmoid(rmsnorm(x, g_ple_norm[i]) @ w_ple_gate[i])
        x = x + ple * ple_gate
    return x
```

```python
import functools

import numpy as np
import jax
import jax.numpy as jnp
from jax import lax
from jax.experimental import pallas as pl
from jax.experimental.pallas import tpu as pltpu

F32 = jnp.float32
BF16 = jnp.bfloat16

EPS = 1e-6
NEG_INF = -1e30
ROPE_THETA = 10000.0

LANES = 128
CHUNK = 64
MLA_HEADS = 8
MLA_NOPE = 64
MLA_ROPE = 32
MLA_QK = MLA_NOPE + MLA_ROPE
MLA_V = 64
FOX_HEADS = 8
FOX_DIM = 64
Q_LORA = 256
KV_LORA = 128
N_GROUPS = 4
EXPERTS_PER_GROUP = 8
N_EXPERTS = N_GROUPS * EXPERTS_PER_GROUP
EXPERT_FF = 256
ROUTE_LANE0 = N_GROUPS

TOKEN_TILE = 512
ATTN_BLOCK = 512
EXPERT_BLOCK = 256
DISPATCH_TILE = 1024
PLE_TILE = 256
VMEM_LIMIT = 56 * 1024 * 1024

_NT = (((1,), (1,)), ((), ()))


def _cparams(n_axes):
    return pltpu.CompilerParams(dimension_semantics=("arbitrary",) * n_axes,
                                vmem_limit_bytes=VMEM_LIMIT)


def _rms(x, g):
    return x * lax.rsqrt(jnp.mean(x * x, axis=-1, keepdims=True) + EPS) * g


def _split3(x):
    hi = x.astype(BF16)
    r1 = x - hi.astype(F32)
    mid = r1.astype(BF16)
    lo = (r1 - mid.astype(F32)).astype(BF16)
    return hi, mid, lo


def _rope_kernel(pos_ref, invf_ref, sign_ref, cos_ref, sin_ref):
    ang = pos_ref[...] * invf_ref[...]
    cos_ref[...] = jnp.cos(ang)
    sin_ref[...] = jnp.sin(ang) * sign_ref[...]


def _rope_tables(positions):
    n = positions.size
    half = MLA_ROPE // 2
    inv_freq = ROPE_THETA ** (-np.arange(0, MLA_ROPE, 2, dtype=np.float32) / MLA_ROPE)
    invf = np.zeros((1, LANES), np.float32)
    sign = np.zeros((1, LANES), np.float32)
    invf[0, MLA_NOPE:MLA_NOPE + half] = inv_freq
    invf[0, MLA_NOPE + half:MLA_QK] = inv_freq
    sign[0, MLA_NOPE:MLA_NOPE + half] = -1.0
    sign[0, MLA_NOPE + half:MLA_QK] = 1.0
    pos = positions.astype(F32).reshape(n, 1)
    t = TOKEN_TILE
    return pl.pallas_call(
        _rope_kernel,
        out_shape=(jax.ShapeDtypeStruct((n, LANES), F32),) * 2,
        grid=(n // t,),
        in_specs=[pl.BlockSpec((t, 1), lambda i: (i, 0)),
                  pl.BlockSpec((1, LANES), lambda i: (0, 0)),
                  pl.BlockSpec((1, LANES), lambda i: (0, 0))],
        out_specs=(pl.BlockSpec((t, LANES), lambda i: (i, 0)),) * 2,
        compiler_params=_cparams(1),
        name="rope_tables",
    )(pos, jnp.asarray(invf), jnp.asarray(sign))


def _pre_kernel(x_ref, cos_ref, sin_ref, gattn_ref, wa_ref, wb_ref, wfl_ref, bfl_ref, tri_ref,
                gql_ref, wuq_ref, gkvl_ref, wuk_ref, wuv_ref, gq_ref, gk_ref, gfq_ref, gfk_ref,
                q_out, k_out, v_out, fq_out, fk_out, fv_out, fcum_out, carry_ref, *, tiles_per_seq):
    step = pl.program_id(0)
    x = x_ref[...]
    h = _rms(x, gattn_ref[...]).astype(BF16)
    pa = jnp.dot(h, wa_ref[...], preferred_element_type=F32)
    pb = jnp.dot(h, wb_ref[...], preferred_element_type=F32)

    lane = lax.broadcasted_iota(jnp.int32, (1, LANES), 1)
    cos_t = cos_ref[...]
    sin_t = sin_ref[...]
    half = MLA_ROPE // 2

    def rope(v):
        swapped = jnp.where(lane < MLA_NOPE + half,
                            pltpu.roll(v, LANES - half, 1), pltpu.roll(v, half, 1))
        return v * cos_t + swapped * sin_t

    def head_norm(v, g):
        ss = jnp.sum(v * v, axis=-1, keepdims=True) * (1.0 / MLA_QK)
        return v * lax.rsqrt(ss + EPS) * g

    qn = _rms(pa[:, :Q_LORA], gql_ref[...]).astype(BF16)
    q = jnp.dot(qn, wuq_ref[...], preferred_element_type=F32)
    kvn = _rms(pa[:, Q_LORA:Q_LORA + KV_LORA], gkvl_ref[...]).astype(BF16)
    kn = jnp.dot(kvn, wuk_ref[...], preferred_element_type=F32)
    v_out[...] = jnp.dot(kvn, wuv_ref[...], preferred_element_type=F32).astype(BF16)
    kpe = pa[:, Q_LORA + KV_LORA:Q_LORA + KV_LORA + LANES]
    gq = gq_ref[...]
    gk = gk_ref[...]
    for hd in range(MLA_HEADS):
        sl = slice(hd * LANES, (hd + 1) * LANES)
        q_out[:, sl] = rope(head_norm(q[:, sl], gq)).astype(BF16)
        k_out[:, sl] = rope(head_norm(kn[:, sl] + kpe, gk)).astype(BF16)

    lo = lane < FOX_DIM
    width = FOX_HEADS * FOX_DIM

    def pair_norm(v, g):
        v2 = v * v
        s_lo = jnp.sum(jnp.where(lo, v2, 0.0), axis=-1, keepdims=True) * (1.0 / FOX_DIM)
        s_hi = jnp.sum(jnp.where(lo, 0.0, v2), axis=-1, keepdims=True) * (1.0 / FOX_DIM)
        r = jnp.where(lo, lax.rsqrt(s_lo + EPS), lax.rsqrt(s_hi + EPS))
        return v * r * g

    gfq = gfq_ref[...]
    gfk = gfk_ref[...]
    for j in range(width // LANES):
        sl = slice(j * LANES, (j + 1) * LANES)
        fq_out[:, sl] = pair_norm(pb[:, sl], gfq).astype(BF16)
        fk_out[:, sl] = pair_norm(pb[:, width + j * LANES:width + (j + 1) * LANES], gfk).astype(BF16)
    fv_out[...] = pb[:, 2 * width:].astype(BF16)

    z = lax.dot_general(wfl_ref[...], h, _NT, preferred_element_type=F32) + bfl_ref[:, :1]
    logf = jnp.minimum(z, 0.0) - jnp.log1p(jnp.exp(-jnp.abs(z)))
    tri = tri_ref[...]
    cum = sum(jnp.dot(part, tri, preferred_element_type=F32) for part in _split3(logf))

    @pl.when(step % tiles_per_seq == 0)
    def _():
        carry_ref[...] = jnp.zeros_like(carry_ref)

    cum = cum + carry_ref[:, :1]
    fcum_out[0] = cum[:FOX_HEADS]
    carry_ref[...] = jnp.broadcast_to(cum[:, -1:], carry_ref.shape)


def _pre_attention(x2d, cos_t, sin_t, w, seq):
    n, d = x2d.shape
    t = TOKEN_TILE
    tiles_per_seq = seq // t
    batch = n // seq
    const = lambda i: (0, 0)
    row = lambda i: (i, 0)
    full = lambda a: pl.BlockSpec(a.shape, const)
    ins = [x2d, cos_t, sin_t, w["g_attn"], w["w_a"], w["w_b"], w["w_fl"], w["b_fl"], w["tri_incl"],
           w["g_q_lora"], w["w_uq"], w["g_kv_lora"], w["w_uk"], w["w_uv"],
           w["g_mla_q"], w["g_mla_k"], w["g_fox_q"], w["g_fox_k"]]
    in_specs = [pl.BlockSpec((t, d), row), pl.BlockSpec((t, LANES), row), pl.BlockSpec((t, LANES), row)]
    in_specs += [full(a) for a in ins[3:]]
    wq = MLA_HEADS * LANES
    wv = MLA_HEADS * MLA_V
    wf = FOX_HEADS * FOX_DIM
    out_shape = (jax.ShapeDtypeStruct((n, wq), BF16), jax.ShapeDtypeStruct((n, wq), BF16),
                 jax.ShapeDtypeStruct((n, wv), BF16), jax.ShapeDtypeStruct((n, wf), BF16),
                 jax.ShapeDtypeStruct((n, wf), BF16), jax.ShapeDtypeStruct((n, wf), BF16),
                 jax.ShapeDtypeStruct((batch, FOX_HEADS, seq), F32))
    out_specs = (pl.BlockSpec((t, wq), row), pl.BlockSpec((t, wq), row), pl.BlockSpec((t, wv), row),
                 pl.BlockSpec((t, wf), row), pl.BlockSpec((t, wf), row), pl.BlockSpec((t, wf), row),
                 pl.BlockSpec((1, FOX_HEADS, t), lambda i: (i // tiles_per_seq, 0, i % tiles_per_seq)))
    return pl.pallas_call(
        functools.partial(_pre_kernel, tiles_per_seq=tiles_per_seq),
        out_shape=out_shape,
        grid=(n // t,),
        in_specs=in_specs,
        out_specs=out_specs,
        scratch_shapes=[pltpu.VMEM((2 * FOX_HEADS, LANES), F32)],
        compiler_params=_cparams(1),
        name="pre_attention",
    )(*ins)


def _attn_kernel(*refs, mla, blk):
    if mla:
        q_ref, k_ref, v_ref, o_ref = refs
        f_ref = None
    else:
        q_ref, k_ref, v_ref, f_ref, o_ref = refs
    hp = pl.program_id(1)
    qi = pl.program_id(2)
    lane = lax.broadcasted_iota(jnp.int32, (1, LANES), 1)
    lo = lane < FOX_DIM
    q = q_ref[0]
    if mla:
        qs = [q[:, :LANES], q[:, LANES:]]
    else:
        zero = jnp.zeros_like(q)
        qs = [jnp.where(lo, q, zero), jnp.where(lo, zero, q)]
    r_idx = lax.broadcasted_iota(jnp.int32, (blk, blk), 0)
    c_idx = lax.broadcasted_iota(jnp.int32, (blk, blk), 1)
    if mla:
        diag_mask = (c_idx // CHUNK) <= (r_idx // CHUNK)
    else:
        diag_mask = c_idx <= r_idx

    def block(kb, carry, masked):
        k0 = pl.multiple_of(kb * blk, blk)
        kblk = k_ref[0, pl.ds(k0, blk), :]
        vblk = v_ref[0, pl.ds(k0, blk), :]
        out = []
        for i in range(2):
            m, l, acc = carry[i]
            kk = kblk[:, i * LANES:(i + 1) * LANES] if mla else kblk
            s = lax.dot_general(qs[i], kk, _NT, preferred_element_type=F32)
            if not mla:
                s = s - f_ref[0, pl.ds(2 * hp + i, 1), pl.ds(k0, blk)]
            if masked:
                s = jnp.where(diag_mask, s, NEG_INF)
            m_new = jnp.maximum(m, jnp.max(s, axis=-1, keepdims=True))
            p = jnp.exp(s - m_new)
            alpha = jnp.exp(m - m_new)
            l = alpha * l + jnp.sum(p, axis=-1, keepdims=True)
            acc = alpha * acc + jnp.dot(p.astype(BF16), vblk, preferred_element_type=F32)
            out.append((m_new, l, acc))
        return tuple(out)

    init = tuple((jnp.full((blk, 1), NEG_INF, F32), jnp.zeros((blk, 1), F32),
                  jnp.zeros((blk, LANES), F32)) for _ in range(2))
    carry = lax.fori_loop(0, qi, lambda kb, c: block(kb, c, False), init)
    carry = block(qi, carry, True)
    o0 = carry[0][2] / carry[0][1]
    o1 = carry[1][2] / carry[1][1]
    o_ref[0] = jnp.where(lo, o0, o1).astype(o_ref.dtype)


def _attention(q, k, v, fcum, *, mla):
    batch, seq, wq = q.shape
    blk = ATTN_BLOCK
    pairs = v.shape[-1] // LANES
    w = wq // pairs
    in_specs = [pl.BlockSpec((1, blk, w), lambda b, h, i: (b, i, h)),
                pl.BlockSpec((1, seq, w), lambda b, h, i: (b, 0, h)),
                pl.BlockSpec((1, seq, LANES), lambda b, h, i: (b, 0, h))]
    ins = [q, k, v]
    if not mla:
        in_specs.append(pl.BlockSpec((1, FOX_HEADS, seq), lambda b, h, i: (b, 0, 0)))
        ins.append(fcum)
    return pl.pallas_call(
        functools.partial(_attn_kernel, mla=mla, blk=blk),
        out_shape=jax.ShapeDtypeStruct((batch, seq, pairs * LANES), BF16),
        grid=(batch, pairs, seq // blk),
        in_specs=in_specs,
        out_specs=pl.BlockSpec((1, blk, LANES), lambda b, h, i: (b, i, h)),
        compiler_params=_cparams(3),
        name="attn_mla" if mla else "attn_fox",
    )(*ins)


def _post_kernel(x_ref, om_ref, of_ref, gom_ref, gof_ref, wom_ref, wof_ref, gffn_ref, wrh_ref, wrl_ref,
                 tri_ref, x1_out, h2_out, route_out, count_out, carry_ref):
    step = pl.program_id(0)
    ym = _rms(om_ref[...].astype(F32), gom_ref[...]).astype(BF16)
    yf = _rms(of_ref[...].astype(F32), gof_ref[...]).astype(BF16)
    x1 = (x_ref[...] + jnp.dot(ym, wom_ref[...], preferred_element_type=F32)
          + jnp.dot(yf, wof_ref[...], preferred_element_type=F32))
    x1_out[...] = x1
    h2 = _rms(x1, gffn_ref[...])
    h2_out[...] = h2

    h_hi = h2.astype(BF16)
    h_lo = (h2 - h_hi.astype(F32)).astype(BF16)
    w_hi = wrh_ref[...]
    logits = (jnp.dot(h_hi, w_hi, preferred_element_type=F32)
              + jnp.dot(h_lo, w_hi, preferred_element_type=F32)
              + jnp.dot(h_hi, wrl_ref[...], preferred_element_type=F32))

    lane = lax.broadcasted_iota(jnp.int32, (1, LANES), 1).astype(F32)
    big = float(LANES)

    def first_argmax(v):
        mx = jnp.max(v, axis=-1, keepdims=True)
        idx = jnp.min(jnp.where(v == mx, lane, big), axis=-1, keepdims=True)
        return mx, idx

    lg = jnp.where(lane < N_GROUPS, logits, NEG_INF)
    mg, g_idx = first_argmax(lg)
    p_g = 1.0 / jnp.sum(jnp.exp(lg - mg), axis=-1, keepdims=True)
    e_lo = ROUTE_LANE0 + EXPERTS_PER_GROUP * g_idx
    le = jnp.where((lane >= e_lo) & (lane < e_lo + EXPERTS_PER_GROUP), logits, NEG_INF)
    m1, i1 = first_argmax(le)
    m2, i2 = first_argmax(jnp.where(lane == i1, NEG_INF, le))
    e2 = jnp.exp(m2 - m1)
    gate1 = p_g / (1.0 + e2)
    gate2 = p_g * e2 / (1.0 + e2)

    oh1 = lane == i1
    oh2 = lane == i2
    oh = (jnp.where(oh1, 1.0, 0.0) + jnp.where(oh2, 1.0, 0.0))
    before = jnp.dot(tri_ref[...], oh.astype(BF16), preferred_element_type=F32)

    @pl.when(step == 0)
    def _():
        carry_ref[...] = jnp.zeros_like(carry_ref)

    base = before + carry_ref[:1, :]
    rank1 = jnp.sum(jnp.where(oh1, base, 0.0), axis=-1, keepdims=True)
    rank2 = jnp.sum(jnp.where(oh2, base, 0.0), axis=-1, keepdims=True)
    total = carry_ref[:1, :] + jnp.sum(oh, axis=0, keepdims=True)
    carry_ref[...] = jnp.broadcast_to(total, carry_ref.shape)
    count_out[...] = jnp.broadcast_to(total, count_out.shape)

    route = jnp.where(lane == 0, i1 - ROUTE_LANE0, 0.0)
    route = jnp.where(lane == 1, i2 - ROUTE_LANE0, route)
    route = jnp.where(lane == 2, rank1, route)
    route = jnp.where(lane == 3, rank2, route)
    route = jnp.where(lane == 4, gate1, route)
    route = jnp.where(lane == 5, gate2, route)
    route_out[...] = route


def _post_attention(x2d, o_mla, o_fox, w):
    n, d = x2d.shape
    t = TOKEN_TILE
    const = lambda i: (0, 0)
    row = lambda i: (i, 0)
    full = lambda a: pl.BlockSpec(a.shape, const)
    ins = [x2d, o_mla, o_fox, w["g_out_mla"], w["g_out_fox"], w["w_out_mla"], w["w_out_fox"],
           w["g_ffn"], w["w_router_hi"], w["w_router_lo"], w["tri_strict"]]
    in_specs = [pl.BlockSpec((t, d), row), pl.BlockSpec((t, o_mla.shape[1]), row),
                pl.BlockSpec((t, o_fox.shape[1]), row)] + [full(a) for a in ins[3:]]
    return pl.pallas_call(
        _post_kernel,
        out_shape=(jax.ShapeDtypeStruct((n, d), F32), jax.ShapeDtypeStruct((n, d), F32),
                   jax.ShapeDtypeStruct((n, LANES), F32), jax.ShapeDtypeStruct((8, LANES), F32)),
        grid=(n // t,),
        in_specs=in_specs,
        out_specs=(pl.BlockSpec((t, d), row), pl.BlockSpec((t, d), row),
                   pl.BlockSpec((t, LANES), row), pl.BlockSpec((8, LANES), const)),
        scratch_shapes=[pltpu.VMEM((8, LANES), F32)],
        compiler_params=_cparams(1),
        name="post_attention",
    )(*ins)


def _dispatch_kernel(dest_ref, h_hbm, xs_in_hbm, xs_hbm, sem, *, tile):
    del xs_in_hbm
    base = pl.program_id(0) * tile

    def issue(j, c):
        for k in range(2):
            d = dest_ref[0, 0, 2 * j + k]
            pltpu.make_async_copy(h_hbm.at[pl.ds(base + j, 1)], xs_hbm.at[pl.ds(d, 1)], sem).start()
        return c

    lax.fori_loop(0, tile, issue, 0)

    def drain(j, c):
        pltpu.make_async_copy(h_hbm.at[pl.ds(0, 1)], xs_hbm.at[pl.ds(0, 1)], sem).wait()
        return c

    lax.fori_loop(0, 2 * tile, drain, 0)


def _dispatch(h2, dest, n_rows):
    n, d = h2.shape
    tile = DISPATCH_TILE
    dest3 = dest.reshape(n // tile, 1, 2 * tile)
    xs0 = jnp.zeros((n_rows, d), h2.dtype)
    return pl.pallas_call(
        functools.partial(_dispatch_kernel, tile=tile),
        out_shape=jax.ShapeDtypeStruct((n_rows, d), h2.dtype),
        grid=(n // tile,),
        in_specs=[pl.BlockSpec((1, 1, 2 * tile), lambda i: (i, 0, 0), memory_space=pltpu.SMEM),
                  pl.BlockSpec(memory_space=pl.ANY),
                  pl.BlockSpec(memory_space=pl.ANY)],
        out_specs=pl.BlockSpec(memory_space=pl.ANY),
        scratch_shapes=[pltpu.SemaphoreType.DMA],
        input_output_aliases={2: 0},
        compiler_params=_cparams(1),
        name="moe_dispatch",
    )(dest3, h2, xs0)


def _expert_kernel(blk_e_ref, n_used_ref, x_ref, wgu_ref, wd_ref, y_ref):
    del blk_e_ref
    used = pl.program_id(0) < n_used_ref[0]

    @pl.when(used)
    def _():
        xb = x_ref[...].astype(BF16)
        gu = jnp.dot(xb, wgu_ref[0], preferred_element_type=F32)
        g = gu[:, :EXPERT_FF]
        act = (g / (1.0 + jnp.exp(-g))) * gu[:, EXPERT_FF:]
        y_ref[...] = jnp.dot(act.astype(BF16), wd_ref[0], preferred_element_type=F32)

    @pl.when(jnp.logical_not(used))
    def _():
        y_ref[...] = jnp.zeros_like(y_ref)


def _experts(xs, blk_e, n_used, w_gu, w_down):
    p, d = xs.shape
    r = EXPERT_BLOCK
    rows = lambda i, be, nu: (jnp.minimum(i, nu[0] - 1), 0)
    wsel = lambda i, be, nu: (be[i], 0, 0)
    return pl.pallas_call(
        _expert_kernel,
        out_shape=jax.ShapeDtypeStruct((p, d), F32),
        grid_spec=pltpu.PrefetchScalarGridSpec(
            num_scalar_prefetch=2,
            grid=(p // r,),
            in_specs=[pl.BlockSpec((r, d), rows),
                      pl.BlockSpec((1, d, 2 * EXPERT_FF), wsel),
                      pl.BlockSpec((1, EXPERT_FF, d), wsel)],
            out_specs=pl.BlockSpec((r, d), lambda i, be, nu: (i, 0))),
        compiler_params=_cparams(1),
        name="moe_experts",
    )(blk_e, n_used, xs, w_gu, w_down)


def _ple_kernel(dest_ref, dest_next_ref, x1_ref, route_ref, p_ref, gnorm_ref, wgate_ref, wproj_ref, gout_ref,
                ys_hbm, x_out, rows_ref, sem, *, tile):
    step = pl.program_id(0)
    n_steps = pl.num_programs(0)

    def row_copy(d, slot, k, j):
        return pltpu.make_async_copy(ys_hbm.at[pl.ds(d, 1)], rows_ref.at[slot, k, pl.ds(j, 1)], sem.at[slot])

    def gather(idx_ref, slot):
        def issue(j, c):
            for k in range(2):
                row_copy(idx_ref[0, 0, 2 * j + k], slot, k, j).start()
            return c
        lax.fori_loop(0, tile, issue, 0)

    @pl.when(step == 0)
    def _():
        gather(dest_ref, 0)

    @pl.when(step + 1 < n_steps)
    def _():
        gather(dest_next_ref, (step + 1) % 2)

    slot = step % 2

    def drain(j, c):
        for k in range(2):
            row_copy(0, slot, k, j).wait()
        return c

    lax.fori_loop(0, tile, drain, 0)

    route = route_ref[...]
    x2 = x1_ref[...] + route[:, 4:5] * rows_ref[slot, 0] + route[:, 5:6] * rows_ref[slot, 1]
    ple = _rms(jnp.dot(p_ref[...].astype(BF16), wproj_ref[...], preferred_element_type=F32), gout_ref[...])
    z = jnp.dot(_rms(x2, gnorm_ref[...]).astype(BF16), wgate_ref[...], preferred_element_type=F32)
    x_out[...] = x2 + ple / (1.0 + jnp.exp(-z))


def _combine_ple(x1, route, dest, ys, p2d, w):
    n, d = x1.shape
    t = PLE_TILE
    steps = n // t
    const = lambda i: (0, 0)
    row = lambda i: (i, 0)
    full = lambda a: pl.BlockSpec(a.shape, const)
    dest3 = dest.reshape(steps, 1, 2 * t)
    return pl.pallas_call(
        functools.partial(_ple_kernel, tile=t),
        out_shape=jax.ShapeDtypeStruct((n, d), F32),
        grid=(steps,),
        in_specs=[pl.BlockSpec((1, 1, 2 * t), lambda i: (i, 0, 0), memory_space=pltpu.SMEM),
                  pl.BlockSpec((1, 1, 2 * t), lambda i: (jnp.minimum(i + 1, steps - 1), 0, 0),
                               memory_space=pltpu.SMEM),
                  pl.BlockSpec((t, d), row), pl.BlockSpec((t, LANES), row),
                  pl.BlockSpec((t, p2d.shape[1]), row),
                  full(w["g_ple_norm"]), full(w["w_ple_gate"]), full(w["w_ple_proj"]), full(w["g_ple_out"]),
                  pl.BlockSpec(memory_space=pl.ANY)],
        out_specs=pl.BlockSpec((t, d), row),
        scratch_shapes=[pltpu.VMEM((2, 2, t, d), F32), pltpu.SemaphoreType.DMA((2,))],
        compiler_params=_cparams(1),
        name="moe_combine_ple",
    )(dest3, dest3, x1, route, p2d, w["g_ple_norm"], w["w_ple_gate"], w["w_ple_proj"], w["g_ple_out"], ys)


def _row(v):
    return v.reshape(1, -1).astype(F32)


def _layer_weights(i, g_attn_norm, w_in, g_q_lora, w_uq, g_kv_lora, w_ukv, g_mla_q, g_mla_k, g_fox_q,
                   g_fox_k, b_fox_f, g_out_mla, g_out_fox, w_out, g_ffn_norm, w_router_group,
                   w_router_expert, w_exp_gate, w_exp_up, w_exp_down, g_ple_norm, w_ple_gate, w_ple_proj,
                   g_ple_out):
    d = w_in.shape[1]
    wf = FOX_HEADS * FOX_DIM
    c_kv = Q_LORA
    c_pe = c_kv + KV_LORA
    c_fq = c_pe + MLA_ROPE
    c_fl = c_fq + 3 * wf
    win = w_in[i]
    pad_pe = jnp.zeros((d, LANES), F32).at[:, MLA_NOPE:MLA_QK].set(win[:, c_pe:c_fq])
    w = {}
    w["g_attn"] = _row(g_attn_norm[i])
    w["w_a"] = jnp.concatenate([win[:, :c_pe], pad_pe], axis=1).astype(BF16)
    w["w_b"] = win[:, c_fq:c_fl].astype(BF16)
    w["w_fl"] = jnp.zeros((2 * FOX_HEADS, d), F32).at[:FOX_HEADS].set(win[:, c_fl:].T).astype(BF16)
    b_fl = jnp.zeros((2 * FOX_HEADS,), F32).at[:FOX_HEADS].set(b_fox_f[i])
    w["b_fl"] = jnp.broadcast_to(b_fl[:, None], (2 * FOX_HEADS, LANES))
    w["g_q_lora"] = _row(g_q_lora[i])
    uq = w_uq[i].reshape(Q_LORA, MLA_HEADS, MLA_QK)
    w["w_uq"] = jnp.pad(uq, ((0, 0), (0, 0), (0, LANES - MLA_QK))).reshape(Q_LORA, -1).astype(BF16)
    w["g_kv_lora"] = _row(g_kv_lora[i])
    ukv = w_ukv[i].reshape(KV_LORA, MLA_HEADS, MLA_NOPE + MLA_V)
    w["w_uk"] = jnp.pad(ukv[:, :, :MLA_NOPE], ((0, 0), (0, 0), (0, LANES - MLA_NOPE))
                        ).reshape(KV_LORA, -1).astype(BF16)
    w["w_uv"] = ukv[:, :, MLA_NOPE:].reshape(KV_LORA, -1).astype(BF16)
    w["g_mla_q"] = _row(jnp.pad(g_mla_q[i], (0, LANES - MLA_QK))) * (MLA_QK ** -0.5)
    w["g_mla_k"] = _row(jnp.pad(g_mla_k[i], (0, LANES - MLA_QK)))
    w["g_fox_q"] = _row(jnp.tile(g_fox_q[i], LANES // FOX_DIM)) * (FOX_DIM ** -0.5)
    w["g_fox_k"] = _row(jnp.tile(g_fox_k[i], LANES // FOX_DIM))
    w["g_out_mla"] = _row(g_out_mla[i])
    w["g_out_fox"] = _row(g_out_fox[i])
    wm = MLA_HEADS * MLA_V
    w["w_out_mla"] = w_out[i, :wm].astype(BF16)
    w["w_out_fox"] = w_out[i, wm:].astype(BF16)
    w["g_ffn"] = _row(g_ffn_norm[i])
    wr = jnp.zeros((d, LANES), F32)
    wr = wr.at[:, :N_GROUPS].set(w_router_group[i]).at[:, ROUTE_LANE0:ROUTE_LANE0 + N_EXPERTS].set(
        w_router_expert[i])
    w["w_router_hi"] = wr.astype(BF16)
    w["w_router_lo"] = (wr - w["w_router_hi"].astype(F32)).astype(BF16)
    w["w_gu"] = jnp.concatenate([w_exp_gate[i], w_exp_up[i]], axis=-1).astype(BF16)
    w["w_down"] = w_exp_down[i].astype(BF16)
    w["g_ple_norm"] = _row(g_ple_norm[i])
    w["w_ple_gate"] = w_ple_gate[i].astype(BF16)
    w["w_ple_proj"] = w_ple_proj[i].astype(BF16)
    w["g_ple_out"] = _row(g_ple_out[i])
    t = TOKEN_TILE
    r = lax.broadcasted_iota(jnp.int32, (t, t), 0)
    c = lax.broadcasted_iota(jnp.int32, (t, t), 1)
    w["tri_incl"] = (r <= c).astype(BF16)
    w["tri_strict"] = (c < r).astype(BF16)
    return w


def _route_plan(route, counts, n):
    r = EXPERT_BLOCK
    cnt = counts[0, ROUTE_LANE0:ROUTE_LANE0 + N_EXPERTS].astype(jnp.int32)
    padded = ((cnt + r - 1) // r) * r
    pad_end = jnp.cumsum(padded)
    pad_start = pad_end - padded
    e = route[:, 0:2].astype(jnp.int32)
    rank = route[:, 2:4].astype(jnp.int32)
    dest = pad_start[e] + rank
    n_rows = 2 * n + N_EXPERTS * r
    n_blk = n_rows // r
    blk_start = jnp.arange(n_blk, dtype=jnp.int32) * r
    blk_e = jnp.minimum(jnp.searchsorted(pad_end, blk_start, side="right"), N_EXPERTS - 1).astype(jnp.int32)
    n_used = (pad_end[-1:] // r).astype(jnp.int32)
    return dest.reshape(-1), blk_e, n_used, n_rows


def kernel(x, p, positions, g_attn_norm, w_in, g_q_lora, w_uq, g_kv_lora, w_ukv, g_mla_q, g_mla_k, g_fox_q,
           g_fox_k, b_fox_f, g_out_mla, g_out_fox, w_out, g_ffn_norm, w_router_group, w_router_expert,
           w_exp_gate, w_exp_up, w_exp_down, g_ple_norm, w_ple_gate, w_ple_proj, g_ple_out):
    batch, seq, d = x.shape
    n = batch * seq
    depth = w_in.shape[0]
    params = (g_attn_norm, w_in, g_q_lora, w_uq, g_kv_lora, w_ukv, g_mla_q, g_mla_k, g_fox_q, g_fox_k,
              b_fox_f, g_out_mla, g_out_fox, w_out, g_ffn_norm, w_router_group, w_router_expert,
              w_exp_gate, w_exp_up, w_exp_down, g_ple_norm, w_ple_gate, w_ple_proj, g_ple_out)
    cos_t, sin_t = _rope_tables(positions)
    xc = x.reshape(n, d)
    for i in range(depth):
        w = _layer_weights(i, *params)
        q, k, v, fq, fk, fv, fcum = _pre_attention(xc, cos_t, sin_t, w, seq)
        b3 = lambda a: a.reshape(batch, seq, a.shape[-1])
        o_mla = _attention(b3(q), b3(k), b3(v), None, mla=True).reshape(n, -1)
        o_fox = _attention(b3(fq), b3(fk), b3(fv), fcum, mla=False).reshape(n, -1)
        x1, h2, route, counts = _post_attention(xc, o_mla, o_fox, w)
        dest, blk_e, n_used, n_rows = _route_plan(route, counts, n)
        xs = _dispatch(h2, dest, n_rows)
        ys = _experts(xs, blk_e, n_used, w["w_gu"], w["w_down"])
        xc = _combine_ple(x1, route, dest, ys, p[i].reshape(n, -1), w)
    return xc.reshape(batch, seq, d)
```

```python
import functools
import math

import numpy as np
import jax
import jax.numpy as jnp
from jax import lax
from jax.experimental import pallas as pl
from jax.experimental.pallas import tpu as pltpu

F32 = jnp.float32
BF16 = jnp.bfloat16

EPS = 1e-6
NEG_INF = -1e30
ROPE_THETA = 10000.0
LOG2E = math.log2(math.e)

LANES = 128
CHUNK = 64
HEADS = 8
MLA_NOPE = 64
MLA_ROPE = 32
MLA_QK = MLA_NOPE + MLA_ROPE
HEAD_V = 64
FOX_DIM = 64
Q_LORA = 256
KV_LORA = 128
N_GROUPS = 4
EXPERTS_PER_GROUP = 8
N_EXPERTS = N_GROUPS * EXPERTS_PER_GROUP
EXPERT_FF = 256
ROUTE_LANE0 = N_GROUPS

AUG_Q = FOX_DIM
AUG_K = FOX_DIM + 3
ONE_LANE = HEADS

TOKEN_TILE = 512
ATTN_TQ = 1024
ATTN_TK = 1024
SUM_ROWS = 16
EXPERT_BLOCK = 256
DISPATCH_TILE = 1024
PLE_TILE = 256
VMEM_LIMIT = 56 * 1024 * 1024
SAFE_SCORE_BOUND = 40.0

_NT = (((1,), (1,)), ((), ()))


def _cparams(n_axes):
    return pltpu.CompilerParams(dimension_semantics=("arbitrary",) * n_axes,
                                vmem_limit_bytes=VMEM_LIMIT)


def _rms(x, g):
    return x * lax.rsqrt(jnp.mean(x * x, axis=-1, keepdims=True) + EPS) * g


def _split3(x):
    hi = x.astype(BF16)
    r1 = x - hi.astype(F32)
    mid = r1.astype(BF16)
    lo = (r1 - mid.astype(F32)).astype(BF16)
    return hi, mid, lo


def _lane_tile(a, width):
    return jnp.tile(a, (1, width // LANES))


def _rope_kernel(pos_col_ref, pos_row_ref, invf_lane_ref, sign_ref, invf_rep_ref,
                 cos_ref, sin_ref, cost_ref, sint_ref):
    ang = pos_col_ref[...] * invf_lane_ref[...]
    cos_ref[...] = jnp.cos(ang)
    sin_ref[...] = jnp.sin(ang) * sign_ref[...]
    t = pos_row_ref.shape[-1]
    ang_t = _lane_tile(invf_rep_ref[...], t) * pos_row_ref[0]
    cost_ref[...] = jnp.cos(ang_t)
    sint_ref[...] = jnp.sin(ang_t)


def _rope_tables(positions):
    n = positions.size
    half = MLA_ROPE // 2
    inv_freq = ROPE_THETA ** (-np.arange(0, MLA_ROPE, 2, dtype=np.float32) / MLA_ROPE)
    invf = np.zeros((1, LANES), np.float32)
    sign = np.zeros((1, LANES), np.float32)
    invf[0, MLA_NOPE:MLA_NOPE + half] = inv_freq
    invf[0, MLA_NOPE + half:MLA_QK] = inv_freq
    sign[0, MLA_NOPE:MLA_NOPE + half] = -1.0
    sign[0, MLA_NOPE + half:MLA_QK] = 1.0
    invf_rep = np.broadcast_to(inv_freq[:, None], (half, LANES)).astype(np.float32)
    pos = positions.astype(F32)
    t = TOKEN_TILE
    const = lambda i: (0, 0)
    return pl.pallas_call(
        _rope_kernel,
        out_shape=(jax.ShapeDtypeStruct((n, LANES), F32), jax.ShapeDtypeStruct((n, LANES), F32),
                   jax.ShapeDtypeStruct((half, n), F32), jax.ShapeDtypeStruct((half, n), F32)),
        grid=(n // t,),
        in_specs=[pl.BlockSpec((t, 1), lambda i: (i, 0)),
                  pl.BlockSpec((1, 1, t), lambda i: (i, 0, 0)),
                  pl.BlockSpec((1, LANES), const), pl.BlockSpec((1, LANES), const),
                  pl.BlockSpec((half, LANES), const)],
        out_specs=(pl.BlockSpec((t, LANES), lambda i: (i, 0)), pl.BlockSpec((t, LANES), lambda i: (i, 0)),
                   pl.BlockSpec((half, t), lambda i: (0, i)), pl.BlockSpec((half, t), lambda i: (0, i))),
        compiler_params=_cparams(1),
        name="rope_tables",
    )(pos.reshape(n, 1), pos.reshape(n // t, 1, t), jnp.asarray(invf), jnp.asarray(sign),
      jnp.asarray(invf_rep))


def _pre_kernel(x_ref, cos_ref, sin_ref, cost_ref, sint_ref, gattn_ref, wa_ref, wfk_ref, wfqt_ref, wfvt_ref,
                bfl_ref, tri_ref, plk_ref, plq_ref, gql_ref, wuqt_ref, gkvl_ref, wuk_ref, wuvt_ref,
                gq_ref, gk_ref, gfq_ref, gfk_ref,
                qt_out, k_out, vt_out, fqt_out, fk_out, fvt_out, carry_ref, *, tiles_per_seq):
    step = pl.program_id(0)
    x = x_ref[...]
    t = x.shape[0]
    h = _rms(x, gattn_ref[...]).astype(BF16)
    pa = jnp.dot(h, wa_ref[...], preferred_element_type=F32)
    half = MLA_ROPE // 2
    lane = lax.broadcasted_iota(jnp.int32, (1, LANES), 1)

    qn = _rms(pa[:, :Q_LORA], gql_ref[...]).astype(BF16)
    kvn = _rms(pa[:, Q_LORA:Q_LORA + KV_LORA], gkvl_ref[...]).astype(BF16)
    vt_out[0] = lax.dot_general(wuvt_ref[...], kvn, _NT, preferred_element_type=F32).astype(BF16)

    kn = jnp.dot(kvn, wuk_ref[...], preferred_element_type=F32)
    kpe = pa[:, Q_LORA + KV_LORA:Q_LORA + KV_LORA + LANES]
    cos_l = cos_ref[...]
    sin_l = sin_ref[...]
    gk = gk_ref[...]
    for hd in range(HEADS):
        sl = slice(hd * LANES, (hd + 1) * LANES)
        v = kn[:, sl] + kpe
        v = v * lax.rsqrt(jnp.sum(v * v, axis=-1, keepdims=True) * (1.0 / MLA_QK) + EPS) * gk
        swapped = jnp.where(lane < MLA_NOPE + half, pltpu.roll(v, LANES - half, 1), pltpu.roll(v, half, 1))
        k_out[:, sl] = (v * cos_l + swapped * sin_l).astype(BF16)

    qt = lax.dot_general(wuqt_ref[...], qn, _NT, preferred_element_type=F32)
    cos_r = cost_ref[...]
    sin_r = sint_ref[...]
    gq = _lane_tile(gq_ref[...], t)
    x1s = slice(MLA_NOPE, MLA_NOPE + half)
    x2s = slice(MLA_NOPE + half, MLA_QK)
    for hd in range(HEADS):
        blk = qt[hd * LANES:(hd + 1) * LANES]
        r = lax.rsqrt(jnp.sum(blk * blk, axis=0, keepdims=True) * (1.0 / MLA_QK) + EPS)
        blk = blk * r * gq
        x1, x2 = blk[x1s], blk[x2s]
        blk = jnp.concatenate([blk[:MLA_NOPE], x1 * cos_r - x2 * sin_r, x2 * cos_r + x1 * sin_r,
                               blk[MLA_QK:]], axis=0)
        qt_out[0, hd * LANES:(hd + 1) * LANES, :] = blk.astype(BF16)

    fvt_out[0] = lax.dot_general(wfvt_ref[...], h, _NT, preferred_element_type=F32).astype(BF16)

    z = pa[:, Q_LORA + KV_LORA + LANES:] + bfl_ref[...]
    logf = jnp.minimum(z, 0.0) - jnp.log1p(jnp.exp(-jnp.abs(z)))
    tri = tri_ref[...]
    cum = sum(jnp.dot(tri, part, preferred_element_type=F32) for part in _split3(logf))

    @pl.when(step % tiles_per_seq == 0)
    def _():
        carry_ref[...] = jnp.zeros_like(carry_ref)

    cum = cum + carry_ref[:1, :]
    carry_ref[...] = jnp.broadcast_to(cum[t - 1:t, :], carry_ref.shape)
    p1, p2, p3 = _split3(cum * LOG2E)
    p1 = jnp.where(lane == ONE_LANE, jnp.ones_like(p1), p1)
    pieces = (p1, p2, p3)
    aug_k = sum(jnp.dot(pc, plk_ref[j], preferred_element_type=F32) for j, pc in enumerate(pieces))
    aug_q = sum(lax.dot_general(plq_ref[j], pc, _NT, preferred_element_type=F32)
                for j, pc in enumerate(pieces))

    fk = jnp.dot(h, wfk_ref[...], preferred_element_type=F32)
    gfk = gfk_ref[...]
    for hd in range(HEADS):
        sl = slice(hd * LANES, (hd + 1) * LANES)
        v = fk[:, sl]
        v = v * lax.rsqrt(jnp.sum(v * v, axis=-1, keepdims=True) * (1.0 / FOX_DIM) + EPS) * gfk
        fk_out[:, sl] = (v + aug_k[:, sl]).astype(BF16)

    fqt = lax.dot_general(wfqt_ref[...], h, _NT, preferred_element_type=F32)
    gfq = _lane_tile(gfq_ref[...], t)
    for hd in range(HEADS):
        rows = slice(hd * LANES, (hd + 1) * LANES)
        blk = fqt[rows]
        r = lax.rsqrt(jnp.sum(blk * blk, axis=0, keepdims=True) * (1.0 / FOX_DIM) + EPS)
        fqt_out[0, rows, :] = (blk * r * gfq + aug_q[rows]).astype(BF16)


def _pre_attention(x2d, tables, w, seq):
    n, d = x2d.shape
    t = TOKEN_TILE
    tiles_per_seq = seq // t
    batch = n // seq
    cos_l, sin_l, cos_t, sin_t = tables
    half = MLA_ROPE // 2
    row = lambda i: (i, 0)
    seq_t = lambda i: (i // tiles_per_seq, 0, i % tiles_per_seq)

    def full(a):
        return pl.BlockSpec(a.shape, lambda i, nd=a.ndim: (0,) * nd)

    weights = [w["g_attn"], w["w_a"], w["w_fk"], w["w_fq_t"], w["w_fv_t"], w["b_fl"], w["tri_incl"],
               w["place_k"], w["place_q"], w["g_q_lora"], w["w_uq_t"], w["g_kv_lora"], w["w_uk"], w["w_uv_t"],
               w["g_mla_q"], w["g_mla_k"], w["g_fox_q"], w["g_fox_k"]]
    in_specs = [pl.BlockSpec((t, d), row), pl.BlockSpec((t, LANES), row), pl.BlockSpec((t, LANES), row),
                pl.BlockSpec((half, t), lambda i: (0, i)), pl.BlockSpec((half, t), lambda i: (0, i))]
    in_specs += [full(a) for a in weights]
    wq = HEADS * LANES
    wv = HEADS * HEAD_V
    out_shape = (jax.ShapeDtypeStruct((batch, wq, seq), BF16), jax.ShapeDtypeStruct((n, wq), BF16),
                 jax.ShapeDtypeStruct((batch, wv, seq), BF16), jax.ShapeDtypeStruct((batch, wq, seq), BF16),
                 jax.ShapeDtypeStruct((n, wq), BF16), jax.ShapeDtypeStruct((batch, wv, seq), BF16))
    out_specs = (pl.BlockSpec((1, wq, t), seq_t), pl.BlockSpec((t, wq), row), pl.BlockSpec((1, wv, t), seq_t),
                 pl.BlockSpec((1, wq, t), seq_t), pl.BlockSpec((t, wq), row), pl.BlockSpec((1, wv, t), seq_t))
    return pl.pallas_call(
        functools.partial(_pre_kernel, tiles_per_seq=tiles_per_seq),
        out_shape=out_shape,
        grid=(n // t,),
        in_specs=in_specs,
        out_specs=out_specs,
        scratch_shapes=[pltpu.VMEM((8, LANES), F32)],
        compiler_params=_cparams(1),
        name="pre_attention",
    )(x2d, cos_l, sin_l, cos_t, sin_t, *weights)


def _allowed(k0, q0, tk, tq, chunked):
    key = k0 + lax.broadcasted_iota(jnp.int32, (tk, tq), 0)
    qry = q0 + lax.broadcasted_iota(jnp.int32, (tk, tq), 1)
    if chunked:
        return (key // CHUNK) <= (qry // CHUNK)
    return key <= qry


def _rowmax_kernel(qt_ref, k_ref, m_ref, *, chunked, tq, tk):
    qi = pl.program_id(2)
    n_diag = tq // tk

    def block(kb, carry, masked):
        k0 = pl.multiple_of(kb * tk, tk)
        kblk = k_ref[0, pl.ds(k0, tk), :]
        out = []
        for i in range(2):
            s = jnp.dot(kblk[:, i * LANES:(i + 1) * LANES], qt_ref[0, i * LANES:(i + 1) * LANES, :],
                        preferred_element_type=F32)
            if masked:
                s = jnp.where(_allowed(k0, qi * tq, tk, tq, chunked), s, NEG_INF)
            out.append(jnp.maximum(carry[i], jnp.max(s, axis=0, keepdims=True)))
        return tuple(out)

    init = tuple(jnp.full((1, tq), NEG_INF, F32) for _ in range(2))
    carry = lax.fori_loop(0, qi * n_diag, lambda kb, c: block(kb, c, False), init)
    for j in range(n_diag):
        carry = block(qi * n_diag + j, carry, True)
    m_ref[0, 0] = jnp.concatenate(carry, axis=0)


def _row_max(qt, k, *, chunked):
    batch, seq, wk = k.shape
    tq, tk = ATTN_TQ, ATTN_TK
    pairs = wk // (2 * LANES)
    out = pl.pallas_call(
        functools.partial(_rowmax_kernel, chunked=chunked, tq=tq, tk=tk),
        out_shape=jax.ShapeDtypeStruct((batch, pairs, 2, seq), F32),
        grid=(batch, pairs, seq // tq),
        in_specs=[pl.BlockSpec((1, 2 * LANES, tq), lambda b, h, i: (b, h, i)),
                  pl.BlockSpec((1, seq, 2 * LANES), lambda b, h, i: (b, 0, h))],
        out_specs=pl.BlockSpec((1, 1, 2, tq), lambda b, h, i: (b, h, 0, i)),
        compiler_params=_cparams(3),
        name="attn_rowmax",
    )(qt, k)
    return out


def _attn_kernel(qt_ref, k_ref, vt_ref, shift_ref, o_ref, acc_ref, *, chunked, tq, tk):
    qi = pl.program_id(2)
    n_diag = tq // tk
    acc_ref[...] = jnp.zeros_like(acc_ref)
    ones = jnp.ones((SUM_ROWS, tk), BF16)

    def block(kb, masked):
        k0 = pl.multiple_of(kb * tk, tk)
        kblk = k_ref[0, pl.ds(k0, tk), :]
        vt = vt_ref[0, :, pl.ds(k0, tk)]
        for i in range(2):
            s = jnp.dot(kblk[:, i * LANES:(i + 1) * LANES], qt_ref[0, i * LANES:(i + 1) * LANES, :],
                        preferred_element_type=F32)
            s = s - shift_ref[0, 0, i:i + 1, :]
            if masked:
                s = jnp.where(_allowed(k0, qi * tq, tk, tq, chunked), s, NEG_INF)
            p = jnp.exp2(s).astype(BF16)
            lhs = jnp.concatenate([vt[i * HEAD_V:(i + 1) * HEAD_V], ones], axis=0)
            acc_ref[i] += jnp.dot(lhs, p, preferred_element_type=F32)

    def body(kb, c):
        block(kb, False)
        return c

    lax.fori_loop(0, qi * n_diag, body, 0)
    for j in range(n_diag):
        block(qi * n_diag + j, True)
    o_t = jnp.concatenate([acc_ref[i, :HEAD_V, :] / acc_ref[i, HEAD_V:HEAD_V + 1, :] for i in range(2)], axis=0)
    o_ref[0] = o_t.T.astype(o_ref.dtype)


def _attention(qt, k, vt, bound, *, chunked):
    batch, seq, wk = k.shape
    tq, tk = ATTN_TQ, ATTN_TK
    pairs = wk // (2 * LANES)
    shift = lax.cond(bound <= SAFE_SCORE_BOUND * LOG2E,
                     lambda: jnp.full((batch, pairs, 2, seq), bound, F32),
                     lambda: _row_max(qt, k, chunked=chunked))
    return pl.pallas_call(
        functools.partial(_attn_kernel, chunked=chunked, tq=tq, tk=tk),
        out_shape=jax.ShapeDtypeStruct((batch, seq, pairs * 2 * HEAD_V), BF16),
        grid=(batch, pairs, seq // tq),
        in_specs=[pl.BlockSpec((1, 2 * LANES, tq), lambda b, h, i: (b, h, i)),
                  pl.BlockSpec((1, seq, 2 * LANES), lambda b, h, i: (b, 0, h)),
                  pl.BlockSpec((1, 2 * HEAD_V, seq), lambda b, h, i: (b, h, 0)),
                  pl.BlockSpec((1, 1, 2, tq), lambda b, h, i: (b, h, 0, i))],
        out_specs=pl.BlockSpec((1, tq, 2 * HEAD_V), lambda b, h, i: (b, i, h)),
        scratch_shapes=[pltpu.VMEM((2, HEAD_V + SUM_ROWS, tq), F32)],
        compiler_params=_cparams(3),
        name="attn_chunk_causal" if chunked else "attn_frame_causal",
    )(qt, k, vt, shift)


def _post_kernel(x_ref, om_ref, of_ref, gom_ref, gof_ref, wom_ref, wof_ref, gffn_ref, wrh_ref, wrl_ref,
                 tri_ref, x1_out, h2_out, route_out, count_out, carry_ref):
    step = pl.program_id(0)
    ym = _rms(om_ref[...].astype(F32), gom_ref[...]).astype(BF16)
    yf = _rms(of_ref[...].astype(F32), gof_ref[...]).astype(BF16)
    x1 = (x_ref[...] + jnp.dot(ym, wom_ref[...], preferred_element_type=F32)
          + jnp.dot(yf, wof_ref[...], preferred_element_type=F32))
    x1_out[...] = x1
    h2 = _rms(x1, gffn_ref[...])
    h2_out[...] = h2

    h_hi = h2.astype(BF16)
    h_lo = (h2 - h_hi.astype(F32)).astype(BF16)
    w_hi = wrh_ref[...]
    logits = (jnp.dot(h_hi, w_hi, preferred_element_type=F32)
              + jnp.dot(h_lo, w_hi, preferred_element_type=F32)
              + jnp.dot(h_hi, wrl_ref[...], preferred_element_type=F32))

    lane = lax.broadcasted_iota(jnp.int32, (1, LANES), 1).astype(F32)
    big = float(LANES)

    def first_argmax(v):
        mx = jnp.max(v, axis=-1, keepdims=True)
        idx = jnp.min(jnp.where(v == mx, lane, big), axis=-1, keepdims=True)
        return mx, idx

    lg = jnp.where(lane < N_GROUPS, logits, NEG_INF)
    mg, g_idx = first_argmax(lg)
    p_g = 1.0 / jnp.sum(jnp.exp(lg - mg), axis=-1, keepdims=True)
    e_lo = ROUTE_LANE0 + EXPERTS_PER_GROUP * g_idx
    le = jnp.where((lane >= e_lo) & (lane < e_lo + EXPERTS_PER_GROUP), logits, NEG_INF)
    m1, i1 = first_argmax(le)
    m2, i2 = first_argmax(jnp.where(lane == i1, NEG_INF, le))
    e2 = jnp.exp(m2 - m1)
    gate1 = p_g / (1.0 + e2)
    gate2 = p_g * e2 / (1.0 + e2)

    oh1 = lane == i1
    oh2 = lane == i2
    oh = (jnp.where(oh1, 1.0, 0.0) + jnp.where(oh2, 1.0, 0.0))
    before = jnp.dot(tri_ref[...], oh.astype(BF16), preferred_element_type=F32)

    @pl.when(step == 0)
    def _():
        carry_ref[...] = jnp.zeros_like(carry_ref)

    base = before + carry_ref[:1, :]
    rank1 = jnp.sum(jnp.where(oh1, base, 0.0), axis=-1, keepdims=True)
    rank2 = jnp.sum(jnp.where(oh2, base, 0.0), axis=-1, keepdims=True)
    total = carry_ref[:1, :] + jnp.sum(oh, axis=0, keepdims=True)
    carry_ref[...] = jnp.broadcast_to(total, carry_ref.shape)
    count_out[...] = jnp.broadcast_to(total, count_out.shape)

    route = jnp.where(lane == 0, i1 - ROUTE_LANE0, 0.0)
    route = jnp.where(lane == 1, i2 - ROUTE_LANE0, route)
    route = jnp.where(lane == 2, rank1, route)
    route = jnp.where(lane == 3, rank2, route)
    route = jnp.where(lane == 4, gate1, route)
    route = jnp.where(lane == 5, gate2, route)
    route_out[...] = route


def _post_attention(x2d, o_mla, o_fox, w):
    n, d = x2d.shape
    t = TOKEN_TILE
    const = lambda i: (0, 0)
    row = lambda i: (i, 0)
    full = lambda a: pl.BlockSpec(a.shape, const)
    ins = [x2d, o_mla, o_fox, w["g_out_mla"], w["g_out_fox"], w["w_out_mla"], w["w_out_fox"],
           w["g_ffn"], w["w_router_hi"], w["w_router_lo"], w["tri_strict"]]
    in_specs = [pl.BlockSpec((t, d), row), pl.BlockSpec((t, o_mla.shape[1]), row),
                pl.BlockSpec((t, o_fox.shape[1]), row)] + [full(a) for a in ins[3:]]
    return pl.pallas_call(
        _post_kernel,
        out_shape=(jax.ShapeDtypeStruct((n, d), F32), jax.ShapeDtypeStruct((n, d), F32),
                   jax.ShapeDtypeStruct((n, LANES), F32), jax.ShapeDtypeStruct((8, LANES), F32)),
        grid=(n // t,),
        in_specs=in_specs,
        out_specs=(pl.BlockSpec((t, d), row), pl.BlockSpec((t, d), row),
                   pl.BlockSpec((t, LANES), row), pl.BlockSpec((8, LANES), const)),
        scratch_shapes=[pltpu.VMEM((8, LANES), F32)],
        compiler_params=_cparams(1),
        name="post_attention",
    )(*ins)


def _dispatch_kernel(dest_ref, h_ref, xs_in_hbm, xs_hbm, sem, *, tile):
    del xs_in_hbm

    def issue(j, c):
        for k in range(2):
            d = dest_ref[0, 0, 2 * j + k]
            pltpu.make_async_copy(h_ref.at[pl.ds(j, 1)], xs_hbm.at[pl.ds(d, 1)], sem).start()
        return c

    lax.fori_loop(0, tile, issue, 0)
    for _ in range(2):
        pltpu.make_async_copy(h_ref, xs_hbm.at[pl.ds(0, tile)], sem).wait()


def _dispatch(h2, dest, n_rows):
    n, d = h2.shape
    tile = DISPATCH_TILE
    dest3 = dest.reshape(n // tile, 1, 2 * tile)
    xs0 = jnp.zeros((n_rows, d), h2.dtype)
    return pl.pallas_call(
        functools.partial(_dispatch_kernel, tile=tile),
        out_shape=jax.ShapeDtypeStruct((n_rows, d), h2.dtype),
        grid=(n // tile,),
        in_specs=[pl.BlockSpec((1, 1, 2 * tile), lambda i: (i, 0, 0), memory_space=pltpu.SMEM),
                  pl.BlockSpec((tile, d), lambda i: (i, 0)),
                  pl.BlockSpec(memory_space=pl.ANY)],
        out_specs=pl.BlockSpec(memory_space=pl.ANY),
        scratch_shapes=[pltpu.SemaphoreType.DMA],
        input_output_aliases={2: 0},
        compiler_params=_cparams(1),
        name="moe_dispatch",
    )(dest3, h2, xs0)


def _expert_kernel(blk_e_ref, n_used_ref, x_ref, wgu_ref, wd_ref, y_ref):
    del blk_e_ref
    used = pl.program_id(0) < n_used_ref[0]

    @pl.when(used)
    def _():
        xb = x_ref[...].astype(BF16)
        gu = jnp.dot(xb, wgu_ref[0], preferred_element_type=F32)
        g = gu[:, :EXPERT_FF]
        act = (g / (1.0 + jnp.exp(-g))) * gu[:, EXPERT_FF:]
        y_ref[...] = jnp.dot(act.astype(BF16), wd_ref[0], preferred_element_type=F32)

    @pl.when(jnp.logical_not(used))
    def _():
        y_ref[...] = jnp.zeros_like(y_ref)


def _experts(xs, blk_e, n_used, w_gu, w_down):
    p, d = xs.shape
    r = EXPERT_BLOCK
    rows = lambda i, be, nu: (jnp.minimum(i, nu[0] - 1), 0)
    wsel = lambda i, be, nu: (be[i], 0, 0)
    return pl.pallas_call(
        _expert_kernel,
        out_shape=jax.ShapeDtypeStruct((p, d), F32),
        grid_spec=pltpu.PrefetchScalarGridSpec(
            num_scalar_prefetch=2,
            grid=(p // r,),
            in_specs=[pl.BlockSpec((r, d), rows),
                      pl.BlockSpec((1, d, 2 * EXPERT_FF), wsel),
                      pl.BlockSpec((1, EXPERT_FF, d), wsel)],
            out_specs=pl.BlockSpec((r, d), lambda i, be, nu: (i, 0))),
        compiler_params=_cparams(1),
        name="moe_experts",
    )(blk_e, n_used, xs, w_gu, w_down)


def _ple_kernel(dest_ref, dest_next_ref, x1_ref, route_ref, p_ref, gnorm_ref, wgate_ref, wproj_ref, gout_ref,
                ys_hbm, x_out, rows_ref, sem, *, tile):
    step = pl.program_id(0)
    n_steps = pl.num_programs(0)

    def row_copy(d, slot, k, j):
        return pltpu.make_async_copy(ys_hbm.at[pl.ds(d, 1)], rows_ref.at[slot, k, pl.ds(j, 1)], sem.at[slot])

    def gather(idx_ref, slot):
        def issue(j, c):
            for k in range(2):
                row_copy(idx_ref[0, 0, 2 * j + k], slot, k, j).start()
            return c
        lax.fori_loop(0, tile, issue, 0)

    @pl.when(step == 0)
    def _():
        gather(dest_ref, 0)

    @pl.when(step + 1 < n_steps)
    def _():
        gather(dest_next_ref, (step + 1) % 2)

    slot = step % 2

    def drain(j, c):
        for k in range(2):
            row_copy(0, slot, k, j).wait()
        return c

    lax.fori_loop(0, tile, drain, 0)

    route = route_ref[...]
    x2 = x1_ref[...] + route[:, 4:5] * rows_ref[slot, 0] + route[:, 5:6] * rows_ref[slot, 1]
    ple = _rms(jnp.dot(p_ref[...].astype(BF16), wproj_ref[...], preferred_element_type=F32), gout_ref[...])
    z = jnp.dot(_rms(x2, gnorm_ref[...]).astype(BF16), wgate_ref[...], preferred_element_type=F32)
    x_out[...] = x2 + ple / (1.0 + jnp.exp(-z))


def _combine_ple(x1, route, dest, ys, p2d, w):
    n, d = x1.shape
    t = PLE_TILE
    steps = n // t
    const = lambda i: (0, 0)
    row = lambda i: (i, 0)
    full = lambda a: pl.BlockSpec(a.shape, const)
    dest3 = dest.reshape(steps, 1, 2 * t)
    return pl.pallas_call(
        functools.partial(_ple_kernel, tile=t),
        out_shape=jax.ShapeDtypeStruct((n, d), F32),
        grid=(steps,),
        in_specs=[pl.BlockSpec((1, 1, 2 * t), lambda i: (i, 0, 0), memory_space=pltpu.SMEM),
                  pl.BlockSpec((1, 1, 2 * t), lambda i: (jnp.minimum(i + 1, steps - 1), 0, 0),
                               memory_space=pltpu.SMEM),
                  pl.BlockSpec((t, d), row), pl.BlockSpec((t, LANES), row),
                  pl.BlockSpec((t, p2d.shape[1]), row),
                  full(w["g_ple_norm"]), full(w["w_ple_gate"]), full(w["w_ple_proj"]), full(w["g_ple_out"]),
                  pl.BlockSpec(memory_space=pl.ANY)],
        out_specs=pl.BlockSpec((t, d), row),
        scratch_shapes=[pltpu.VMEM((2, 2, t, d), F32), pltpu.SemaphoreType.DMA((2,))],
        compiler_params=_cparams(1),
        name="moe_combine_ple",
    )(dest3, dest3, x1, route, p2d, w["g_ple_norm"], w["w_ple_gate"], w["w_ple_proj"], w["g_ple_out"], ys)


def _row(v):
    return v.reshape(1, -1).astype(F32)


def _col_rep(v):
    return jnp.broadcast_to(v.astype(F32)[:, None], (v.shape[0], LANES))


def _pad_heads(wmat, real):
    k = wmat.shape[0]
    return jnp.pad(wmat.reshape(k, HEADS, real), ((0, 0), (0, 0), (0, LANES - real))).reshape(k, HEADS * LANES)


def _placement():
    place_k = np.zeros((3, LANES, HEADS * LANES), np.float32)
    place_q = np.zeros((3, HEADS * LANES, LANES), np.float32)
    for hd in range(HEADS):
        base = hd * LANES
        for j in range(3):
            place_k[j, hd, base + AUG_K + j] = -1.0
            place_q[j, base + AUG_Q + j, hd] = 1.0
            place_k[0, ONE_LANE, base + AUG_Q + j] = 1.0
            place_q[0, base + AUG_K + j, ONE_LANE] = 1.0
    return jnp.asarray(place_k, BF16), jnp.asarray(place_q, BF16)


def _layer_weights(i, g_attn_norm, w_in, g_q_lora, w_uq, g_kv_lora, w_ukv, g_mla_q, g_mla_k, g_fox_q,
                   g_fox_k, b_fox_f, g_out_mla, g_out_fox, w_out, g_ffn_norm, w_router_group,
                   w_router_expert, w_exp_gate, w_exp_up, w_exp_down, g_ple_norm, w_ple_gate, w_ple_proj,
                   g_ple_out):
    d = w_in.shape[1]
    wf = HEADS * FOX_DIM
    c_kv = Q_LORA
    c_pe = c_kv + KV_LORA
    c_fq = c_pe + MLA_ROPE
    c_fk = c_fq + wf
    c_fv = c_fk + wf
    c_fl = c_fv + wf
    win = w_in[i]
    pad_pe = jnp.zeros((d, LANES), F32).at[:, MLA_NOPE:MLA_QK].set(win[:, c_pe:c_fq])
    pad_fl = jnp.zeros((d, LANES), F32).at[:, :HEADS].set(win[:, c_fl:])
    w = {}
    w["g_attn"] = _row(g_attn_norm[i])
    w["w_a"] = jnp.concatenate([win[:, :c_pe], pad_pe, pad_fl], axis=1).astype(BF16)
    w["w_fk"] = _pad_heads(win[:, c_fk:c_fv], FOX_DIM).astype(BF16)
    w["w_fq_t"] = _pad_heads(win[:, c_fq:c_fk], FOX_DIM).T.astype(BF16)
    w["w_fv_t"] = win[:, c_fv:c_fl].T.astype(BF16)
    w["b_fl"] = _row(jnp.pad(b_fox_f[i], (0, LANES - HEADS)))
    w["place_k"], w["place_q"] = _placement()
    w["g_q_lora"] = _row(g_q_lora[i])
    w["w_uq_t"] = _pad_heads(w_uq[i], MLA_QK).T.astype(BF16)
    w["g_kv_lora"] = _row(g_kv_lora[i])
    ukv = w_ukv[i].reshape(KV_LORA, HEADS, MLA_NOPE + HEAD_V)
    w["w_uk"] = _pad_heads(ukv[:, :, :MLA_NOPE].reshape(KV_LORA, -1), MLA_NOPE).astype(BF16)
    w["w_uv_t"] = ukv[:, :, MLA_NOPE:].reshape(KV_LORA, -1).T.astype(BF16)
    w["g_mla_q"] = _col_rep(jnp.pad(g_mla_q[i], (0, LANES - MLA_QK))) * (MLA_QK ** -0.5 * LOG2E)
    w["g_mla_k"] = _row(jnp.pad(g_mla_k[i], (0, LANES - MLA_QK)))
    w["g_fox_q"] = _col_rep(jnp.pad(g_fox_q[i], (0, LANES - FOX_DIM))) * (FOX_DIM ** -0.5 * LOG2E)
    w["g_fox_k"] = _row(jnp.pad(g_fox_k[i], (0, LANES - FOX_DIM)))
    bound = lambda gq, gk, dim: (1.02 * LOG2E * dim ** 0.5) * jnp.max(jnp.abs(gq)) * jnp.max(jnp.abs(gk))
    w["bound_mla"] = bound(g_mla_q[i], g_mla_k[i], MLA_QK).astype(F32)
    w["bound_fox"] = bound(g_fox_q[i], g_fox_k[i], FOX_DIM).astype(F32)
    w["g_out_mla"] = _row(g_out_mla[i])
    w["g_out_fox"] = _row(g_out_fox[i])
    wm = HEADS * HEAD_V
    w["w_out_mla"] = w_out[i, :wm].astype(BF16)
    w["w_out_fox"] = w_out[i, wm:].astype(BF16)
    w["g_ffn"] = _row(g_ffn_norm[i])
    wr = jnp.zeros((d, LANES), F32)
    wr = wr.at[:, :N_GROUPS].set(w_router_group[i]).at[:, ROUTE_LANE0:ROUTE_LANE0 + N_EXPERTS].set(
        w_router_expert[i])
    w["w_router_hi"] = wr.astype(BF16)
    w["w_router_lo"] = (wr - w["w_router_hi"].astype(F32)).astype(BF16)
    w["w_gu"] = jnp.concatenate([w_exp_gate[i], w_exp_up[i]], axis=-1).astype(BF16)
    w["w_down"] = w_exp_down[i].astype(BF16)
    w["g_ple_norm"] = _row(g_ple_norm[i])
    w["w_ple_gate"] = w_ple_gate[i].astype(BF16)
    w["w_ple_proj"] = w_ple_proj[i].astype(BF16)
    w["g_ple_out"] = _row(g_ple_out[i])
    t = TOKEN_TILE
    r = lax.broadcasted_iota(jnp.int32, (t, t), 0)
    c = lax.broadcasted_iota(jnp.int32, (t, t), 1)
    w["tri_incl"] = (c <= r).astype(BF16)
    w["tri_strict"] = (c < r).astype(BF16)
    return w


def _route_plan(route, counts, n):
    r = EXPERT_BLOCK
    cnt = counts[0, ROUTE_LANE0:ROUTE_LANE0 + N_EXPERTS].astype(jnp.int32)
    padded = ((cnt + r - 1) // r) * r
    pad_end = jnp.cumsum(padded)
    pad_start = pad_end - padded
    e = route[:, 0:2].astype(jnp.int32)
    rank = route[:, 2:4].astype(jnp.int32)
    dest = pad_start[e] + rank
    n_rows = 2 * n + N_EXPERTS * r
    blk_start = jnp.arange(n_rows // r, dtype=jnp.int32) * r
    blk_e = jnp.sum((blk_start[:, None] >= pad_end[None, :]).astype(jnp.int32), axis=1)
    blk_e = jnp.minimum(blk_e, N_EXPERTS - 1)
    n_used = (pad_end[-1:] // r).astype(jnp.int32)
    return dest.reshape(-1), blk_e, n_used, n_rows


def kernel(x, p, positions, g_attn_norm, w_in, g_q_lora, w_uq, g_kv_lora, w_ukv, g_mla_q, g_mla_k, g_fox_q,
           g_fox_k, b_fox_f, g_out_mla, g_out_fox, w_out, g_ffn_norm, w_router_group, w_router_expert,
           w_exp_gate, w_exp_up, w_exp_down, g_ple_norm, w_ple_gate, w_ple_proj, g_ple_out):
    batch, seq, d = x.shape
    n = batch * seq
    depth = w_in.shape[0]
    params = (g_attn_norm, w_in, g_q_lora, w_uq, g_kv_lora, w_ukv, g_mla_q, g_mla_k, g_fox_q, g_fox_k,
              b_fox_f, g_out_mla, g_out_fox, w_out, g_ffn_norm, w_router_group, w_router_expert,
              w_exp_gate, w_exp_up, w_exp_down, g_ple_norm, w_ple_gate, w_ple_proj, g_ple_out)
    tables = _rope_tables(positions)
    xc = x.reshape(n, d)
    for i in range(depth):
        w = _layer_weights(i, *params)
        qt, k, vt, fqt, fk, fvt = _pre_attention(xc, tables, w, seq)
        b3 = lambda a: a.reshape(batch, seq, a.shape[-1])
        o_mla = _attention(qt, b3(k), vt, w["bound_mla"], chunked=True).reshape(n, -1)
        o_fox = _attention(fqt, b3(fk), fvt, w["bound_fox"], chunked=False).reshape(n, -1)
        x1, h2, route, counts = _post_attention(xc, o_mla, o_fox, w)
        dest, blk_e, n_used, n_rows = _route_plan(route, counts, n)
        xs = _dispatch(h2, dest, n_rows)
        ys = _experts(xs, blk_e, n_used, w["w_gu"], w["w_down"])
        xc = _combine_ple(x1, route, dest, ys, p[i].reshape(n, -1), w)
    return xc.reshape(batch, seq, d)
```

```python
import functools
import math

import numpy as np
import jax
import jax.numpy as jnp
from jax import lax
from jax.experimental import pallas as pl
from jax.experimental.pallas import tpu as pltpu

F32 = jnp.float32
BF16 = jnp.bfloat16

EPS = 1e-6
NEG_INF = -1e30
ROPE_THETA = 10000.0
LOG2E = math.log2(math.e)

LANES = 128
CHUNK = 64
HEADS = 8
MLA_NOPE = 64
MLA_ROPE = 32
MLA_QK = MLA_NOPE + MLA_ROPE
HEAD_V = 64
FOX_DIM = 64
Q_LORA = 256
KV_LORA = 128
N_GROUPS = 4
EXPERTS_PER_GROUP = 8
N_EXPERTS = N_GROUPS * EXPERTS_PER_GROUP
EXPERT_FF = 256
ROUTE_LANE0 = N_GROUPS

AUG_Q = FOX_DIM
AUG_K = FOX_DIM + 3
ONE_LANE = HEADS

TOKEN_TILE = 512
ATTN_TQ = 1024
ATTN_TK = 1024
SUM_ROWS = 16
EXPERT_BLOCK = 256
DISPATCH_TILE = 1024
PLE_TILE = 256
ISSUE_UNROLL = 8
VMEM_LIMIT = 56 * 1024 * 1024
SAFE_SCORE_BOUND = 40.0
SKIP_LOG2 = 160.0

_NT = (((1,), (1,)), ((), ()))


def _cparams(n_axes):
    return pltpu.CompilerParams(dimension_semantics=("arbitrary",) * n_axes,
                                vmem_limit_bytes=VMEM_LIMIT)


def _rms(x, g):
    return x * lax.rsqrt(jnp.mean(x * x, axis=-1, keepdims=True) + EPS) * g


def _split3(x):
    hi = x.astype(BF16)
    r1 = x - hi.astype(F32)
    mid = r1.astype(BF16)
    lo = (r1 - mid.astype(F32)).astype(BF16)
    return hi, mid, lo


def _lane_tile(a, width):
    return jnp.tile(a, (1, width // LANES))


def _rope_kernel(pos_col_ref, pos_row_ref, invf_lane_ref, sign_ref, invf_rep_ref,
                 cos_ref, sin_ref, cost_ref, sint_ref):
    ang = pos_col_ref[...] * invf_lane_ref[...]
    cos_ref[...] = jnp.cos(ang)
    sin_ref[...] = jnp.sin(ang) * sign_ref[...]
    t = pos_row_ref.shape[-1]
    ang_t = _lane_tile(invf_rep_ref[...], t) * pos_row_ref[0]
    cost_ref[...] = jnp.cos(ang_t)
    sint_ref[...] = jnp.sin(ang_t)


def _rope_tables(positions):
    n = positions.size
    half = MLA_ROPE // 2
    inv_freq = ROPE_THETA ** (-np.arange(0, MLA_ROPE, 2, dtype=np.float32) / MLA_ROPE)
    invf = np.zeros((1, LANES), np.float32)
    sign = np.zeros((1, LANES), np.float32)
    invf[0, MLA_NOPE:MLA_NOPE + half] = inv_freq
    invf[0, MLA_NOPE + half:MLA_QK] = inv_freq
    sign[0, MLA_NOPE:MLA_NOPE + half] = -1.0
    sign[0, MLA_NOPE + half:MLA_QK] = 1.0
    invf_rep = np.broadcast_to(inv_freq[:, None], (half, LANES)).astype(np.float32)
    pos = positions.astype(F32)
    t = TOKEN_TILE
    const = lambda i: (0, 0)
    return pl.pallas_call(
        _rope_kernel,
        out_shape=(jax.ShapeDtypeStruct((n, LANES), F32), jax.ShapeDtypeStruct((n, LANES), F32),
                   jax.ShapeDtypeStruct((half, n), F32), jax.ShapeDtypeStruct((half, n), F32)),
        grid=(n // t,),
        in_specs=[pl.BlockSpec((t, 1), lambda i: (i, 0)),
                  pl.BlockSpec((1, 1, t), lambda i: (i, 0, 0)),
                  pl.BlockSpec((1, LANES), const), pl.BlockSpec((1, LANES), const),
                  pl.BlockSpec((half, LANES), const)],
        out_specs=(pl.BlockSpec((t, LANES), lambda i: (i, 0)), pl.BlockSpec((t, LANES), lambda i: (i, 0)),
                   pl.BlockSpec((half, t), lambda i: (0, i)), pl.BlockSpec((half, t), lambda i: (0, i))),
        compiler_params=_cparams(1),
        name="rope_tables",
    )(pos.reshape(n, 1), pos.reshape(n // t, 1, t), jnp.asarray(invf), jnp.asarray(sign),
      jnp.asarray(invf_rep))


def _pre_kernel(x_ref, cos_ref, sin_ref, cost_ref, sint_ref, gattn_ref, wa_ref, wfk_ref, wfqt_ref, wfvt_ref,
                bfl_ref, tri_ref, plk_ref, plq_ref, gql_ref, wuqt_ref, gkvl_ref, wuk_ref, wuvt_ref,
                gq_ref, gk_ref, gfq_ref, gfk_ref,
                qt_out, k_out, vt_out, fqt_out, fk_out, fvt_out, fcum_out, carry_ref, *, tiles_per_seq):
    step = pl.program_id(0)
    x = x_ref[...]
    t = x.shape[0]
    h = _rms(x, gattn_ref[...]).astype(BF16)
    pa = jnp.dot(h, wa_ref[...], preferred_element_type=F32)
    half = MLA_ROPE // 2
    lane = lax.broadcasted_iota(jnp.int32, (1, LANES), 1)

    qn = _rms(pa[:, :Q_LORA], gql_ref[...]).astype(BF16)
    kvn = _rms(pa[:, Q_LORA:Q_LORA + KV_LORA], gkvl_ref[...]).astype(BF16)
    vt_out[0] = lax.dot_general(wuvt_ref[...], kvn, _NT, preferred_element_type=F32).astype(BF16)

    kn = jnp.dot(kvn, wuk_ref[...], preferred_element_type=F32)
    kpe = pa[:, Q_LORA + KV_LORA:Q_LORA + KV_LORA + LANES]
    cos_l = cos_ref[...]
    sin_l = sin_ref[...]
    gk = gk_ref[...]
    for hd in range(HEADS):
        sl = slice(hd * LANES, (hd + 1) * LANES)
        v = kn[:, sl] + kpe
        v = v * lax.rsqrt(jnp.sum(v * v, axis=-1, keepdims=True) * (1.0 / MLA_QK) + EPS) * gk
        swapped = jnp.where(lane < MLA_NOPE + half, pltpu.roll(v, LANES - half, 1), pltpu.roll(v, half, 1))
        k_out[:, sl] = (v * cos_l + swapped * sin_l).astype(BF16)

    qt = lax.dot_general(wuqt_ref[...], qn, _NT, preferred_element_type=F32)
    cos_r = cost_ref[...]
    sin_r = sint_ref[...]
    gq = _lane_tile(gq_ref[...], t)
    x1s = slice(MLA_NOPE, MLA_NOPE + half)
    x2s = slice(MLA_NOPE + half, MLA_QK)
    for hd in range(HEADS):
        blk = qt[hd * LANES:(hd + 1) * LANES]
        r = lax.rsqrt(jnp.sum(blk * blk, axis=0, keepdims=True) * (1.0 / MLA_QK) + EPS)
        blk = blk * r * gq
        x1, x2 = blk[x1s], blk[x2s]
        blk = jnp.concatenate([blk[:MLA_NOPE], x1 * cos_r - x2 * sin_r, x2 * cos_r + x1 * sin_r,
                               blk[MLA_QK:]], axis=0)
        qt_out[0, hd * LANES:(hd + 1) * LANES, :] = blk.astype(BF16)

    fvt_out[0] = lax.dot_general(wfvt_ref[...], h, _NT, preferred_element_type=F32).astype(BF16)

    z = pa[:, Q_LORA + KV_LORA + LANES:] + bfl_ref[...]
    logf = jnp.minimum(z, 0.0) - jnp.log1p(jnp.exp(-jnp.abs(z)))
    tri = tri_ref[...]
    cum = sum(jnp.dot(tri, part, preferred_element_type=F32) for part in _split3(logf))

    @pl.when(step % tiles_per_seq == 0)
    def _():
        carry_ref[...] = jnp.zeros_like(carry_ref)

    cum = cum + carry_ref[:1, :]
    carry_ref[...] = jnp.broadcast_to(cum[t - 1:t, :], carry_ref.shape)
    cum2 = cum * LOG2E
    fcum_out[...] = cum2
    p1, p2, p3 = _split3(cum2)
    p1 = jnp.where(lane == ONE_LANE, jnp.ones_like(p1), p1)
    pieces = (p1, p2, p3)
    aug_k = sum(jnp.dot(pc, plk_ref[j], preferred_element_type=F32) for j, pc in enumerate(pieces))
    aug_q = sum(lax.dot_general(plq_ref[j], pc, _NT, preferred_element_type=F32)
                for j, pc in enumerate(pieces))

    fk = jnp.dot(h, wfk_ref[...], preferred_element_type=F32)
    gfk = gfk_ref[...]
    for hd in range(HEADS):
        sl = slice(hd * LANES, (hd + 1) * LANES)
        v = fk[:, sl]
        v = v * lax.rsqrt(jnp.sum(v * v, axis=-1, keepdims=True) * (1.0 / FOX_DIM) + EPS) * gfk
        fk_out[:, sl] = (v + aug_k[:, sl]).astype(BF16)

    fqt = lax.dot_general(wfqt_ref[...], h, _NT, preferred_element_type=F32)
    gfq = _lane_tile(gfq_ref[...], t)
    for hd in range(HEADS):
        rows = slice(hd * LANES, (hd + 1) * LANES)
        blk = fqt[rows]
        r = lax.rsqrt(jnp.sum(blk * blk, axis=0, keepdims=True) * (1.0 / FOX_DIM) + EPS)
        fqt_out[0, rows, :] = (blk * r * gfq + aug_q[rows]).astype(BF16)


def _pre_attention(x2d, tables, w, seq):
    n, d = x2d.shape
    t = TOKEN_TILE
    tiles_per_seq = seq // t
    batch = n // seq
    cos_l, sin_l, cos_t, sin_t = tables
    half = MLA_ROPE // 2
    row = lambda i: (i, 0)
    seq_t = lambda i: (i // tiles_per_seq, 0, i % tiles_per_seq)

    def full(a):
        return pl.BlockSpec(a.shape, lambda i, nd=a.ndim: (0,) * nd)

    weights = [w["g_attn"], w["w_a"], w["w_fk"], w["w_fq_t"], w["w_fv_t"], w["b_fl"], w["tri_incl"],
               w["place_k"], w["place_q"], w["g_q_lora"], w["w_uq_t"], w["g_kv_lora"], w["w_uk"], w["w_uv_t"],
               w["g_mla_q"], w["g_mla_k"], w["g_fox_q"], w["g_fox_k"]]
    in_specs = [pl.BlockSpec((t, d), row), pl.BlockSpec((t, LANES), row), pl.BlockSpec((t, LANES), row),
                pl.BlockSpec((half, t), lambda i: (0, i)), pl.BlockSpec((half, t), lambda i: (0, i))]
    in_specs += [full(a) for a in weights]
    wq = HEADS * LANES
    wv = HEADS * HEAD_V
    out_shape = (jax.ShapeDtypeStruct((batch, wq, seq), BF16), jax.ShapeDtypeStruct((n, wq), BF16),
                 jax.ShapeDtypeStruct((batch, wv, seq), BF16), jax.ShapeDtypeStruct((batch, wq, seq), BF16),
                 jax.ShapeDtypeStruct((n, wq), BF16), jax.ShapeDtypeStruct((batch, wv, seq), BF16),
                 jax.ShapeDtypeStruct((n, LANES), F32))
    out_specs = (pl.BlockSpec((1, wq, t), seq_t), pl.BlockSpec((t, wq), row), pl.BlockSpec((1, wv, t), seq_t),
                 pl.BlockSpec((1, wq, t), seq_t), pl.BlockSpec((t, wq), row), pl.BlockSpec((1, wv, t), seq_t),
                 pl.BlockSpec((t, LANES), row))
    return pl.pallas_call(
        functools.partial(_pre_kernel, tiles_per_seq=tiles_per_seq),
        out_shape=out_shape,
        grid=(n // t,),
        in_specs=in_specs,
        out_specs=out_specs,
        scratch_shapes=[pltpu.VMEM((8, LANES), F32)],
        compiler_params=_cparams(1),
        name="pre_attention",
    )(x2d, cos_l, sin_l, cos_t, sin_t, *weights)


def _allowed(k0, q0, tk, tq, chunked):
    key = k0 + lax.broadcasted_iota(jnp.int32, (tk, tq), 0)
    qry = q0 + lax.broadcasted_iota(jnp.int32, (tk, tq), 1)
    if chunked:
        return (key // CHUNK) <= (qry // CHUNK)
    return key <= qry


def _rowmax_kernel(qt_ref, k_ref, m_ref, *, chunked, tq, tk):
    qi = pl.program_id(2)
    n_diag = tq // tk

    def block(kb, carry, masked):
        k0 = pl.multiple_of(kb * tk, tk)
        kblk = k_ref[0, pl.ds(k0, tk), :]
        out = []
        for i in range(2):
            s = jnp.dot(kblk[:, i * LANES:(i + 1) * LANES], qt_ref[0, i * LANES:(i + 1) * LANES, :],
                        preferred_element_type=F32)
            if masked:
                s = jnp.where(_allowed(k0, qi * tq, tk, tq, chunked), s, NEG_INF)
            out.append(jnp.maximum(carry[i], jnp.max(s, axis=0, keepdims=True)))
        return tuple(out)

    init = tuple(jnp.full((1, tq), NEG_INF, F32) for _ in range(2))
    carry = lax.fori_loop(0, qi * n_diag, lambda kb, c: block(kb, c, False), init)
    for j in range(n_diag):
        carry = block(qi * n_diag + j, carry, True)
    m_ref[0, 0] = jnp.concatenate(carry, axis=0)


def _row_max(qt, k, *, chunked):
    batch, seq, wk = k.shape
    tq, tk = ATTN_TQ, ATTN_TK
    pairs = wk // (2 * LANES)
    out = pl.pallas_call(
        functools.partial(_rowmax_kernel, chunked=chunked, tq=tq, tk=tk),
        out_shape=jax.ShapeDtypeStruct((batch, pairs, 2, seq), F32),
        grid=(batch, pairs, seq // tq),
        in_specs=[pl.BlockSpec((1, 2 * LANES, tq), lambda b, h, i: (b, h, i)),
                  pl.BlockSpec((1, seq, 2 * LANES), lambda b, h, i: (b, 0, h))],
        out_specs=pl.BlockSpec((1, 1, 2, tq), lambda b, h, i: (b, h, 0, i)),
        compiler_params=_cparams(3),
        name="attn_rowmax",
    )(qt, k)
    return out


def _attn_kernel(start_ref, qt_ref, k_ref, vt_ref, shift_ref, mask_ref, o_ref, acc_ref, *, tq, tk):
    pairs = pl.num_programs(1)
    n_q = pl.num_programs(2)
    qi = pl.program_id(2)
    n_diag = tq // tk
    acc_ref[...] = jnp.zeros_like(acc_ref)
    ones = jnp.ones((SUM_ROWS, tk), BF16)

    def block(kb, diag):
        k0 = pl.multiple_of(kb * tk, tk)
        kblk = k_ref[0, pl.ds(k0, tk), :]
        vt = vt_ref[0, :, pl.ds(k0, tk)]
        for i in range(2):
            s = jnp.dot(kblk[:, i * LANES:(i + 1) * LANES], qt_ref[0, i * LANES:(i + 1) * LANES, :],
                        preferred_element_type=F32)
            s = s - shift_ref[0, 0, i:i + 1, :]
            if diag is not None:
                s = s + mask_ref[diag]
            p = jnp.exp2(s).astype(BF16)
            lhs = jnp.concatenate([vt[i * HEAD_V:(i + 1) * HEAD_V], ones], axis=0)
            acc_ref[i] += jnp.dot(lhs, p, preferred_element_type=F32)

    def body(kb, c):
        block(kb, None)
        return c

    first = start_ref[(pl.program_id(0) * pairs + pl.program_id(1)) * n_q + qi]
    lax.fori_loop(first, qi * n_diag, body, 0)
    for j in range(n_diag):
        block(qi * n_diag + j, j)
    o_t = jnp.concatenate([acc_ref[i, :HEAD_V, :] / acc_ref[i, HEAD_V:HEAD_V + 1, :] for i in range(2)], axis=0)
    o_ref[0] = o_t.T.astype(o_ref.dtype)


def _diag_mask(tq, tk, chunked):
    key = np.arange(tq)[:, None]
    qry = np.arange(tq)[None, :]
    ok = (key // CHUNK) <= (qry // CHUNK) if chunked else key <= qry
    return jnp.asarray(np.where(ok, 0.0, NEG_INF).astype(np.float32).reshape(tq // tk, tk, tq))


def _first_block(fcum, batch, seq, tq, tk):
    f = fcum.reshape(batch, seq, LANES)[:, :, :HEADS]
    f_q0 = f[:, ::tq, :]
    f_kl = f[:, tk - 1::tk, :]
    dead = (f_q0[:, :, None, :] - f_kl[:, None, :, :]) < -SKIP_LOG2
    dead = dead.reshape(batch, seq // tq, seq // tk, HEADS // 2, 2).all(axis=-1)
    lead = jnp.cumprod(dead.astype(jnp.int32), axis=2).sum(axis=2)
    limit = (jnp.arange(seq // tq, dtype=jnp.int32) * (tq // tk))[None, :, None]
    return jnp.minimum(lead, limit).transpose(0, 2, 1).reshape(-1).astype(jnp.int32)


def _attention(qt, k, vt, bound, fcum, *, chunked):
    batch, seq, wk = k.shape
    tq, tk = ATTN_TQ, ATTN_TK
    pairs = wk // (2 * LANES)
    n_q = seq // tq
    fast = bound <= SAFE_SCORE_BOUND * LOG2E
    shift = lax.cond(fast, lambda: jnp.full((batch, pairs, 2, seq), bound, F32),
                     lambda: _row_max(qt, k, chunked=chunked))
    first = jnp.zeros((batch * pairs * n_q,), jnp.int32)
    if fcum is not None:
        first = jnp.where(fast, _first_block(fcum, batch, seq, tq, tk), first)
    return pl.pallas_call(
        functools.partial(_attn_kernel, tq=tq, tk=tk),
        out_shape=jax.ShapeDtypeStruct((batch, seq, pairs * 2 * HEAD_V), BF16),
        grid_spec=pltpu.PrefetchScalarGridSpec(
            num_scalar_prefetch=1,
            grid=(batch, pairs, n_q),
            in_specs=[pl.BlockSpec((1, 2 * LANES, tq), lambda b, h, i, st: (b, h, i)),
                      pl.BlockSpec((1, seq, 2 * LANES), lambda b, h, i, st: (b, 0, h)),
                      pl.BlockSpec((1, 2 * HEAD_V, seq), lambda b, h, i, st: (b, h, 0)),
                      pl.BlockSpec((1, 1, 2, tq), lambda b, h, i, st: (b, h, 0, i)),
                      pl.BlockSpec((tq // tk, tk, tq), lambda b, h, i, st: (0, 0, 0))],
            out_specs=pl.BlockSpec((1, tq, 2 * HEAD_V), lambda b, h, i, st: (b, i, h)),
            scratch_shapes=[pltpu.VMEM((2, HEAD_V + SUM_ROWS, tq), F32)]),
        compiler_params=_cparams(3),
        name="attn_chunk_causal" if chunked else "attn_frame_causal",
    )(first, qt, k, vt, shift, _diag_mask(tq, tk, chunked))


def _post_kernel(x_ref, om_ref, of_ref, gom_ref, gof_ref, wom_ref, wof_ref, gffn_ref, wrh_ref, wrl_ref,
                 tri_ref, x1_out, h2_out, route_out, count_out, carry_ref):
    step = pl.program_id(0)
    ym = _rms(om_ref[...].astype(F32), gom_ref[...]).astype(BF16)
    yf = _rms(of_ref[...].astype(F32), gof_ref[...]).astype(BF16)
    x1 = (x_ref[...] + jnp.dot(ym, wom_ref[...], preferred_element_type=F32)
          + jnp.dot(yf, wof_ref[...], preferred_element_type=F32))
    x1_out[...] = x1
    h2 = _rms(x1, gffn_ref[...])
    h2_out[...] = h2

    h_hi = h2.astype(BF16)
    h_lo = (h2 - h_hi.astype(F32)).astype(BF16)
    w_hi = wrh_ref[...]
    logits = (jnp.dot(h_hi, w_hi, preferred_element_type=F32)
              + jnp.dot(h_lo, w_hi, preferred_element_type=F32)
              + jnp.dot(h_hi, wrl_ref[...], preferred_element_type=F32))

    lane = lax.broadcasted_iota(jnp.int32, (1, LANES), 1).astype(F32)
    big = float(LANES)

    def first_argmax(v):
        mx = jnp.max(v, axis=-1, keepdims=True)
        idx = jnp.min(jnp.where(v == mx, lane, big), axis=-1, keepdims=True)
        return mx, idx

    lg = jnp.where(lane < N_GROUPS, logits, NEG_INF)
    mg, g_idx = first_argmax(lg)
    p_g = 1.0 / jnp.sum(jnp.exp(lg - mg), axis=-1, keepdims=True)
    e_lo = ROUTE_LANE0 + EXPERTS_PER_GROUP * g_idx
    le = jnp.where((lane >= e_lo) & (lane < e_lo + EXPERTS_PER_GROUP), logits, NEG_INF)
    m1, i1 = first_argmax(le)
    m2, i2 = first_argmax(jnp.where(lane == i1, NEG_INF, le))
    e2 = jnp.exp(m2 - m1)
    gate1 = p_g / (1.0 + e2)
    gate2 = p_g * e2 / (1.0 + e2)

    oh1 = lane == i1
    oh2 = lane == i2
    oh = (jnp.where(oh1, 1.0, 0.0) + jnp.where(oh2, 1.0, 0.0))
    before = jnp.dot(tri_ref[...], oh.astype(BF16), preferred_element_type=F32)

    @pl.when(step == 0)
    def _():
        carry_ref[...] = jnp.zeros_like(carry_ref)

    base = before + carry_ref[:1, :]
    rank1 = jnp.sum(jnp.where(oh1, base, 0.0), axis=-1, keepdims=True)
    rank2 = jnp.sum(jnp.where(oh2, base, 0.0), axis=-1, keepdims=True)
    total = carry_ref[:1, :] + jnp.sum(oh, axis=0, keepdims=True)
    carry_ref[...] = jnp.broadcast_to(total, carry_ref.shape)
    count_out[...] = jnp.broadcast_to(total, count_out.shape)

    route = jnp.where(lane == 0, i1 - ROUTE_LANE0, 0.0)
    route = jnp.where(lane == 1, i2 - ROUTE_LANE0, route)
    route = jnp.where(lane == 2, rank1, route)
    route = jnp.where(lane == 3, rank2, route)
    route = jnp.where(lane == 4, gate1, route)
    route = jnp.where(lane == 5, gate2, route)
    route_out[...] = route


def _post_attention(x2d, o_mla, o_fox, w):
    n, d = x2d.shape
    t = TOKEN_TILE
    const = lambda i: (0, 0)
    row = lambda i: (i, 0)
    full = lambda a: pl.BlockSpec(a.shape, const)
    ins = [x2d, o_mla, o_fox, w["g_out_mla"], w["g_out_fox"], w["w_out_mla"], w["w_out_fox"],
           w["g_ffn"], w["w_router_hi"], w["w_router_lo"], w["tri_strict"]]
    in_specs = [pl.BlockSpec((t, d), row), pl.BlockSpec((t, o_mla.shape[1]), row),
                pl.BlockSpec((t, o_fox.shape[1]), row)] + [full(a) for a in ins[3:]]
    return pl.pallas_call(
        _post_kernel,
        out_shape=(jax.ShapeDtypeStruct((n, d), F32), jax.ShapeDtypeStruct((n, d), F32),
                   jax.ShapeDtypeStruct((n, LANES), F32), jax.ShapeDtypeStruct((8, LANES), F32)),
        grid=(n // t,),
        in_specs=in_specs,
        out_specs=(pl.BlockSpec((t, d), row), pl.BlockSpec((t, d), row),
                   pl.BlockSpec((t, LANES), row), pl.BlockSpec((8, LANES), const)),
        scratch_shapes=[pltpu.VMEM((8, LANES), F32)],
        compiler_params=_cparams(1),
        name="post_attention",
    )(*ins)


def _dispatch_kernel(pad_end_ref, padded_ref, n_used_ref, dest_ref, h_ref, xs_hbm, zero_ref, sem, zsem, *, tile):
    rblk = zero_ref.shape[0]
    n_blk = xs_hbm.shape[0] // rblk

    def zero_copy(blk):
        return pltpu.make_async_copy(zero_ref, xs_hbm.at[pl.ds(pl.multiple_of(blk * rblk, rblk), rblk)], zsem)

    @pl.when(pl.program_id(0) == 0)
    def _():
        zero_ref[...] = jnp.zeros_like(zero_ref)
        for wait in (False, True):
            for e in range(N_EXPERTS):
                @pl.when(padded_ref[e] > 0)
                def _():
                    cp = zero_copy(pad_end_ref[e] // rblk - 1)
                    cp.wait() if wait else cp.start()

            def tail(blk, c):
                cp = zero_copy(blk)
                cp.wait() if wait else cp.start()
                return c

            lax.fori_loop(n_used_ref[0], n_blk, tail, 0)

    def issue(jo, c):
        for ji in range(ISSUE_UNROLL):
            j = jo * ISSUE_UNROLL + ji
            for k in range(2):
                d = dest_ref[0, 0, 2 * j + k]
                pltpu.make_async_copy(h_ref.at[pl.ds(j, 1)], xs_hbm.at[pl.ds(d, 1)], sem).start()
        return c

    lax.fori_loop(0, tile // ISSUE_UNROLL, issue, 0)
    for _ in range(2):
        pltpu.make_async_copy(h_ref, xs_hbm.at[pl.ds(0, tile)], sem).wait()


def _dispatch(h2, plan):
    n, d = h2.shape
    tile = DISPATCH_TILE
    dest3 = plan["dest"].reshape(n // tile, 1, 2 * tile)
    return pl.pallas_call(
        functools.partial(_dispatch_kernel, tile=tile),
        out_shape=jax.ShapeDtypeStruct((plan["n_rows"], d), h2.dtype),
        grid_spec=pltpu.PrefetchScalarGridSpec(
            num_scalar_prefetch=3,
            grid=(n // tile,),
            in_specs=[pl.BlockSpec((1, 1, 2 * tile), lambda i, *_: (i, 0, 0), memory_space=pltpu.SMEM),
                      pl.BlockSpec((tile, d), lambda i, *_: (i, 0))],
            out_specs=pl.BlockSpec(memory_space=pl.ANY),
            scratch_shapes=[pltpu.VMEM((EXPERT_BLOCK, d), h2.dtype), pltpu.SemaphoreType.DMA,
                            pltpu.SemaphoreType.DMA]),
        compiler_params=_cparams(1),
        name="moe_dispatch",
    )(plan["pad_end"], plan["padded"], plan["n_used"], dest3, h2)


def _expert_kernel(blk_e_ref, n_used_ref, x_ref, wgu_ref, wd_ref, y_ref):
    del blk_e_ref
    used = pl.program_id(0) < n_used_ref[0]

    @pl.when(used)
    def _():
        xb = x_ref[...].astype(BF16)
        gu = jnp.dot(xb, wgu_ref[0], preferred_element_type=F32)
        g = gu[:, :EXPERT_FF]
        act = (g / (1.0 + jnp.exp(-g))) * gu[:, EXPERT_FF:]
        y_ref[...] = jnp.dot(act.astype(BF16), wd_ref[0], preferred_element_type=F32)

    @pl.when(jnp.logical_not(used))
    def _():
        y_ref[...] = jnp.zeros_like(y_ref)


def _experts(xs, blk_e, n_used, w_gu, w_down):
    p, d = xs.shape
    r = EXPERT_BLOCK
    rows = lambda i, be, nu: (jnp.minimum(i, nu[0] - 1), 0)
    wsel = lambda i, be, nu: (be[i], 0, 0)
    return pl.pallas_call(
        _expert_kernel,
        out_shape=jax.ShapeDtypeStruct((p, d), F32),
        grid_spec=pltpu.PrefetchScalarGridSpec(
            num_scalar_prefetch=2,
            grid=(p // r,),
            in_specs=[pl.BlockSpec((r, d), rows),
                      pl.BlockSpec((1, d, 2 * EXPERT_FF), wsel),
                      pl.BlockSpec((1, EXPERT_FF, d), wsel)],
            out_specs=pl.BlockSpec((r, d), lambda i, be, nu: (i, 0))),
        compiler_params=_cparams(1),
        name="moe_experts",
    )(blk_e, n_used, xs, w_gu, w_down)


def _ple_kernel(dest_ref, dest_next_ref, x1_ref, route_ref, p_ref, gnorm_ref, wgate_ref, wproj_ref, gout_ref,
                ys_hbm, x_out, rows_ref, sem, *, tile):
    step = pl.program_id(0)
    n_steps = pl.num_programs(0)

    def row_copy(d, slot, k, j):
        return pltpu.make_async_copy(ys_hbm.at[pl.ds(d, 1)], rows_ref.at[slot, k, pl.ds(j, 1)], sem.at[slot])

    def gather(idx_ref, slot):
        def issue(jo, c):
            for ji in range(ISSUE_UNROLL):
                j = jo * ISSUE_UNROLL + ji
                for k in range(2):
                    row_copy(idx_ref[0, 0, 2 * j + k], slot, k, j).start()
            return c
        lax.fori_loop(0, tile // ISSUE_UNROLL, issue, 0)

    @pl.when(step == 0)
    def _():
        gather(dest_ref, 0)

    @pl.when(step + 1 < n_steps)
    def _():
        gather(dest_next_ref, (step + 1) % 2)

    slot = step % 2

    for k in range(2):
        pltpu.make_async_copy(ys_hbm.at[pl.ds(0, tile)], rows_ref.at[slot, k], sem.at[slot]).wait()

    route = route_ref[...]
    x2 = x1_ref[...] + route[:, 4:5] * rows_ref[slot, 0] + route[:, 5:6] * rows_ref[slot, 1]
    ple = _rms(jnp.dot(p_ref[...].astype(BF16), wproj_ref[...], preferred_element_type=F32), gout_ref[...])
    z = jnp.dot(_rms(x2, gnorm_ref[...]).astype(BF16), wgate_ref[...], preferred_element_type=F32)
    x_out[...] = x2 + ple / (1.0 + jnp.exp(-z))


def _combine_ple(x1, route, dest, ys, p2d, w):
    n, d = x1.shape
    t = PLE_TILE
    steps = n // t
    const = lambda i: (0, 0)
    row = lambda i: (i, 0)
    full = lambda a: pl.BlockSpec(a.shape, const)
    dest3 = dest.reshape(steps, 1, 2 * t)
    return pl.pallas_call(
        functools.partial(_ple_kernel, tile=t),
        out_shape=jax.ShapeDtypeStruct((n, d), F32),
        grid=(steps,),
        in_specs=[pl.BlockSpec((1, 1, 2 * t), lambda i: (i, 0, 0), memory_space=pltpu.SMEM),
                  pl.BlockSpec((1, 1, 2 * t), lambda i: (jnp.minimum(i + 1, steps - 1), 0, 0),
                               memory_space=pltpu.SMEM),
                  pl.BlockSpec((t, d), row), pl.BlockSpec((t, LANES), row),
                  pl.BlockSpec((t, p2d.shape[1]), row),
                  full(w["g_ple_norm"]), full(w["w_ple_gate"]), full(w["w_ple_proj"]), full(w["g_ple_out"]),
                  pl.BlockSpec(memory_space=pl.ANY)],
        out_specs=pl.BlockSpec((t, d), row),
        scratch_shapes=[pltpu.VMEM((2, 2, t, d), F32), pltpu.SemaphoreType.DMA((2,))],
        compiler_params=_cparams(1),
        name="moe_combine_ple",
    )(dest3, dest3, x1, route, p2d, w["g_ple_norm"], w["w_ple_gate"], w["w_ple_proj"], w["g_ple_out"], ys)


def _row(v):
    return v.reshape(1, -1).astype(F32)


def _col_rep(v):
    return jnp.broadcast_to(v.astype(F32)[:, None], (v.shape[0], LANES))


def _pad_heads(wmat, real):
    k = wmat.shape[0]
    return jnp.pad(wmat.reshape(k, HEADS, real), ((0, 0), (0, 0), (0, LANES - real))).reshape(k, HEADS * LANES)


def _placement():
    place_k = np.zeros((3, LANES, HEADS * LANES), np.float32)
    place_q = np.zeros((3, HEADS * LANES, LANES), np.float32)
    for hd in range(HEADS):
        base = hd * LANES
        for j in range(3):
            place_k[j, hd, base + AUG_K + j] = -1.0
            place_q[j, base + AUG_Q + j, hd] = 1.0
            place_k[0, ONE_LANE, base + AUG_Q + j] = 1.0
            place_q[0, base + AUG_K + j, ONE_LANE] = 1.0
    return jnp.asarray(place_k, BF16), jnp.asarray(place_q, BF16)


def _layer_weights(i, g_attn_norm, w_in, g_q_lora, w_uq, g_kv_lora, w_ukv, g_mla_q, g_mla_k, g_fox_q,
                   g_fox_k, b_fox_f, g_out_mla, g_out_fox, w_out, g_ffn_norm, w_router_group,
                   w_router_expert, w_exp_gate, w_exp_up, w_exp_down, g_ple_norm, w_ple_gate, w_ple_proj,
                   g_ple_out):
    d = w_in.shape[1]
    wf = HEADS * FOX_DIM
    c_kv = Q_LORA
    c_pe = c_kv + KV_LORA
    c_fq = c_pe + MLA_ROPE
    c_fk = c_fq + wf
    c_fv = c_fk + wf
    c_fl = c_fv + wf
    win = w_in[i]
    pad_pe = jnp.zeros((d, LANES), F32).at[:, MLA_NOPE:MLA_QK].set(win[:, c_pe:c_fq])
    pad_fl = jnp.zeros((d, LANES), F32).at[:, :HEADS].set(win[:, c_fl:])
    w = {}
    w["g_attn"] = _row(g_attn_norm[i])
    w["w_a"] = jnp.concatenate([win[:, :c_pe], pad_pe, pad_fl], axis=1).astype(BF16)
    w["w_fk"] = _pad_heads(win[:, c_fk:c_fv], FOX_DIM).astype(BF16)
    w["w_fq_t"] = _pad_heads(win[:, c_fq:c_fk], FOX_DIM).T.astype(BF16)
    w["w_fv_t"] = win[:, c_fv:c_fl].T.astype(BF16)
    w["b_fl"] = _row(jnp.pad(b_fox_f[i], (0, LANES - HEADS)))
    w["place_k"], w["place_q"] = _placement()
    w["g_q_lora"] = _row(g_q_lora[i])
    w["w_uq_t"] = _pad_heads(w_uq[i], MLA_QK).T.astype(BF16)
    w["g_kv_lora"] = _row(g_kv_lora[i])
    ukv = w_ukv[i].reshape(KV_LORA, HEADS, MLA_NOPE + HEAD_V)
    w["w_uk"] = _pad_heads(ukv[:, :, :MLA_NOPE].reshape(KV_LORA, -1), MLA_NOPE).astype(BF16)
    w["w_uv_t"] = ukv[:, :, MLA_NOPE:].reshape(KV_LORA, -1).T.astype(BF16)
    w["g_mla_q"] = _col_rep(jnp.pad(g_mla_q[i], (0, LANES - MLA_QK))) * (MLA_QK ** -0.5 * LOG2E)
    w["g_mla_k"] = _row(jnp.pad(g_mla_k[i], (0, LANES - MLA_QK)))
    w["g_fox_q"] = _col_rep(jnp.pad(g_fox_q[i], (0, LANES - FOX_DIM))) * (FOX_DIM ** -0.5 * LOG2E)
    w["g_fox_k"] = _row(jnp.pad(g_fox_k[i], (0, LANES - FOX_DIM)))
    bound = lambda gq, gk, dim: (1.02 * LOG2E * dim ** 0.5) * jnp.max(jnp.abs(gq)) * jnp.max(jnp.abs(gk))
    w["bound_mla"] = bound(g_mla_q[i], g_mla_k[i], MLA_QK).astype(F32)
    w["bound_fox"] = bound(g_fox_q[i], g_fox_k[i], FOX_DIM).astype(F32)
    w["g_out_mla"] = _row(g_out_mla[i])
    w["g_out_fox"] = _row(g_out_fox[i])
    wm = HEADS * HEAD_V
    w["w_out_mla"] = w_out[i, :wm].astype(BF16)
    w["w_out_fox"] = w_out[i, wm:].astype(BF16)
    w["g_ffn"] = _row(g_ffn_norm[i])
    wr = jnp.zeros((d, LANES), F32)
    wr = wr.at[:, :N_GROUPS].set(w_router_group[i]).at[:, ROUTE_LANE0:ROUTE_LANE0 + N_EXPERTS].set(
        w_router_expert[i])
    w["w_router_hi"] = wr.astype(BF16)
    w["w_router_lo"] = (wr - w["w_router_hi"].astype(F32)).astype(BF16)
    w["w_gu"] = jnp.concatenate([w_exp_gate[i], w_exp_up[i]], axis=-1).astype(BF16)
    w["w_down"] = w_exp_down[i].astype(BF16)
    w["g_ple_norm"] = _row(g_ple_norm[i])
    w["w_ple_gate"] = w_ple_gate[i].astype(BF16)
    w["w_ple_proj"] = w_ple_proj[i].astype(BF16)
    w["g_ple_out"] = _row(g_ple_out[i])
    t = TOKEN_TILE
    r = lax.broadcasted_iota(jnp.int32, (t, t), 0)
    c = lax.broadcasted_iota(jnp.int32, (t, t), 1)
    w["tri_incl"] = (c <= r).astype(BF16)
    w["tri_strict"] = (c < r).astype(BF16)
    return w


def _route_plan(route, counts, n):
    r = EXPERT_BLOCK
    cnt = counts[0, ROUTE_LANE0:ROUTE_LANE0 + N_EXPERTS].astype(jnp.int32)
    padded = ((cnt + r - 1) // r) * r
    pad_end = jnp.cumsum(padded)
    pad_start = pad_end - padded
    e = route[:, 0:2].astype(jnp.int32)
    rank = route[:, 2:4].astype(jnp.int32)
    onehot = e[:, :, None] == jnp.arange(N_EXPERTS, dtype=jnp.int32)[None, None, :]
    dest = jnp.sum(jnp.where(onehot, pad_start[None, None, :], 0), axis=-1) + rank
    n_rows = 2 * n + N_EXPERTS * r
    blk_start = jnp.arange(n_rows // r, dtype=jnp.int32) * r
    blk_e = jnp.sum((blk_start[:, None] >= pad_end[None, :]).astype(jnp.int32), axis=1)
    blk_e = jnp.minimum(blk_e, N_EXPERTS - 1)
    n_used = (pad_end[-1:] // r).astype(jnp.int32)
    return {"dest": dest.reshape(-1), "blk_e": blk_e, "n_used": n_used, "n_rows": n_rows,
            "pad_end": pad_end.astype(jnp.int32), "padded": padded.astype(jnp.int32)}


def kernel(x, p, positions, g_attn_norm, w_in, g_q_lora, w_uq, g_kv_lora, w_ukv, g_mla_q, g_mla_k, g_fox_q,
           g_fox_k, b_fox_f, g_out_mla, g_out_fox, w_out, g_ffn_norm, w_router_group, w_router_expert,
           w_exp_gate, w_exp_up, w_exp_down, g_ple_norm, w_ple_gate, w_ple_proj, g_ple_out):
    batch, seq, d = x.shape
    n = batch * seq
    depth = w_in.shape[0]
    params = (g_attn_norm, w_in, g_q_lora, w_uq, g_kv_lora, w_ukv, g_mla_q, g_mla_k, g_fox_q, g_fox_k,
              b_fox_f, g_out_mla, g_out_fox, w_out, g_ffn_norm, w_router_group, w_router_expert,
              w_exp_gate, w_exp_up, w_exp_down, g_ple_norm, w_ple_gate, w_ple_proj, g_ple_out)
    tables = _rope_tables(positions)
    xc = x.reshape(n, d)
    for i in range(depth):
        w = _layer_weights(i, *params)
        qt, k, vt, fqt, fk, fvt, fcum = _pre_attention(xc, tables, w, seq)
        b3 = lambda a: a.reshape(batch, seq, a.shape[-1])
        o_mla = _attention(qt, b3(k), vt, w["bound_mla"], None, chunked=True).reshape(n, -1)
        o_fox = _attention(fqt, b3(fk), fvt, w["bound_fox"], fcum, chunked=False).reshape(n, -1)
        x1, h2, route, counts = _post_attention(xc, o_mla, o_fox, w)
        plan = _route_plan(route, counts, n)
        xs = _dispatch(h2, plan)
        ys = _experts(xs, plan["blk_e"], plan["n_used"], w["w_gu"], w["w_down"])
        xc = _combine_ple(x1, route, plan["dest"], ys, p[i].reshape(n, -1), w)
    return xc.reshape(batch, seq, d)
```

```python
import functools
import math

import numpy as np
import jax
import jax.numpy as jnp
from jax import lax
from jax.experimental import pallas as pl
from jax.experimental.pallas import tpu as pltpu

F32 = jnp.float32
BF16 = jnp.bfloat16

EPS = 1e-6
NEG_INF = -1e30
ROPE_THETA = 10000.0
LOG2E = math.log2(math.e)

LANES = 128
CHUNK = 64
HEADS = 8
MLA_NOPE = 64
MLA_ROPE = 32
MLA_QK = MLA_NOPE + MLA_ROPE
HEAD_V = 64
FOX_DIM = 64
Q_LORA = 256
KV_LORA = 128
N_GROUPS = 4
EXPERTS_PER_GROUP = 8
N_EXPERTS = N_GROUPS * EXPERTS_PER_GROUP
EXPERT_FF = 256
ROUTE_LANE0 = N_GROUPS

AUG_Q = FOX_DIM
AUG_K = FOX_DIM + 3
ONE_LANE = HEADS

TOKEN_TILE = 512
POST_TILE = 1024
ATTN_TQ = 1024
ATTN_TK = 1024
SUM_ROWS = 16
EXPERT_BLOCK = 256
DISPATCH_TILE = 1024
PLE_TILE = 256
ISSUE_UNROLL = 8
VMEM_LIMIT = 56 * 1024 * 1024
SAFE_SCORE_BOUND = 40.0
SKIP_LOG2 = 160.0

_NT = (((1,), (1,)), ((), ()))


def _cparams(n_axes):
    return pltpu.CompilerParams(dimension_semantics=("arbitrary",) * n_axes,
                                vmem_limit_bytes=VMEM_LIMIT)


def _rms(x, g):
    return x * lax.rsqrt(jnp.mean(x * x, axis=-1, keepdims=True) + EPS) * g


def _split3(x):
    hi = x.astype(BF16)
    r1 = x - hi.astype(F32)
    mid = r1.astype(BF16)
    lo = (r1 - mid.astype(F32)).astype(BF16)
    return hi, mid, lo


def _store_row_tiles(ref, v):
    parts = v.shape[1] // LANES
    for s in range(parts):
        ref[pl.ds(s, v.shape[0], stride=parts), :] = v[:, s * LANES:(s + 1) * LANES]


def _load_row_tiles(ref, rows, parts):
    return jnp.concatenate([ref[pl.ds(s, rows, stride=parts), :] for s in range(parts)], axis=1)


def _lane_tile(a, width):
    return jnp.tile(a, (1, width // LANES))


def _rope_kernel(pos_col_ref, pos_row_ref, invf_lane_ref, sign_ref, invf_rep_ref,
                 cos_ref, sin_ref, cost_ref, sint_ref):
    ang = pos_col_ref[...] * invf_lane_ref[...]
    cos_ref[...] = jnp.cos(ang)
    sin_ref[...] = jnp.sin(ang) * sign_ref[...]
    t = pos_row_ref.shape[-1]
    ang_t = _lane_tile(invf_rep_ref[...], t) * pos_row_ref[0]
    cost_ref[...] = jnp.cos(ang_t)
    sint_ref[...] = jnp.sin(ang_t)


def _rope_tables(positions):
    n = positions.size
    half = MLA_ROPE // 2
    inv_freq = ROPE_THETA ** (-np.arange(0, MLA_ROPE, 2, dtype=np.float32) / MLA_ROPE)
    invf = np.zeros((1, LANES), np.float32)
    sign = np.zeros((1, LANES), np.float32)
    invf[0, MLA_NOPE:MLA_NOPE + half] = inv_freq
    invf[0, MLA_NOPE + half:MLA_QK] = inv_freq
    sign[0, MLA_NOPE:MLA_NOPE + half] = -1.0
    sign[0, MLA_NOPE + half:MLA_QK] = 1.0
    invf_rep = np.broadcast_to(inv_freq[:, None], (half, LANES)).astype(np.float32)
    pos = positions.astype(F32)
    t = TOKEN_TILE
    const = lambda i: (0, 0)
    return pl.pallas_call(
        _rope_kernel,
        out_shape=(jax.ShapeDtypeStruct((n, LANES), F32), jax.ShapeDtypeStruct((n, LANES), F32),
                   jax.ShapeDtypeStruct((half, n), F32), jax.ShapeDtypeStruct((half, n), F32)),
        grid=(n // t,),
        in_specs=[pl.BlockSpec((t, 1), lambda i: (i, 0)),
                  pl.BlockSpec((1, 1, t), lambda i: (i, 0, 0)),
                  pl.BlockSpec((1, LANES), const), pl.BlockSpec((1, LANES), const),
                  pl.BlockSpec((half, LANES), const)],
        out_specs=(pl.BlockSpec((t, LANES), lambda i: (i, 0)), pl.BlockSpec((t, LANES), lambda i: (i, 0)),
                   pl.BlockSpec((half, t), lambda i: (0, i)), pl.BlockSpec((half, t), lambda i: (0, i))),
        compiler_params=_cparams(1),
        name="rope_tables",
    )(pos.reshape(n, 1), pos.reshape(n // t, 1, t), jnp.asarray(invf), jnp.asarray(sign),
      jnp.asarray(invf_rep))


def _pre_kernel(x_ref, cos_ref, sin_ref, cost_ref, sint_ref, gattn_ref, wa_ref, wfk_ref, wfqt_ref, wfvt_ref,
                bfl_ref, tri_ref, plk_ref, plq_ref, gql_ref, wuqt_ref, gkvl_ref, wuk_ref, wuvt_ref,
                gq_ref, gk_ref, gfq_ref, gfk_ref,
                qt_out, k_out, vt_out, fqt_out, fk_out, fvt_out, fcum_out, carry_ref, *, tiles_per_seq):
    step = pl.program_id(0)
    x = x_ref[...]
    t = x.shape[0]
    h = _rms(x, gattn_ref[...]).astype(BF16)
    pa = jnp.dot(h, wa_ref[...], preferred_element_type=F32)
    half = MLA_ROPE // 2
    lane = lax.broadcasted_iota(jnp.int32, (1, LANES), 1)

    qn = _rms(pa[:, :Q_LORA], gql_ref[...]).astype(BF16)
    kvn = _rms(pa[:, Q_LORA:Q_LORA + KV_LORA], gkvl_ref[...]).astype(BF16)
    vt_out[0] = lax.dot_general(wuvt_ref[...], kvn, _NT, preferred_element_type=F32).astype(BF16)

    kn = jnp.dot(kvn, wuk_ref[...], preferred_element_type=F32)
    kpe = pa[:, Q_LORA + KV_LORA:Q_LORA + KV_LORA + LANES]
    cos_l = cos_ref[...]
    sin_l = sin_ref[...]
    gk = gk_ref[...]
    for hd in range(HEADS):
        sl = slice(hd * LANES, (hd + 1) * LANES)
        v = kn[:, sl] + kpe
        v = v * lax.rsqrt(jnp.sum(v * v, axis=-1, keepdims=True) * (1.0 / MLA_QK) + EPS) * gk
        swapped = jnp.where(lane < MLA_NOPE + half, pltpu.roll(v, LANES - half, 1), pltpu.roll(v, half, 1))
        k_out[:, sl] = (v * cos_l + swapped * sin_l).astype(BF16)

    qt = lax.dot_general(wuqt_ref[...], qn, _NT, preferred_element_type=F32)
    cos_r = cost_ref[...]
    sin_r = sint_ref[...]
    gq = _lane_tile(gq_ref[...], t)
    x1s = slice(MLA_NOPE, MLA_NOPE + half)
    x2s = slice(MLA_NOPE + half, MLA_QK)
    for hd in range(HEADS):
        blk = qt[hd * LANES:(hd + 1) * LANES]
        r = lax.rsqrt(jnp.sum(blk * blk, axis=0, keepdims=True) * (1.0 / MLA_QK) + EPS)
        blk = blk * r * gq
        x1, x2 = blk[x1s], blk[x2s]
        blk = jnp.concatenate([blk[:MLA_NOPE], x1 * cos_r - x2 * sin_r, x2 * cos_r + x1 * sin_r,
                               blk[MLA_QK:]], axis=0)
        qt_out[0, hd * LANES:(hd + 1) * LANES, :] = blk.astype(BF16)

    fvt_out[0] = lax.dot_general(wfvt_ref[...], h, _NT, preferred_element_type=F32).astype(BF16)

    z = pa[:, Q_LORA + KV_LORA + LANES:] + bfl_ref[...]
    logf = jnp.minimum(z, 0.0) - jnp.log1p(jnp.exp(-jnp.abs(z)))
    tri = tri_ref[...]
    cum = sum(jnp.dot(tri, part, preferred_element_type=F32) for part in _split3(logf))

    @pl.when(step % tiles_per_seq == 0)
    def _():
        carry_ref[...] = jnp.zeros_like(carry_ref)

    cum = cum + carry_ref[:1, :]
    carry_ref[...] = jnp.broadcast_to(cum[t - 1:t, :], carry_ref.shape)
    cum2 = cum * LOG2E
    fcum_out[...] = cum2
    p1, p2, p3 = _split3(cum2)
    p1 = jnp.where(lane == ONE_LANE, jnp.ones_like(p1), p1)
    pieces = (p1, p2, p3)
    aug_k = sum(jnp.dot(pc, plk_ref[j], preferred_element_type=F32) for j, pc in enumerate(pieces))
    aug_q = sum(lax.dot_general(plq_ref[j], pc, _NT, preferred_element_type=F32)
                for j, pc in enumerate(pieces))

    fk = jnp.dot(h, wfk_ref[...], preferred_element_type=F32)
    gfk = gfk_ref[...]
    for hd in range(HEADS):
        sl = slice(hd * LANES, (hd + 1) * LANES)
        v = fk[:, sl]
        v = v * lax.rsqrt(jnp.sum(v * v, axis=-1, keepdims=True) * (1.0 / FOX_DIM) + EPS) * gfk
        fk_out[:, sl] = (v + aug_k[:, sl]).astype(BF16)

    fqt = lax.dot_general(wfqt_ref[...], h, _NT, preferred_element_type=F32)
    gfq = _lane_tile(gfq_ref[...], t)
    for hd in range(HEADS):
        rows = slice(hd * LANES, (hd + 1) * LANES)
        blk = fqt[rows]
        r = lax.rsqrt(jnp.sum(blk * blk, axis=0, keepdims=True) * (1.0 / FOX_DIM) + EPS)
        fqt_out[0, rows, :] = (blk * r * gfq + aug_q[rows]).astype(BF16)


def _pre_attention(x2d, tables, w, seq):
    n, d = x2d.shape
    t = TOKEN_TILE
    tiles_per_seq = seq // t
    batch = n // seq
    cos_l, sin_l, cos_t, sin_t = tables
    half = MLA_ROPE // 2
    row = lambda i: (i, 0)
    seq_t = lambda i: (i // tiles_per_seq, 0, i % tiles_per_seq)

    def full(a):
        return pl.BlockSpec(a.shape, lambda i, nd=a.ndim: (0,) * nd)

    weights = [w["g_attn"], w["w_a"], w["w_fk"], w["w_fq_t"], w["w_fv_t"], w["b_fl"], w["tri_incl"],
               w["place_k"], w["place_q"], w["g_q_lora"], w["w_uq_t"], w["g_kv_lora"], w["w_uk"], w["w_uv_t"],
               w["g_mla_q"], w["g_mla_k"], w["g_fox_q"], w["g_fox_k"]]
    in_specs = [pl.BlockSpec((t, d), row), pl.BlockSpec((t, LANES), row), pl.BlockSpec((t, LANES), row),
                pl.BlockSpec((half, t), lambda i: (0, i)), pl.BlockSpec((half, t), lambda i: (0, i))]
    in_specs += [full(a) for a in weights]
    wq = HEADS * LANES
    wv = HEADS * HEAD_V
    out_shape = (jax.ShapeDtypeStruct((batch, wq, seq), BF16), jax.ShapeDtypeStruct((n, wq), BF16),
                 jax.ShapeDtypeStruct((batch, wv, seq), BF16), jax.ShapeDtypeStruct((batch, wq, seq), BF16),
                 jax.ShapeDtypeStruct((n, wq), BF16), jax.ShapeDtypeStruct((batch, wv, seq), BF16),
                 jax.ShapeDtypeStruct((n, LANES), F32))
    out_specs = (pl.BlockSpec((1, wq, t), seq_t), pl.BlockSpec((t, wq), row), pl.BlockSpec((1, wv, t), seq_t),
                 pl.BlockSpec((1, wq, t), seq_t), pl.BlockSpec((t, wq), row), pl.BlockSpec((1, wv, t), seq_t),
                 pl.BlockSpec((t, LANES), row))
    return pl.pallas_call(
        functools.partial(_pre_kernel, tiles_per_seq=tiles_per_seq),
        out_shape=out_shape,
        grid=(n // t,),
        in_specs=in_specs,
        out_specs=out_specs,
        scratch_shapes=[pltpu.VMEM((8, LANES), F32)],
        compiler_params=_cparams(1),
        name="pre_attention",
    )(x2d, cos_l, sin_l, cos_t, sin_t, *weights)


def _allowed(k0, q0, tk, tq, chunked):
    key = k0 + lax.broadcasted_iota(jnp.int32, (tk, tq), 0)
    qry = q0 + lax.broadcasted_iota(jnp.int32, (tk, tq), 1)
    if chunked:
        return (key // CHUNK) <= (qry // CHUNK)
    return key <= qry


def _rowmax_kernel(qt_ref, k_ref, m_ref, *, chunked, tq, tk):
    qi = pl.program_id(2)
    n_diag = tq // tk

    def block(kb, carry, masked):
        k0 = pl.multiple_of(kb * tk, tk)
        kblk = k_ref[0, pl.ds(k0, tk), :]
        out = []
        for i in range(2):
            s = jnp.dot(kblk[:, i * LANES:(i + 1) * LANES], qt_ref[0, i * LANES:(i + 1) * LANES, :],
                        preferred_element_type=F32)
            if masked:
                s = jnp.where(_allowed(k0, qi * tq, tk, tq, chunked), s, NEG_INF)
            out.append(jnp.maximum(carry[i], jnp.max(s, axis=0, keepdims=True)))
        return tuple(out)

    init = tuple(jnp.full((1, tq), NEG_INF, F32) for _ in range(2))
    carry = lax.fori_loop(0, qi * n_diag, lambda kb, c: block(kb, c, False), init)
    for j in range(n_diag):
        carry = block(qi * n_diag + j, carry, True)
    m_ref[0, 0] = jnp.concatenate(carry, axis=0)


def _row_max(qt, k, *, chunked):
    batch, seq, wk = k.shape
    tq, tk = ATTN_TQ, ATTN_TK
    pairs = wk // (2 * LANES)
    out = pl.pallas_call(
        functools.partial(_rowmax_kernel, chunked=chunked, tq=tq, tk=tk),
        out_shape=jax.ShapeDtypeStruct((batch, pairs, 2, seq), F32),
        grid=(batch, pairs, seq // tq),
        in_specs=[pl.BlockSpec((1, 2 * LANES, tq), lambda b, h, i: (b, h, i)),
                  pl.BlockSpec((1, seq, 2 * LANES), lambda b, h, i: (b, 0, h))],
        out_specs=pl.BlockSpec((1, 1, 2, tq), lambda b, h, i: (b, h, 0, i)),
        compiler_params=_cparams(3),
        name="attn_rowmax",
    )(qt, k)
    return out


def _attn_kernel(start_ref, qt_ref, k_ref, vt_ref, shift_ref, mask_ref, o_ref, acc_ref, *, tq, tk):
    pairs = pl.num_programs(1)
    n_q = pl.num_programs(2)
    qi = pl.program_id(2)
    n_diag = tq // tk
    acc_ref[...] = jnp.zeros_like(acc_ref)
    ones = jnp.ones((SUM_ROWS, tk), BF16)

    def block(kb, diag):
        k0 = pl.multiple_of(kb * tk, tk)
        kblk = k_ref[0, pl.ds(k0, tk), :]
        vt = vt_ref[0, :, pl.ds(k0, tk)]
        for i in range(2):
            s = jnp.dot(kblk[:, i * LANES:(i + 1) * LANES], qt_ref[0, i * LANES:(i + 1) * LANES, :],
                        preferred_element_type=F32)
            s = s - shift_ref[0, 0, i:i + 1, :]
            if diag is not None:
                s = s + mask_ref[diag]
            p = jnp.exp2(s).astype(BF16)
            lhs = jnp.concatenate([vt[i * HEAD_V:(i + 1) * HEAD_V], ones], axis=0)
            acc_ref[i] += jnp.dot(lhs, p, preferred_element_type=F32)

    def body(kb, c):
        block(kb, None)
        return c

    first = start_ref[(pl.program_id(0) * pairs + pl.program_id(1)) * n_q + qi]
    lax.fori_loop(first, qi * n_diag, body, 0)
    for j in range(n_diag):
        block(qi * n_diag + j, j)
    o_t = jnp.concatenate([acc_ref[i, :HEAD_V, :] / acc_ref[i, HEAD_V:HEAD_V + 1, :] for i in range(2)], axis=0)
    o_ref[0] = o_t.T.astype(o_ref.dtype)


def _diag_mask(tq, tk, chunked):
    key = np.arange(tq)[:, None]
    qry = np.arange(tq)[None, :]
    ok = (key // CHUNK) <= (qry // CHUNK) if chunked else key <= qry
    return jnp.asarray(np.where(ok, 0.0, NEG_INF).astype(np.float32).reshape(tq // tk, tk, tq))


def _first_block(fcum, batch, seq, tq, tk):
    f = fcum.reshape(batch, seq, LANES)[:, :, :HEADS]
    f_q0 = f[:, ::tq, :]
    f_kl = f[:, tk - 1::tk, :]
    dead = (f_q0[:, :, None, :] - f_kl[:, None, :, :]) < -SKIP_LOG2
    dead = dead.reshape(batch, seq // tq, seq // tk, HEADS // 2, 2).all(axis=-1)
    lead = jnp.cumprod(dead.astype(jnp.int32), axis=2).sum(axis=2)
    limit = (jnp.arange(seq // tq, dtype=jnp.int32) * (tq // tk))[None, :, None]
    return jnp.minimum(lead, limit).transpose(0, 2, 1).reshape(-1).astype(jnp.int32)


def _attention(qt, k, vt, bound, fcum, *, chunked):
    batch, seq, wk = k.shape
    tq, tk = ATTN_TQ, ATTN_TK
    pairs = wk // (2 * LANES)
    n_q = seq // tq
    fast = bound <= SAFE_SCORE_BOUND * LOG2E
    shift = lax.cond(fast, lambda: jnp.full((batch, pairs, 2, seq), bound, F32),
                     lambda: _row_max(qt, k, chunked=chunked))
    first = jnp.zeros((batch * pairs * n_q,), jnp.int32)
    if fcum is not None:
        first = jnp.where(fast, _first_block(fcum, batch, seq, tq, tk), first)
    return pl.pallas_call(
        functools.partial(_attn_kernel, tq=tq, tk=tk),
        out_shape=jax.ShapeDtypeStruct((batch, seq, pairs * 2 * HEAD_V), BF16),
        grid_spec=pltpu.PrefetchScalarGridSpec(
            num_scalar_prefetch=1,
            grid=(batch, pairs, n_q),
            in_specs=[pl.BlockSpec((1, 2 * LANES, tq), lambda b, h, i, st: (b, h, i)),
                      pl.BlockSpec((1, seq, 2 * LANES), lambda b, h, i, st: (b, 0, h)),
                      pl.BlockSpec((1, 2 * HEAD_V, seq), lambda b, h, i, st: (b, h, 0)),
                      pl.BlockSpec((1, 1, 2, tq), lambda b, h, i, st: (b, h, 0, i)),
                      pl.BlockSpec((tq // tk, tk, tq), lambda b, h, i, st: (0, 0, 0))],
            out_specs=pl.BlockSpec((1, tq, 2 * HEAD_V), lambda b, h, i, st: (b, i, h)),
            scratch_shapes=[pltpu.VMEM((2, HEAD_V + SUM_ROWS, tq), F32)]),
        compiler_params=_cparams(3),
        name="attn_chunk_causal" if chunked else "attn_frame_causal",
    )(first, qt, k, vt, shift, _diag_mask(tq, tk, chunked))


def _post_kernel(x_ref, om_ref, of_ref, gom_ref, gof_ref, wom_ref, wof_ref, gffn_ref, wrh_ref, wrl_ref,
                 tri_ref, x1_out, h2_out, route_out, count_out, carry_ref):
    step = pl.program_id(0)
    ym = _rms(om_ref[...].astype(F32), gom_ref[...]).astype(BF16)
    yf = _rms(of_ref[...].astype(F32), gof_ref[...]).astype(BF16)
    x1 = (x_ref[...] + jnp.dot(ym, wom_ref[...], preferred_element_type=F32)
          + jnp.dot(yf, wof_ref[...], preferred_element_type=F32))
    x1_out[...] = x1
    h2 = _rms(x1, gffn_ref[...])
    _store_row_tiles(h2_out, h2)

    h_hi = h2.astype(BF16)
    h_lo = (h2 - h_hi.astype(F32)).astype(BF16)
    w_hi = wrh_ref[...]
    logits = (jnp.dot(h_hi, w_hi, preferred_element_type=F32)
              + jnp.dot(h_lo, w_hi, preferred_element_type=F32)
              + jnp.dot(h_hi, wrl_ref[...], preferred_element_type=F32))

    lane = lax.broadcasted_iota(jnp.int32, (1, LANES), 1).astype(F32)
    big = float(LANES)

    def first_argmax(v):
        mx = jnp.max(v, axis=-1, keepdims=True)
        idx = jnp.min(jnp.where(v == mx, lane, big), axis=-1, keepdims=True)
        return mx, idx

    lg = jnp.where(lane < N_GROUPS, logits, NEG_INF)
    mg, g_idx = first_argmax(lg)
    p_g = 1.0 / jnp.sum(jnp.exp(lg - mg), axis=-1, keepdims=True)
    e_lo = ROUTE_LANE0 + EXPERTS_PER_GROUP * g_idx
    le = jnp.where((lane >= e_lo) & (lane < e_lo + EXPERTS_PER_GROUP), logits, NEG_INF)
    m1, i1 = first_argmax(le)
    m2, i2 = first_argmax(jnp.where(lane == i1, NEG_INF, le))
    e2 = jnp.exp(m2 - m1)
    gate1 = p_g / (1.0 + e2)
    gate2 = p_g * e2 / (1.0 + e2)

    oh1 = lane == i1
    oh2 = lane == i2
    oh = (jnp.where(oh1, 1.0, 0.0) + jnp.where(oh2, 1.0, 0.0))
    before = jnp.dot(tri_ref[...], oh.astype(BF16), preferred_element_type=F32)

    @pl.when(step == 0)
    def _():
        carry_ref[...] = jnp.zeros_like(carry_ref)

    base = before + carry_ref[:1, :]
    rank1 = jnp.sum(jnp.where(oh1, base, 0.0), axis=-1, keepdims=True)
    rank2 = jnp.sum(jnp.where(oh2, base, 0.0), axis=-1, keepdims=True)
    total = carry_ref[:1, :] + jnp.sum(oh, axis=0, keepdims=True)
    carry_ref[...] = jnp.broadcast_to(total, carry_ref.shape)
    count_out[...] = jnp.broadcast_to(total, count_out.shape)

    route = jnp.where(lane == 0, i1 - ROUTE_LANE0, 0.0)
    route = jnp.where(lane == 1, i2 - ROUTE_LANE0, route)
    route = jnp.where(lane == 2, rank1, route)
    route = jnp.where(lane == 3, rank2, route)
    route = jnp.where(lane == 4, gate1, route)
    route = jnp.where(lane == 5, gate2, route)
    route_out[...] = route


def _post_attention(x2d, o_mla, o_fox, w):
    n, d = x2d.shape
    t = POST_TILE
    parts = d // LANES
    const = lambda i: (0, 0)
    row = lambda i: (i, 0)
    full = lambda a: pl.BlockSpec(a.shape, const)
    ins = [x2d, o_mla, o_fox, w["g_out_mla"], w["g_out_fox"], w["w_out_mla"], w["w_out_fox"],
           w["g_ffn"], w["w_router_hi"], w["w_router_lo"], w["tri_strict"]]
    in_specs = [pl.BlockSpec((t, d), row), pl.BlockSpec((t, o_mla.shape[1]), row),
                pl.BlockSpec((t, o_fox.shape[1]), row)] + [full(a) for a in ins[3:]]
    return pl.pallas_call(
        _post_kernel,
        out_shape=(jax.ShapeDtypeStruct((n, d), F32), jax.ShapeDtypeStruct((n * parts, LANES), F32),
                   jax.ShapeDtypeStruct((n, LANES), F32), jax.ShapeDtypeStruct((8, LANES), F32)),
        grid=(n // t,),
        in_specs=in_specs,
        out_specs=(pl.BlockSpec((t, d), row), pl.BlockSpec((t * parts, LANES), row),
                   pl.BlockSpec((t, LANES), row), pl.BlockSpec((8, LANES), const)),
        scratch_shapes=[pltpu.VMEM((8, LANES), F32)],
        compiler_params=_cparams(1),
        name="post_attention",
    )(*ins)


def _dispatch_kernel(pad_end_ref, padded_ref, n_used_ref, dest_ref, h_ref, xs_hbm, zero_ref, sem, zsem, *,
                     tile, parts):
    zrows = zero_ref.shape[0]
    n_blk = xs_hbm.shape[0] // zrows
    rblk = zrows // parts

    def zero_copy(blk):
        return pltpu.make_async_copy(zero_ref, xs_hbm.at[pl.ds(pl.multiple_of(blk * zrows, zrows), zrows)], zsem)

    @pl.when(pl.program_id(0) == 0)
    def _():
        zero_ref[...] = jnp.zeros_like(zero_ref)
        for wait in (False, True):
            for e in range(N_EXPERTS):
                @pl.when(padded_ref[e] > 0)
                def _():
                    cp = zero_copy(pad_end_ref[e] // rblk - 1)
                    cp.wait() if wait else cp.start()

            def tail(blk, c):
                cp = zero_copy(blk)
                cp.wait() if wait else cp.start()
                return c

            lax.fori_loop(n_used_ref[0], n_blk, tail, 0)

    def issue(jo, c):
        for ji in range(ISSUE_UNROLL):
            j = jo * ISSUE_UNROLL + ji
            src = h_ref.at[pl.ds(pl.multiple_of(j * parts, parts), parts)]
            for k in range(2):
                d = pl.multiple_of(dest_ref[0, 0, 2 * j + k], parts)
                pltpu.make_async_copy(src, xs_hbm.at[pl.ds(d, parts)], sem).start()
        return c

    lax.fori_loop(0, tile // ISSUE_UNROLL, issue, 0)
    for _ in range(2):
        pltpu.make_async_copy(h_ref, xs_hbm.at[pl.ds(0, tile * parts)], sem).wait()


def _dispatch(h2t, plan, parts):
    n = h2t.shape[0] // parts
    tile = DISPATCH_TILE
    dest3 = (plan["dest"] * parts).reshape(n // tile, 1, 2 * tile)
    return pl.pallas_call(
        functools.partial(_dispatch_kernel, tile=tile, parts=parts),
        out_shape=jax.ShapeDtypeStruct((plan["n_rows"] * parts, LANES), h2t.dtype),
        grid_spec=pltpu.PrefetchScalarGridSpec(
            num_scalar_prefetch=3,
            grid=(n // tile,),
            in_specs=[pl.BlockSpec((1, 1, 2 * tile), lambda i, *_: (i, 0, 0), memory_space=pltpu.SMEM),
                      pl.BlockSpec((tile * parts, LANES), lambda i, *_: (i, 0))],
            out_specs=pl.BlockSpec(memory_space=pl.ANY),
            scratch_shapes=[pltpu.VMEM((EXPERT_BLOCK * parts, LANES), h2t.dtype), pltpu.SemaphoreType.DMA,
                            pltpu.SemaphoreType.DMA]),
        compiler_params=_cparams(1),
        name="moe_dispatch",
    )(plan["pad_end"], plan["padded"], plan["n_used"], dest3, h2t)


def _expert_kernel(blk_e_ref, n_used_ref, x_ref, wg_ref, wu_ref, wdn_ref, y_ref, wgu_bf, wd_bf, *, parts):
    i = pl.program_id(0)
    used = i < n_used_ref[0]
    rows = x_ref.shape[0] // parts
    fresh = jnp.logical_or(i == 0, blk_e_ref[i] != blk_e_ref[jnp.maximum(i - 1, 0)])

    @pl.when(jnp.logical_and(used, fresh))
    def _():
        wgu_bf[:, :EXPERT_FF] = wg_ref[0, 0].astype(BF16)
        wgu_bf[:, EXPERT_FF:] = wu_ref[0, 0].astype(BF16)
        wd_bf[...] = wdn_ref[0, 0].astype(BF16)

    @pl.when(used)
    def _():
        xb = _load_row_tiles(x_ref, rows, parts).astype(BF16)
        gu = jnp.dot(xb, wgu_bf[...], preferred_element_type=F32)
        g = gu[:, :EXPERT_FF]
        act = (g / (1.0 + jnp.exp(-g))) * gu[:, EXPERT_FF:]
        _store_row_tiles(y_ref, jnp.dot(act.astype(BF16), wd_bf[...], preferred_element_type=F32))

    @pl.when(jnp.logical_not(used))
    def _():
        y_ref[...] = jnp.zeros_like(y_ref)


def _experts(xs, plan, layer, w_gate, w_up, w_down, parts):
    r = EXPERT_BLOCK
    d = parts * LANES
    rows = lambda i, be, nu: (jnp.minimum(i, nu[0] - 1), 0)
    wsel = lambda i, be, nu: (layer, be[i], 0, 0)
    return pl.pallas_call(
        functools.partial(_expert_kernel, parts=parts),
        out_shape=jax.ShapeDtypeStruct(xs.shape, F32),
        grid_spec=pltpu.PrefetchScalarGridSpec(
            num_scalar_prefetch=2,
            grid=(xs.shape[0] // (r * parts),),
            in_specs=[pl.BlockSpec((r * parts, LANES), rows),
                      pl.BlockSpec((1, 1, d, EXPERT_FF), wsel),
                      pl.BlockSpec((1, 1, d, EXPERT_FF), wsel),
                      pl.BlockSpec((1, 1, EXPERT_FF, d), wsel)],
            out_specs=pl.BlockSpec((r * parts, LANES), lambda i, be, nu: (i, 0)),
            scratch_shapes=[pltpu.VMEM((d, 2 * EXPERT_FF), BF16), pltpu.VMEM((EXPERT_FF, d), BF16)]),
        compiler_params=_cparams(1),
        name="moe_experts",
    )(plan["blk_e"], plan["n_used"], xs, w_gate, w_up, w_down)


def _ple_kernel(dest_ref, dest_next_ref, x1_ref, route_ref, p_ref, gnorm_ref, wgate_ref, wproj_ref, gout_ref,
                ys_hbm, x_out, rows_ref, sem, *, tile, parts):
    step = pl.program_id(0)
    n_steps = pl.num_programs(0)

    def row_copy(d, slot, k, j):
        return pltpu.make_async_copy(ys_hbm.at[pl.ds(pl.multiple_of(d, parts), parts)],
                                     rows_ref.at[slot, k, pl.ds(pl.multiple_of(j * parts, parts), parts)],
                                     sem.at[slot])

    def gather(idx_ref, slot):
        def issue(jo, c):
            for ji in range(ISSUE_UNROLL):
                j = jo * ISSUE_UNROLL + ji
                for k in range(2):
                    row_copy(idx_ref[0, 0, 2 * j + k], slot, k, j).start()
            return c
        lax.fori_loop(0, tile // ISSUE_UNROLL, issue, 0)

    @pl.when(step == 0)
    def _():
        gather(dest_ref, 0)

    @pl.when(step + 1 < n_steps)
    def _():
        gather(dest_next_ref, (step + 1) % 2)

    slot = step % 2

    for k in range(2):
        pltpu.make_async_copy(ys_hbm.at[pl.ds(0, tile * parts)], rows_ref.at[slot, k], sem.at[slot]).wait()

    route = route_ref[...]
    y1 = _load_row_tiles(rows_ref.at[slot, 0], tile, parts)
    y2 = _load_row_tiles(rows_ref.at[slot, 1], tile, parts)
    x2 = x1_ref[...] + route[:, 4:5] * y1 + route[:, 5:6] * y2
    ple = _rms(jnp.dot(p_ref[0].astype(BF16), wproj_ref[...], preferred_element_type=F32), gout_ref[...])
    z = jnp.dot(_rms(x2, gnorm_ref[...]).astype(BF16), wgate_ref[...], preferred_element_type=F32)
    x_out[...] = x2 + ple / (1.0 + jnp.exp(-z))


def _combine_ple(x1, route, dest, ys, p3d, layer, w):
    n, d = x1.shape
    t = PLE_TILE
    parts = d // LANES
    steps = n // t
    const = lambda i: (0, 0)
    row = lambda i: (i, 0)
    full = lambda a: pl.BlockSpec(a.shape, const)
    dest3 = (dest * parts).reshape(steps, 1, 2 * t)
    return pl.pallas_call(
        functools.partial(_ple_kernel, tile=t, parts=parts),
        out_shape=jax.ShapeDtypeStruct((n, d), F32),
        grid=(steps,),
        in_specs=[pl.BlockSpec((1, 1, 2 * t), lambda i: (i, 0, 0), memory_space=pltpu.SMEM),
                  pl.BlockSpec((1, 1, 2 * t), lambda i: (jnp.minimum(i + 1, steps - 1), 0, 0),
                               memory_space=pltpu.SMEM),
                  pl.BlockSpec((t, d), row), pl.BlockSpec((t, LANES), row),
                  pl.BlockSpec((1, t, p3d.shape[2]), lambda i: (layer, i, 0)),
                  full(w["g_ple_norm"]), full(w["w_ple_gate"]), full(w["w_ple_proj"]), full(w["g_ple_out"]),
                  pl.BlockSpec(memory_space=pl.ANY)],
        out_specs=pl.BlockSpec((t, d), row),
        scratch_shapes=[pltpu.VMEM((2, 2, t * parts, LANES), F32), pltpu.SemaphoreType.DMA((2,))],
        compiler_params=_cparams(1),
        name="moe_combine_ple",
    )(dest3, dest3, x1, route, p3d, w["g_ple_norm"], w["w_ple_gate"], w["w_ple_proj"], w["g_ple_out"], ys)


def _row(v):
    return v.reshape(1, -1).astype(F32)


def _col_rep(v):
    return jnp.broadcast_to(v.astype(F32)[:, None], (v.shape[0], LANES))


def _pad_heads(wmat, real):
    k = wmat.shape[0]
    return jnp.pad(wmat.reshape(k, HEADS, real), ((0, 0), (0, 0), (0, LANES - real))).reshape(k, HEADS * LANES)


def _placement():
    place_k = np.zeros((3, LANES, HEADS * LANES), np.float32)
    place_q = np.zeros((3, HEADS * LANES, LANES), np.float32)
    for hd in range(HEADS):
        base = hd * LANES
        for j in range(3):
            place_k[j, hd, base + AUG_K + j] = -1.0
            place_q[j, base + AUG_Q + j, hd] = 1.0
            place_k[0, ONE_LANE, base + AUG_Q + j] = 1.0
            place_q[0, base + AUG_K + j, ONE_LANE] = 1.0
    return jnp.asarray(place_k, BF16), jnp.asarray(place_q, BF16)


def _layer_weights(i, g_attn_norm, w_in, g_q_lora, w_uq, g_kv_lora, w_ukv, g_mla_q, g_mla_k, g_fox_q,
                   g_fox_k, b_fox_f, g_out_mla, g_out_fox, w_out, g_ffn_norm, w_router_group,
                   w_router_expert, w_exp_gate, w_exp_up, w_exp_down, g_ple_norm, w_ple_gate, w_ple_proj,
                   g_ple_out):
    d = w_in.shape[1]
    wf = HEADS * FOX_DIM
    c_kv = Q_LORA
    c_pe = c_kv + KV_LORA
    c_fq = c_pe + MLA_ROPE
    c_fk = c_fq + wf
    c_fv = c_fk + wf
    c_fl = c_fv + wf
    win = w_in[i]
    pad_pe = jnp.zeros((d, LANES), F32).at[:, MLA_NOPE:MLA_QK].set(win[:, c_pe:c_fq])
    pad_fl = jnp.zeros((d, LANES), F32).at[:, :HEADS].set(win[:, c_fl:])
    w = {}
    w["g_attn"] = _row(g_attn_norm[i])
    w["w_a"] = jnp.concatenate([win[:, :c_pe], pad_pe, pad_fl], axis=1).astype(BF16)
    w["w_fk"] = _pad_heads(win[:, c_fk:c_fv], FOX_DIM).astype(BF16)
    w["w_fq_t"] = _pad_heads(win[:, c_fq:c_fk], FOX_DIM).T.astype(BF16)
    w["w_fv_t"] = win[:, c_fv:c_fl].T.astype(BF16)
    w["b_fl"] = _row(jnp.pad(b_fox_f[i], (0, LANES - HEADS)))
    w["place_k"], w["place_q"] = _placement()
    w["g_q_lora"] = _row(g_q_lora[i])
    w["w_uq_t"] = _pad_heads(w_uq[i], MLA_QK).T.astype(BF16)
    w["g_kv_lora"] = _row(g_kv_lora[i])
    ukv = w_ukv[i].reshape(KV_LORA, HEADS, MLA_NOPE + HEAD_V)
    w["w_uk"] = _pad_heads(ukv[:, :, :MLA_NOPE].reshape(KV_LORA, -1), MLA_NOPE).astype(BF16)
    w["w_uv_t"] = ukv[:, :, MLA_NOPE:].reshape(KV_LORA, -1).T.astype(BF16)
    w["g_mla_q"] = _col_rep(jnp.pad(g_mla_q[i], (0, LANES - MLA_QK))) * (MLA_QK ** -0.5 * LOG2E)
    w["g_mla_k"] = _row(jnp.pad(g_mla_k[i], (0, LANES - MLA_QK)))
    w["g_fox_q"] = _col_rep(jnp.pad(g_fox_q[i], (0, LANES - FOX_DIM))) * (FOX_DIM ** -0.5 * LOG2E)
    w["g_fox_k"] = _row(jnp.pad(g_fox_k[i], (0, LANES - FOX_DIM)))
    bound = lambda gq, gk, dim: (1.02 * LOG2E * dim ** 0.5) * jnp.max(jnp.abs(gq)) * jnp.max(jnp.abs(gk))
    w["bound_mla"] = bound(g_mla_q[i], g_mla_k[i], MLA_QK).astype(F32)
    w["bound_fox"] = bound(g_fox_q[i], g_fox_k[i], FOX_DIM).astype(F32)
    w["g_out_mla"] = _row(g_out_mla[i])
    w["g_out_fox"] = _row(g_out_fox[i])
    wm = HEADS * HEAD_V
    w["w_out_mla"] = w_out[i, :wm].astype(BF16)
    w["w_out_fox"] = w_out[i, wm:].astype(BF16)
    w["g_ffn"] = _row(g_ffn_norm[i])
    wr = jnp.zeros((d, LANES), F32)
    wr = wr.at[:, :N_GROUPS].set(w_router_group[i]).at[:, ROUTE_LANE0:ROUTE_LANE0 + N_EXPERTS].set(
        w_router_expert[i])
    w["w_router_hi"] = wr.astype(BF16)
    w["w_router_lo"] = (wr - w["w_router_hi"].astype(F32)).astype(BF16)
    w["g_ple_norm"] = _row(g_ple_norm[i])
    w["w_ple_gate"] = w_ple_gate[i].astype(BF16)
    w["w_ple_proj"] = w_ple_proj[i].astype(BF16)
    w["g_ple_out"] = _row(g_ple_out[i])
    def tri(t, strict):
        r = lax.broadcasted_iota(jnp.int32, (t, t), 0)
        c = lax.broadcasted_iota(jnp.int32, (t, t), 1)
        return ((c < r) if strict else (c <= r)).astype(BF16)

    w["tri_incl"] = tri(TOKEN_TILE, False)
    w["tri_strict"] = tri(POST_TILE, True)
    return w


def _route_plan(route, counts, n):
    r = EXPERT_BLOCK
    cnt = counts[0, ROUTE_LANE0:ROUTE_LANE0 + N_EXPERTS].astype(jnp.int32)
    padded = ((cnt + r - 1) // r) * r
    pad_end = jnp.cumsum(padded)
    pad_start = pad_end - padded
    e = route[:, 0:2].astype(jnp.int32)
    rank = route[:, 2:4].astype(jnp.int32)
    onehot = e[:, :, None] == jnp.arange(N_EXPERTS, dtype=jnp.int32)[None, None, :]
    dest = jnp.sum(jnp.where(onehot, pad_start[None, None, :], 0), axis=-1) + rank
    n_rows = 2 * n + N_EXPERTS * r
    blk_start = jnp.arange(n_rows // r, dtype=jnp.int32) * r
    blk_e = jnp.sum((blk_start[:, None] >= pad_end[None, :]).astype(jnp.int32), axis=1)
    blk_e = jnp.minimum(blk_e, N_EXPERTS - 1)
    n_used = (pad_end[-1:] // r).astype(jnp.int32)
    return {"dest": dest.reshape(-1), "blk_e": blk_e, "n_used": n_used, "n_rows": n_rows,
            "pad_end": pad_end.astype(jnp.int32), "padded": padded.astype(jnp.int32)}


def kernel(x, p, positions, g_attn_norm, w_in, g_q_lora, w_uq, g_kv_lora, w_ukv, g_mla_q, g_mla_k, g_fox_q,
           g_fox_k, b_fox_f, g_out_mla, g_out_fox, w_out, g_ffn_norm, w_router_group, w_router_expert,
           w_exp_gate, w_exp_up, w_exp_down, g_ple_norm, w_ple_gate, w_ple_proj, g_ple_out):
    batch, seq, d = x.shape
    n = batch * seq
    depth = w_in.shape[0]
    params = (g_attn_norm, w_in, g_q_lora, w_uq, g_kv_lora, w_ukv, g_mla_q, g_mla_k, g_fox_q, g_fox_k,
              b_fox_f, g_out_mla, g_out_fox, w_out, g_ffn_norm, w_router_group, w_router_expert,
              w_exp_gate, w_exp_up, w_exp_down, g_ple_norm, w_ple_gate, w_ple_proj, g_ple_out)
    tables = _rope_tables(positions)
    xc = x.reshape(n, d)
    for i in range(depth):
        w = _layer_weights(i, *params)
        qt, k, vt, fqt, fk, fvt, fcum = _pre_attention(xc, tables, w, seq)
        b3 = lambda a: a.reshape(batch, seq, a.shape[-1])
        o_mla = _attention(qt, b3(k), vt, w["bound_mla"], None, chunked=True).reshape(n, -1)
        o_fox = _attention(fqt, b3(fk), fvt, w["bound_fox"], fcum, chunked=False).reshape(n, -1)
        x1, h2, route, counts = _post_attention(xc, o_mla, o_fox, w)
        plan = _route_plan(route, counts, n)
        parts = d // LANES
        xs = _dispatch(h2, plan, parts)
        ys = _experts(xs, plan, i, w_exp_gate, w_exp_up, w_exp_down, parts)
        xc = _combine_ple(x1, route, plan["dest"], ys, p.reshape(depth, n, -1), i, w)
    return xc.reshape(batch, seq, d)
```

```python
import functools
import math

import numpy as np
import jax
import jax.numpy as jnp
from jax import lax
from jax.experimental import pallas as pl
from jax.experimental.pallas import tpu as pltpu

F32 = jnp.float32
BF16 = jnp.bfloat16

EPS = 1e-6
NEG_INF = -1e30
ROPE_THETA = 10000.0
LOG2E = math.log2(math.e)

LANES = 128
CHUNK = 64
HEADS = 8
MLA_NOPE = 64
MLA_ROPE = 32
MLA_QK = MLA_NOPE + MLA_ROPE
HEAD_V = 64
FOX_DIM = 64
Q_LORA = 256
KV_LORA = 128
N_GROUPS = 4
EXPERTS_PER_GROUP = 8
N_EXPERTS = N_GROUPS * EXPERTS_PER_GROUP
EXPERT_FF = 256
ROUTE_LANE0 = N_GROUPS

AUG_Q = FOX_DIM
AUG_K = FOX_DIM + 3
PIECE_STRIDE = 16
ONE_LANE = 3 * PIECE_STRIDE

TOKEN_TILE = 512
POST_TILE = 1024
ATTN_TQ = 1024
ATTN_TK = 1024
SUM_ROWS = 16
EXPERT_BLOCK = 256
DISPATCH_TILE = 1024
PLE_TILE = 256
ISSUE_UNROLL = 8
VMEM_LIMIT = 56 * 1024 * 1024
SAFE_SCORE_BOUND = 40.0
SKIP_LOG2 = 160.0

_NT = (((1,), (1,)), ((), ()))


def _cparams(n_axes):
    return pltpu.CompilerParams(dimension_semantics=("arbitrary",) * n_axes,
                                vmem_limit_bytes=VMEM_LIMIT)


def _rms(x, g):
    return x * lax.rsqrt(jnp.mean(x * x, axis=-1, keepdims=True) + EPS) * g


def _split3(x):
    hi = x.astype(BF16)
    r1 = x - hi.astype(F32)
    mid = r1.astype(BF16)
    lo = (r1 - mid.astype(F32)).astype(BF16)
    return hi, mid, lo


def _store_row_tiles(ref, v):
    parts = v.shape[1] // LANES
    for s in range(parts):
        ref[pl.ds(s, v.shape[0], stride=parts), :] = v[:, s * LANES:(s + 1) * LANES]


def _load_row_tiles(ref, rows, parts):
    return jnp.concatenate([ref[pl.ds(s, rows, stride=parts), :] for s in range(parts)], axis=1)


def _lane_tile(a, width):
    return jnp.tile(a, (1, width // LANES))


def _rope_kernel(pos_col_ref, pos_row_ref, invf_lane_ref, sign_ref, invf_rep_ref,
                 cos_ref, sin_ref, cost_ref, sint_ref):
    ang = pos_col_ref[...] * invf_lane_ref[...]
    cos_ref[...] = jnp.cos(ang)
    sin_ref[...] = jnp.sin(ang) * sign_ref[...]
    t = pos_row_ref.shape[-1]
    ang_t = _lane_tile(invf_rep_ref[...], t) * pos_row_ref[0]
    cost_ref[...] = jnp.cos(ang_t)
    sint_ref[...] = jnp.sin(ang_t)


def _rope_tables(positions):
    n = positions.size
    half = MLA_ROPE // 2
    inv_freq = ROPE_THETA ** (-np.arange(0, MLA_ROPE, 2, dtype=np.float32) / MLA_ROPE)
    invf = np.zeros((1, LANES), np.float32)
    sign = np.zeros((1, LANES), np.float32)
    invf[0, MLA_NOPE:MLA_NOPE + half] = inv_freq
    invf[0, MLA_NOPE + half:MLA_QK] = inv_freq
    sign[0, MLA_NOPE:MLA_NOPE + half] = -1.0
    sign[0, MLA_NOPE + half:MLA_QK] = 1.0
    invf_rep = np.broadcast_to(inv_freq[:, None], (half, LANES)).astype(np.float32)
    pos = positions.astype(F32)
    t = TOKEN_TILE
    const = lambda i: (0, 0)
    return pl.pallas_call(
        _rope_kernel,
        out_shape=(jax.ShapeDtypeStruct((n, LANES), F32), jax.ShapeDtypeStruct((n, LANES), F32),
                   jax.ShapeDtypeStruct((half, n), F32), jax.ShapeDtypeStruct((half, n), F32)),
        grid=(n // t,),
        in_specs=[pl.BlockSpec((t, 1), lambda i: (i, 0)),
                  pl.BlockSpec((1, 1, t), lambda i: (i, 0, 0)),
                  pl.BlockSpec((1, LANES), const), pl.BlockSpec((1, LANES), const),
                  pl.BlockSpec((half, LANES), const)],
        out_specs=(pl.BlockSpec((t, LANES), lambda i: (i, 0)), pl.BlockSpec((t, LANES), lambda i: (i, 0)),
                   pl.BlockSpec((half, t), lambda i: (0, i)), pl.BlockSpec((half, t), lambda i: (0, i))),
        compiler_params=_cparams(1),
        name="rope_tables",
    )(pos.reshape(n, 1), pos.reshape(n // t, 1, t), jnp.asarray(invf), jnp.asarray(sign),
      jnp.asarray(invf_rep))


def _pre_kernel(x_ref, cos_ref, sin_ref, cost_ref, sint_ref, gattn_ref, wa_ref, wfk_ref, wfqt_ref, wfvt_ref,
                bfl_ref, tri_ref, plk_ref, plq_ref, gql_ref, wuqt_ref, gkvl_ref, wuk_ref, wuvt_ref,
                gq_ref, gk_ref, gfq_ref, gfk_ref,
                qt_out, k_out, vt_out, fqt_out, fk_out, fvt_out, fcum_out, carry_ref, *, tiles_per_seq):
    step = pl.program_id(0)
    x = x_ref[...]
    t = x.shape[0]
    h = _rms(x, gattn_ref[...]).astype(BF16)
    pa = jnp.dot(h, wa_ref[...], preferred_element_type=F32)
    half = MLA_ROPE // 2
    lane = lax.broadcasted_iota(jnp.int32, (1, LANES), 1)
    low = lane < FOX_DIM

    def head_lanes(m, hd):
        v = m[:, (hd // 2) * LANES:(hd // 2 + 1) * LANES]
        if hd % 2:
            v = pltpu.roll(v, LANES // 2, 1)
        return jnp.where(low, v, 0.0)

    qn = _rms(pa[:, :Q_LORA], gql_ref[...]).astype(BF16)
    kvn = _rms(pa[:, Q_LORA:Q_LORA + KV_LORA], gkvl_ref[...]).astype(BF16)
    vt_out[0] = lax.dot_general(wuvt_ref[...], kvn, _NT, preferred_element_type=F32).astype(BF16)

    kn = jnp.dot(kvn, wuk_ref[...], preferred_element_type=F32)
    kpe = pa[:, Q_LORA + KV_LORA:Q_LORA + KV_LORA + LANES]
    cos_l = cos_ref[...]
    sin_l = sin_ref[...]
    gk = gk_ref[...]
    for hd in range(HEADS):
        v = head_lanes(kn, hd) + kpe
        v = v * lax.rsqrt(jnp.sum(v * v, axis=-1, keepdims=True) * (1.0 / MLA_QK) + EPS) * gk
        swapped = jnp.where(lane < MLA_NOPE + half, pltpu.roll(v, LANES - half, 1), pltpu.roll(v, half, 1))
        k_out[:, hd * LANES:(hd + 1) * LANES] = (v * cos_l + swapped * sin_l).astype(BF16)

    qt = lax.dot_general(wuqt_ref[...], qn, _NT, preferred_element_type=F32)
    cos_r = cost_ref[...]
    sin_r = sint_ref[...]
    gq = _lane_tile(gq_ref[...], t)[:MLA_QK]
    pad_q = jnp.zeros((LANES - MLA_QK, t), BF16)
    for hd in range(HEADS):
        blk = qt[hd * MLA_QK:(hd + 1) * MLA_QK]
        r = lax.rsqrt(jnp.sum(blk * blk, axis=0, keepdims=True) * (1.0 / MLA_QK) + EPS)
        blk = blk * r * gq
        x1, x2 = blk[MLA_NOPE:MLA_NOPE + half], blk[MLA_NOPE + half:]
        blk = jnp.concatenate([blk[:MLA_NOPE], x1 * cos_r - x2 * sin_r, x2 * cos_r + x1 * sin_r], axis=0)
        qt_out[0, hd * LANES:(hd + 1) * LANES, :] = jnp.concatenate([blk.astype(BF16), pad_q], axis=0)

    fvt_out[0] = lax.dot_general(wfvt_ref[...], h, _NT, preferred_element_type=F32).astype(BF16)

    def by_group(a, b, c):
        return jnp.where(lane < PIECE_STRIDE, a, jnp.where(lane < 2 * PIECE_STRIDE, b, c))

    z = pa[:, Q_LORA + KV_LORA + LANES:] + bfl_ref[...]
    logf = jnp.minimum(z, 0.0) - jnp.log1p(jnp.exp(-jnp.abs(z)))
    parts = jnp.dot(tri_ref[...], by_group(*_split3(logf)), preferred_element_type=F32)
    tot = parts + pltpu.roll(parts, PIECE_STRIDE, 1) + pltpu.roll(parts, 2 * PIECE_STRIDE, 1)
    cum = by_group(pltpu.roll(tot, LANES - 2 * PIECE_STRIDE, 1), pltpu.roll(tot, LANES - PIECE_STRIDE, 1), tot)

    @pl.when(step % tiles_per_seq == 0)
    def _():
        carry_ref[...] = jnp.zeros_like(carry_ref)

    cum = cum + carry_ref[:1, :]
    carry_ref[...] = jnp.broadcast_to(cum[t - 1:t, :], carry_ref.shape)
    cum2 = cum * LOG2E
    fcum_out[...] = cum2
    packed = by_group(*_split3(cum2))
    packed = jnp.where(lane == ONE_LANE, jnp.ones_like(packed), packed)
    aug_k = jnp.dot(packed, plk_ref[...], preferred_element_type=F32)
    aug_q = lax.dot_general(plq_ref[...], packed, _NT, preferred_element_type=F32)

    fk = jnp.dot(h, wfk_ref[...], preferred_element_type=F32)
    gfk = gfk_ref[...]
    for hd in range(HEADS):
        sl = slice(hd * LANES, (hd + 1) * LANES)
        v = head_lanes(fk, hd)
        v = v * lax.rsqrt(jnp.sum(v * v, axis=-1, keepdims=True) * (1.0 / FOX_DIM) + EPS) * gfk
        fk_out[:, sl] = (v + aug_k[:, sl]).astype(BF16)

    fqt = lax.dot_general(wfqt_ref[...], h, _NT, preferred_element_type=F32)
    gfq = _lane_tile(gfq_ref[...], t)[:FOX_DIM]
    for hd in range(HEADS):
        blk = fqt[hd * FOX_DIM:(hd + 1) * FOX_DIM]
        r = lax.rsqrt(jnp.sum(blk * blk, axis=0, keepdims=True) * (1.0 / FOX_DIM) + EPS)
        aug = aug_q[hd * LANES + FOX_DIM:(hd + 1) * LANES]
        fqt_out[0, hd * LANES:(hd + 1) * LANES, :] = jnp.concatenate([blk * r * gfq, aug], axis=0).astype(BF16)


def _pre_attention(x2d, tables, w, seq):
    n, d = x2d.shape
    t = TOKEN_TILE
    tiles_per_seq = seq // t
    batch = n // seq
    cos_l, sin_l, cos_t, sin_t = tables
    half = MLA_ROPE // 2
    row = lambda i: (i, 0)
    seq_t = lambda i: (i // tiles_per_seq, 0, i % tiles_per_seq)

    def full(a):
        return pl.BlockSpec(a.shape, lambda i, nd=a.ndim: (0,) * nd)

    weights = [w["g_attn"], w["w_a"], w["w_fk"], w["w_fq_t"], w["w_fv_t"], w["b_fl"], w["tri_incl"],
               w["place_k"], w["place_q"], w["g_q_lora"], w["w_uq_t"], w["g_kv_lora"], w["w_uk"], w["w_uv_t"],
               w["g_mla_q"], w["g_mla_k"], w["g_fox_q"], w["g_fox_k"]]
    in_specs = [pl.BlockSpec((t, d), row), pl.BlockSpec((t, LANES), row), pl.BlockSpec((t, LANES), row),
                pl.BlockSpec((half, t), lambda i: (0, i)), pl.BlockSpec((half, t), lambda i: (0, i))]
    in_specs += [full(a) for a in weights]
    wq = HEADS * LANES
    wv = HEADS * HEAD_V
    out_shape = (jax.ShapeDtypeStruct((batch, wq, seq), BF16), jax.ShapeDtypeStruct((n, wq), BF16),
                 jax.ShapeDtypeStruct((batch, wv, seq), BF16), jax.ShapeDtypeStruct((batch, wq, seq), BF16),
                 jax.ShapeDtypeStruct((n, wq), BF16), jax.ShapeDtypeStruct((batch, wv, seq), BF16),
                 jax.ShapeDtypeStruct((n, LANES), F32))
    out_specs = (pl.BlockSpec((1, wq, t), seq_t), pl.BlockSpec((t, wq), row), pl.BlockSpec((1, wv, t), seq_t),
                 pl.BlockSpec((1, wq, t), seq_t), pl.BlockSpec((t, wq), row), pl.BlockSpec((1, wv, t), seq_t),
                 pl.BlockSpec((t, LANES), row))
    return pl.pallas_call(
        functools.partial(_pre_kernel, tiles_per_seq=tiles_per_seq),
        out_shape=out_shape,
        grid=(n // t,),
        in_specs=in_specs,
        out_specs=out_specs,
        scratch_shapes=[pltpu.VMEM((8, LANES), F32)],
        compiler_params=_cparams(1),
        name="pre_attention",
    )(x2d, cos_l, sin_l, cos_t, sin_t, *weights)


def _allowed(k0, q0, tk, tq, chunked):
    key = k0 + lax.broadcasted_iota(jnp.int32, (tk, tq), 0)
    qry = q0 + lax.broadcasted_iota(jnp.int32, (tk, tq), 1)
    if chunked:
        return (key // CHUNK) <= (qry // CHUNK)
    return key <= qry


def _rowmax_kernel(qt_ref, k_ref, m_ref, *, chunked, tq, tk):
    qi = pl.program_id(2)
    n_diag = tq // tk

    def block(kb, carry, masked):
        k0 = pl.multiple_of(kb * tk, tk)
        kblk = k_ref[0, pl.ds(k0, tk), :]
        out = []
        for i in range(2):
            s = jnp.dot(kblk[:, i * LANES:(i + 1) * LANES], qt_ref[0, i * LANES:(i + 1) * LANES, :],
                        preferred_element_type=F32)
            if masked:
                s = jnp.where(_allowed(k0, qi * tq, tk, tq, chunked), s, NEG_INF)
            out.append(jnp.maximum(carry[i], jnp.max(s, axis=0, keepdims=True)))
        return tuple(out)

    init = tuple(jnp.full((1, tq), NEG_INF, F32) for _ in range(2))
    carry = lax.fori_loop(0, qi * n_diag, lambda kb, c: block(kb, c, False), init)
    for j in range(n_diag):
        carry = block(qi * n_diag + j, carry, True)
    m_ref[0, 0] = jnp.concatenate(carry, axis=0)


def _row_max(qt, k, *, chunked):
    batch, seq, wk = k.shape
    tq, tk = ATTN_TQ, ATTN_TK
    pairs = wk // (2 * LANES)
    out = pl.pallas_call(
        functools.partial(_rowmax_kernel, chunked=chunked, tq=tq, tk=tk),
        out_shape=jax.ShapeDtypeStruct((batch, pairs, 2, seq), F32),
        grid=(batch, pairs, seq // tq),
        in_specs=[pl.BlockSpec((1, 2 * LANES, tq), lambda b, h, i: (b, h, i)),
                  pl.BlockSpec((1, seq, 2 * LANES), lambda b, h, i: (b, 0, h))],
        out_specs=pl.BlockSpec((1, 1, 2, tq), lambda b, h, i: (b, h, 0, i)),
        compiler_params=_cparams(3),
        name="attn_rowmax",
    )(qt, k)
    return out


def _attn_kernel(start_ref, qt_ref, k_ref, vt_ref, shift_ref, mask_ref, o_ref, acc_ref, *, tq, tk):
    pairs = pl.num_programs(1)
    n_q = pl.num_programs(2)
    qi = pl.program_id(2)
    acc_ref[...] = jnp.zeros_like(acc_ref)

    def block(k0, nk, q_lo, masked):
        qs = slice(q_lo, tq)
        kblk = k_ref[0, pl.ds(k0, nk), :]
        vt = vt_ref[0, :, pl.ds(k0, nk)]
        ones = jnp.ones((SUM_ROWS, nk), BF16)
        for i in range(2):
            s = jnp.dot(kblk[:, i * LANES:(i + 1) * LANES], qt_ref[0, i * LANES:(i + 1) * LANES, qs],
                        preferred_element_type=F32)
            s = s - shift_ref[0, 0, i:i + 1, qs]
            if masked:
                s = s + mask_ref[q_lo:q_lo + nk, qs]
            p = jnp.exp2(s).astype(BF16)
            lhs = jnp.concatenate([vt[i * HEAD_V:(i + 1) * HEAD_V], ones], axis=0)
            acc_ref[i, :, qs] += jnp.dot(lhs, p, preferred_element_type=F32)

    def body(kb, c):
        block(pl.multiple_of(kb * tk, tk), tk, 0, False)
        return c

    first = start_ref[(pl.program_id(0) * pairs + pl.program_id(1)) * n_q + qi]
    lax.fori_loop(first, qi, body, 0)
    half = tk // 2
    q0 = pl.multiple_of(qi * tq, tq)
    block(q0, half, 0, True)
    block(pl.multiple_of(q0 + half, half), half, half, True)
    o_t = jnp.concatenate([acc_ref[i, :HEAD_V, :] / acc_ref[i, HEAD_V:HEAD_V + 1, :] for i in range(2)], axis=0)
    o_ref[0] = o_t.T.astype(o_ref.dtype)


def _diag_mask(tq, chunked):
    key = np.arange(tq)[:, None]
    qry = np.arange(tq)[None, :]
    ok = (key // CHUNK) <= (qry // CHUNK) if chunked else key <= qry
    return jnp.asarray(np.where(ok, 0.0, NEG_INF).astype(np.float32))


def _first_block(fcum, batch, seq, tq, tk):
    f = fcum.reshape(batch, seq, LANES)[:, :, :HEADS]
    f_q0 = f[:, ::tq, :]
    f_kl = f[:, tk - 1::tk, :]
    dead = (f_q0[:, :, None, :] - f_kl[:, None, :, :]) < -SKIP_LOG2
    dead = dead.reshape(batch, seq // tq, seq // tk, HEADS // 2, 2).all(axis=-1)
    lead = jnp.cumprod(dead.astype(jnp.int32), axis=2).sum(axis=2)
    limit = (jnp.arange(seq // tq, dtype=jnp.int32) * (tq // tk))[None, :, None]
    return jnp.minimum(lead, limit).transpose(0, 2, 1).reshape(-1).astype(jnp.int32)


def _attention(qt, k, vt, bound, fcum, *, chunked):
    batch, seq, wk = k.shape
    tq, tk = ATTN_TQ, ATTN_TK
    pairs = wk // (2 * LANES)
    n_q = seq // tq
    fast = bound <= SAFE_SCORE_BOUND * LOG2E
    shift = lax.cond(fast, lambda: jnp.full((batch, pairs, 2, seq), bound, F32),
                     lambda: _row_max(qt, k, chunked=chunked))
    first = jnp.zeros((batch * pairs * n_q,), jnp.int32)
    if fcum is not None:
        first = jnp.where(fast, _first_block(fcum, batch, seq, tq, tk), first)
    return pl.pallas_call(
        functools.partial(_attn_kernel, tq=tq, tk=tk),
        out_shape=jax.ShapeDtypeStruct((batch, seq, pairs * 2 * HEAD_V), BF16),
        grid_spec=pltpu.PrefetchScalarGridSpec(
            num_scalar_prefetch=1,
            grid=(batch, pairs, n_q),
            in_specs=[pl.BlockSpec((1, 2 * LANES, tq), lambda b, h, i, st: (b, h, i)),
                      pl.BlockSpec((1, seq, 2 * LANES), lambda b, h, i, st: (b, 0, h)),
                      pl.BlockSpec((1, 2 * HEAD_V, seq), lambda b, h, i, st: (b, h, 0)),
                      pl.BlockSpec((1, 1, 2, tq), lambda b, h, i, st: (b, h, 0, i)),
                      pl.BlockSpec((tq, tq), lambda b, h, i, st: (0, 0))],
            out_specs=pl.BlockSpec((1, tq, 2 * HEAD_V), lambda b, h, i, st: (b, i, h)),
            scratch_shapes=[pltpu.VMEM((2, HEAD_V + SUM_ROWS, tq), F32)]),
        compiler_params=_cparams(3),
        name="attn_chunk_causal" if chunked else "attn_frame_causal",
    )(first, qt, k, vt, shift, _diag_mask(tq, chunked))


def _post_kernel(x_ref, om_ref, of_ref, gom_ref, gof_ref, wom_ref, wof_ref, gffn_ref, wrh_ref, wrl_ref,
                 tri_ref, x1_out, h2_out, route_out, count_out, carry_ref):
    step = pl.program_id(0)
    ym = _rms(om_ref[...].astype(F32), gom_ref[...]).astype(BF16)
    yf = _rms(of_ref[...].astype(F32), gof_ref[...]).astype(BF16)
    x1 = (x_ref[...] + jnp.dot(ym, wom_ref[...], preferred_element_type=F32)
          + jnp.dot(yf, wof_ref[...], preferred_element_type=F32))
    x1_out[...] = x1
    h2 = _rms(x1, gffn_ref[...])
    _store_row_tiles(h2_out, h2)

    h_hi = h2.astype(BF16)
    h_lo = (h2 - h_hi.astype(F32)).astype(BF16)
    w_hi = wrh_ref[...]
    logits = (jnp.dot(h_hi, w_hi, preferred_element_type=F32)
              + jnp.dot(h_lo, w_hi, preferred_element_type=F32)
              + jnp.dot(h_hi, wrl_ref[...], preferred_element_type=F32))

    lane = lax.broadcasted_iota(jnp.int32, (1, LANES), 1).astype(F32)
    big = float(LANES)

    def first_argmax(v):
        mx = jnp.max(v, axis=-1, keepdims=True)
        idx = jnp.min(jnp.where(v == mx, lane, big), axis=-1, keepdims=True)
        return mx, idx

    lg = jnp.where(lane < N_GROUPS, logits, NEG_INF)
    mg, g_idx = first_argmax(lg)
    p_g = 1.0 / jnp.sum(jnp.exp(lg - mg), axis=-1, keepdims=True)
    e_lo = ROUTE_LANE0 + EXPERTS_PER_GROUP * g_idx
    le = jnp.where((lane >= e_lo) & (lane < e_lo + EXPERTS_PER_GROUP), logits, NEG_INF)
    m1, i1 = first_argmax(le)
    m2, i2 = first_argmax(jnp.where(lane == i1, NEG_INF, le))
    e2 = jnp.exp(m2 - m1)
    gate1 = p_g / (1.0 + e2)
    gate2 = p_g * e2 / (1.0 + e2)

    oh1 = lane == i1
    oh2 = lane == i2
    oh = (jnp.where(oh1, 1.0, 0.0) + jnp.where(oh2, 1.0, 0.0))
    before = jnp.dot(tri_ref[...], oh.astype(BF16), preferred_element_type=F32)

    @pl.when(step == 0)
    def _():
        carry_ref[...] = jnp.zeros_like(carry_ref)

    base = before + carry_ref[:1, :]
    rank1 = jnp.sum(jnp.where(oh1, base, 0.0), axis=-1, keepdims=True)
    rank2 = jnp.sum(jnp.where(oh2, base, 0.0), axis=-1, keepdims=True)
    total = carry_ref[:1, :] + jnp.sum(oh, axis=0, keepdims=True)
    carry_ref[...] = jnp.broadcast_to(total, carry_ref.shape)
    count_out[...] = jnp.broadcast_to(total, count_out.shape)

    route = jnp.where(lane == 0, i1 - ROUTE_LANE0, 0.0)
    route = jnp.where(lane == 1, i2 - ROUTE_LANE0, route)
    route = jnp.where(lane == 2, rank1, route)
    route = jnp.where(lane == 3, rank2, route)
    route = jnp.where(lane == 4, gate1, route)
    route = jnp.where(lane == 5, gate2, route)
    route_out[...] = route


def _post_attention(x2d, o_mla, o_fox, w):
    n, d = x2d.shape
    t = POST_TILE
    parts = d // LANES
    const = lambda i: (0, 0)
    row = lambda i: (i, 0)
    full = lambda a: pl.BlockSpec(a.shape, const)
    ins = [x2d, o_mla, o_fox, w["g_out_mla"], w["g_out_fox"], w["w_out_mla"], w["w_out_fox"],
           w["g_ffn"], w["w_router_hi"], w["w_router_lo"], w["tri_strict"]]
    in_specs = [pl.BlockSpec((t, d), row), pl.BlockSpec((t, o_mla.shape[1]), row),
                pl.BlockSpec((t, o_fox.shape[1]), row)] + [full(a) for a in ins[3:]]
    return pl.pallas_call(
        _post_kernel,
        out_shape=(jax.ShapeDtypeStruct((n, d), F32), jax.ShapeDtypeStruct((n * parts, LANES), F32),
                   jax.ShapeDtypeStruct((n, LANES), F32), jax.ShapeDtypeStruct((8, LANES), F32)),
        grid=(n // t,),
        in_specs=in_specs,
        out_specs=(pl.BlockSpec((t, d), row), pl.BlockSpec((t * parts, LANES), row),
                   pl.BlockSpec((t, LANES), row), pl.BlockSpec((8, LANES), const)),
        scratch_shapes=[pltpu.VMEM((8, LANES), F32)],
        compiler_params=_cparams(1),
        name="post_attention",
    )(*ins)


def _dispatch_kernel(pad_end_ref, padded_ref, n_used_ref, dest_ref, h_ref, xs_hbm, zero_ref, sem, zsem, *,
                     tile, parts):
    zrows = zero_ref.shape[0]
    n_blk = xs_hbm.shape[0] // zrows
    rblk = zrows // parts

    def zero_copy(blk):
        return pltpu.make_async_copy(zero_ref, xs_hbm.at[pl.ds(pl.multiple_of(blk * zrows, zrows), zrows)], zsem)

    @pl.when(pl.program_id(0) == 0)
    def _():
        zero_ref[...] = jnp.zeros_like(zero_ref)
        for wait in (False, True):
            for e in range(N_EXPERTS):
                @pl.when(padded_ref[e] > 0)
                def _():
                    cp = zero_copy(pad_end_ref[e] // rblk - 1)
                    cp.wait() if wait else cp.start()

            def tail(blk, c):
                cp = zero_copy(blk)
                cp.wait() if wait else cp.start()
                return c

            lax.fori_loop(n_used_ref[0], n_blk, tail, 0)

    def issue(jo, c):
        for ji in range(ISSUE_UNROLL):
            j = jo * ISSUE_UNROLL + ji
            src = h_ref.at[pl.ds(pl.multiple_of(j * parts, parts), parts)]
            for k in range(2):
                d = pl.multiple_of(dest_ref[0, 0, 2 * j + k], parts)
                pltpu.make_async_copy(src, xs_hbm.at[pl.ds(d, parts)], sem).start()
        return c

    lax.fori_loop(0, tile // ISSUE_UNROLL, issue, 0)
    for _ in range(2):
        pltpu.make_async_copy(h_ref, xs_hbm.at[pl.ds(0, tile * parts)], sem).wait()


def _dispatch(h2t, plan, parts):
    n = h2t.shape[0] // parts
    tile = DISPATCH_TILE
    dest3 = (plan["dest"] * parts).reshape(n // tile, 1, 2 * tile)
    return pl.pallas_call(
        functools.partial(_dispatch_kernel, tile=tile, parts=parts),
        out_shape=jax.ShapeDtypeStruct((plan["n_rows"] * parts, LANES), h2t.dtype),
        grid_spec=pltpu.PrefetchScalarGridSpec(
            num_scalar_prefetch=3,
            grid=(n // tile,),
            in_specs=[pl.BlockSpec((1, 1, 2 * tile), lambda i, *_: (i, 0, 0), memory_space=pltpu.SMEM),
                      pl.BlockSpec((tile * parts, LANES), lambda i, *_: (i, 0))],
            out_specs=pl.BlockSpec(memory_space=pl.ANY),
            scratch_shapes=[pltpu.VMEM((EXPERT_BLOCK * parts, LANES), h2t.dtype), pltpu.SemaphoreType.DMA,
                            pltpu.SemaphoreType.DMA]),
        compiler_params=_cparams(1),
        name="moe_dispatch",
    )(plan["pad_end"], plan["padded"], plan["n_used"], dest3, h2t)


def _expert_kernel(blk_e_ref, n_used_ref, x_ref, wg_ref, wu_ref, wdn_ref, y_ref, wgu_bf, wd_bf, *, parts):
    i = pl.program_id(0)
    used = i < n_used_ref[0]
    rows = x_ref.shape[0] // parts
    fresh = jnp.logical_or(i == 0, blk_e_ref[i] != blk_e_ref[jnp.maximum(i - 1, 0)])

    @pl.when(jnp.logical_and(used, fresh))
    def _():
        wgu_bf[:, :EXPERT_FF] = wg_ref[0, 0].astype(BF16)
        wgu_bf[:, EXPERT_FF:] = wu_ref[0, 0].astype(BF16)
        wd_bf[...] = wdn_ref[0, 0].astype(BF16)

    @pl.when(used)
    def _():
        xb = _load_row_tiles(x_ref, rows, parts).astype(BF16)
        gu = jnp.dot(xb, wgu_bf[...], preferred_element_type=F32)
        g = gu[:, :EXPERT_FF]
        act = (g / (1.0 + jnp.exp(-g))) * gu[:, EXPERT_FF:]
        _store_row_tiles(y_ref, jnp.dot(act.astype(BF16), wd_bf[...], preferred_element_type=F32))

    @pl.when(jnp.logical_not(used))
    def _():
        y_ref[...] = jnp.zeros_like(y_ref)


def _experts(xs, plan, layer, w_gate, w_up, w_down, parts):
    r = EXPERT_BLOCK
    d = parts * LANES
    rows = lambda i, be, nu: (jnp.minimum(i, nu[0] - 1), 0)
    wsel = lambda i, be, nu: (layer, be[i], 0, 0)
    return pl.pallas_call(
        functools.partial(_expert_kernel, parts=parts),
        out_shape=jax.ShapeDtypeStruct(xs.shape, F32),
        grid_spec=pltpu.PrefetchScalarGridSpec(
            num_scalar_prefetch=2,
            grid=(xs.shape[0] // (r * parts),),
            in_specs=[pl.BlockSpec((r * parts, LANES), rows),
                      pl.BlockSpec((1, 1, d, EXPERT_FF), wsel),
                      pl.BlockSpec((1, 1, d, EXPERT_FF), wsel),
                      pl.BlockSpec((1, 1, EXPERT_FF, d), wsel)],
            out_specs=pl.BlockSpec((r * parts, LANES), lambda i, be, nu: (i, 0)),
            scratch_shapes=[pltpu.VMEM((d, 2 * EXPERT_FF), BF16), pltpu.VMEM((EXPERT_FF, d), BF16)]),
        compiler_params=_cparams(1),
        name="moe_experts",
    )(plan["blk_e"], plan["n_used"], xs, w_gate, w_up, w_down)


def _ple_kernel(dest_ref, dest_next_ref, x1_ref, route_ref, p_ref, gnorm_ref, wgate_ref, wproj_ref, gout_ref,
                ys_hbm, x_out, rows_ref, sem, *, tile, parts):
    step = pl.program_id(0)
    n_steps = pl.num_programs(0)

    def row_copy(d, slot, k, j):
        return pltpu.make_async_copy(ys_hbm.at[pl.ds(pl.multiple_of(d, parts), parts)],
                                     rows_ref.at[slot, k, pl.ds(pl.multiple_of(j * parts, parts), parts)],
                                     sem.at[slot])

    def gather(idx_ref, slot):
        def issue(jo, c):
            for ji in range(ISSUE_UNROLL):
                j = jo * ISSUE_UNROLL + ji
                for k in range(2):
                    row_copy(idx_ref[0, 0, 2 * j + k], slot, k, j).start()
            return c
        lax.fori_loop(0, tile // ISSUE_UNROLL, issue, 0)

    @pl.when(step == 0)
    def _():
        gather(dest_ref, 0)

    @pl.when(step + 1 < n_steps)
    def _():
        gather(dest_next_ref, (step + 1) % 2)

    slot = step % 2

    for k in range(2):
        pltpu.make_async_copy(ys_hbm.at[pl.ds(0, tile * parts)], rows_ref.at[slot, k], sem.at[slot]).wait()

    route = route_ref[...]
    y1 = _load_row_tiles(rows_ref.at[slot, 0], tile, parts)
    y2 = _load_row_tiles(rows_ref.at[slot, 1], tile, parts)
    x2 = x1_ref[...] + route[:, 4:5] * y1 + route[:, 5:6] * y2
    ple = _rms(jnp.dot(p_ref[0].astype(BF16), wproj_ref[...], preferred_element_type=F32), gout_ref[...])
    z = jnp.dot(_rms(x2, gnorm_ref[...]).astype(BF16), wgate_ref[...], preferred_element_type=F32)
    x_out[...] = x2 + ple / (1.0 + jnp.exp(-z))


def _combine_ple(x1, route, dest, ys, p3d, layer, w):
    n, d = x1.shape
    t = PLE_TILE
    parts = d // LANES
    steps = n // t
    const = lambda i: (0, 0)
    row = lambda i: (i, 0)
    full = lambda a: pl.BlockSpec(a.shape, const)
    dest3 = (dest * parts).reshape(steps, 1, 2 * t)
    return pl.pallas_call(
        functools.partial(_ple_kernel, tile=t, parts=parts),
        out_shape=jax.ShapeDtypeStruct((n, d), F32),
        grid=(steps,),
        in_specs=[pl.BlockSpec((1, 1, 2 * t), lambda i: (i, 0, 0), memory_space=pltpu.SMEM),
                  pl.BlockSpec((1, 1, 2 * t), lambda i: (jnp.minimum(i + 1, steps - 1), 0, 0),
                               memory_space=pltpu.SMEM),
                  pl.BlockSpec((t, d), row), pl.BlockSpec((t, LANES), row),
                  pl.BlockSpec((1, t, p3d.shape[2]), lambda i: (layer, i, 0)),
                  full(w["g_ple_norm"]), full(w["w_ple_gate"]), full(w["w_ple_proj"]), full(w["g_ple_out"]),
                  pl.BlockSpec(memory_space=pl.ANY)],
        out_specs=pl.BlockSpec((t, d), row),
        scratch_shapes=[pltpu.VMEM((2, 2, t * parts, LANES), F32), pltpu.SemaphoreType.DMA((2,))],
        compiler_params=_cparams(1),
        name="moe_combine_ple",
    )(dest3, dest3, x1, route, p3d, w["g_ple_norm"], w["w_ple_gate"], w["w_ple_proj"], w["g_ple_out"], ys)


def _row(v):
    return v.reshape(1, -1).astype(F32)


def _col_rep(v):
    return jnp.broadcast_to(v.astype(F32)[:, None], (v.shape[0], LANES))


def _pad_heads(wmat, real):
    k = wmat.shape[0]
    return jnp.pad(wmat.reshape(k, HEADS, real), ((0, 0), (0, 0), (0, LANES - real))).reshape(k, HEADS * LANES)


def _placement():
    place_k = np.zeros((LANES, HEADS * LANES), np.float32)
    place_q = np.zeros((HEADS * LANES, LANES), np.float32)
    for hd in range(HEADS):
        base = hd * LANES
        for j in range(3):
            src = j * PIECE_STRIDE + hd
            place_k[src, base + AUG_K + j] = -1.0
            place_q[base + AUG_Q + j, src] = 1.0
            place_k[ONE_LANE, base + AUG_Q + j] = 1.0
            place_q[base + AUG_K + j, ONE_LANE] = 1.0
    return jnp.asarray(place_k, BF16), jnp.asarray(place_q, BF16)


def _layer_weights(i, g_attn_norm, w_in, g_q_lora, w_uq, g_kv_lora, w_ukv, g_mla_q, g_mla_k, g_fox_q,
                   g_fox_k, b_fox_f, g_out_mla, g_out_fox, w_out, g_ffn_norm, w_router_group,
                   w_router_expert, w_exp_gate, w_exp_up, w_exp_down, g_ple_norm, w_ple_gate, w_ple_proj,
                   g_ple_out):
    d = w_in.shape[1]
    wf = HEADS * FOX_DIM
    c_kv = Q_LORA
    c_pe = c_kv + KV_LORA
    c_fq = c_pe + MLA_ROPE
    c_fk = c_fq + wf
    c_fv = c_fk + wf
    c_fl = c_fv + wf
    win = w_in[i]
    pad_pe = jnp.zeros((d, LANES), F32).at[:, MLA_NOPE:MLA_QK].set(win[:, c_pe:c_fq])
    group = lambda v: jnp.concatenate(
        [jnp.pad(v, [(0, 0)] * (v.ndim - 1) + [(0, PIECE_STRIDE - HEADS)])] * 3
        + [jnp.zeros(v.shape[:-1] + (LANES - 3 * PIECE_STRIDE,), v.dtype)], axis=-1)
    pad_fl = group(win[:, c_fl:])
    w = {}
    w["g_attn"] = _row(g_attn_norm[i])
    w["w_a"] = jnp.concatenate([win[:, :c_pe], pad_pe, pad_fl], axis=1).astype(BF16)
    w["w_fk"] = win[:, c_fk:c_fv].astype(BF16)
    w["w_fq_t"] = win[:, c_fq:c_fk].T.astype(BF16)
    w["w_fv_t"] = win[:, c_fv:c_fl].T.astype(BF16)
    w["b_fl"] = group(b_fox_f[i].astype(F32)[None, :])
    w["place_k"], w["place_q"] = _placement()
    w["g_q_lora"] = _row(g_q_lora[i])
    w["w_uq_t"] = w_uq[i].T.astype(BF16)
    w["g_kv_lora"] = _row(g_kv_lora[i])
    ukv = w_ukv[i].reshape(KV_LORA, HEADS, MLA_NOPE + HEAD_V)
    w["w_uk"] = ukv[:, :, :MLA_NOPE].reshape(KV_LORA, -1).astype(BF16)
    w["w_uv_t"] = ukv[:, :, MLA_NOPE:].reshape(KV_LORA, -1).T.astype(BF16)
    w["g_mla_q"] = _col_rep(jnp.pad(g_mla_q[i], (0, LANES - MLA_QK))) * (MLA_QK ** -0.5 * LOG2E)
    w["g_mla_k"] = _row(jnp.pad(g_mla_k[i], (0, LANES - MLA_QK)))
    w["g_fox_q"] = _col_rep(jnp.pad(g_fox_q[i], (0, LANES - FOX_DIM))) * (FOX_DIM ** -0.5 * LOG2E)
    w["g_fox_k"] = _row(jnp.pad(g_fox_k[i], (0, LANES - FOX_DIM)))
    bound = lambda gq, gk, dim: (1.02 * LOG2E * dim ** 0.5) * jnp.max(jnp.abs(gq)) * jnp.max(jnp.abs(gk))
    w["bound_mla"] = bound(g_mla_q[i], g_mla_k[i], MLA_QK).astype(F32)
    w["bound_fox"] = bound(g_fox_q[i], g_fox_k[i], FOX_DIM).astype(F32)
    w["g_out_mla"] = _row(g_out_mla[i])
    w["g_out_fox"] = _row(g_out_fox[i])
    wm = HEADS * HEAD_V
    w["w_out_mla"] = w_out[i, :wm].astype(BF16)
    w["w_out_fox"] = w_out[i, wm:].astype(BF16)
    w["g_ffn"] = _row(g_ffn_norm[i])
    wr = jnp.zeros((d, LANES), F32)
    wr = wr.at[:, :N_GROUPS].set(w_router_group[i]).at[:, ROUTE_LANE0:ROUTE_LANE0 + N_EXPERTS].set(
        w_router_expert[i])
    w["w_router_hi"] = wr.astype(BF16)
    w["w_router_lo"] = (wr - w["w_router_hi"].astype(F32)).astype(BF16)
    w["g_ple_norm"] = _row(g_ple_norm[i])
    w["w_ple_gate"] = w_ple_gate[i].astype(BF16)
    w["w_ple_proj"] = w_ple_proj[i].astype(BF16)
    w["g_ple_out"] = _row(g_ple_out[i])
    def tri(t, strict):
        r = lax.broadcasted_iota(jnp.int32, (t, t), 0)
        c = lax.broadcasted_iota(jnp.int32, (t, t), 1)
        return ((c < r) if strict else (c <= r)).astype(BF16)

    w["tri_incl"] = tri(TOKEN_TILE, False)
    w["tri_strict"] = tri(POST_TILE, True)
    return w


def _route_plan(route, counts, n):
    r = EXPERT_BLOCK
    cnt = counts[0, ROUTE_LANE0:ROUTE_LANE0 + N_EXPERTS].astype(jnp.int32)
    padded = ((cnt + r - 1) // r) * r
    pad_end = jnp.cumsum(padded)
    pad_start = pad_end - padded
    e = route[:, 0:2].astype(jnp.int32)
    rank = route[:, 2:4].astype(jnp.int32)
    onehot = e[:, :, None] == jnp.arange(N_EXPERTS, dtype=jnp.int32)[None, None, :]
    dest = jnp.sum(jnp.where(onehot, pad_start[None, None, :], 0), axis=-1) + rank
    n_rows = 2 * n + N_EXPERTS * r
    blk_start = jnp.arange(n_rows // r, dtype=jnp.int32) * r
    blk_e = jnp.sum((blk_start[:, None] >= pad_end[None, :]).astype(jnp.int32), axis=1)
    blk_e = jnp.minimum(blk_e, N_EXPERTS - 1)
    n_used = (pad_end[-1:] // r).astype(jnp.int32)
    return {"dest": dest.reshape(-1), "blk_e": blk_e, "n_used": n_used, "n_rows": n_rows,
            "pad_end": pad_end.astype(jnp.int32), "padded": padded.astype(jnp.int32)}


def kernel(x, p, positions, g_attn_norm, w_in, g_q_lora, w_uq, g_kv_lora, w_ukv, g_mla_q, g_mla_k, g_fox_q,
           g_fox_k, b_fox_f, g_out_mla, g_out_fox, w_out, g_ffn_norm, w_router_group, w_router_expert,
           w_exp_gate, w_exp_up, w_exp_down, g_ple_norm, w_ple_gate, w_ple_proj, g_ple_out):
    batch, seq, d = x.shape
    n = batch * seq
    depth = w_in.shape[0]
    params = (g_attn_norm, w_in, g_q_lora, w_uq, g_kv_lora, w_ukv, g_mla_q, g_mla_k, g_fox_q, g_fox_k,
              b_fox_f, g_out_mla, g_out_fox, w_out, g_ffn_norm, w_router_group, w_router_expert,
              w_exp_gate, w_exp_up, w_exp_down, g_ple_norm, w_ple_gate, w_ple_proj, g_ple_out)
    tables = _rope_tables(positions)
    xc = x.reshape(n, d)
    for i in range(depth):
        w = _layer_weights(i, *params)
        qt, k, vt, fqt, fk, fvt, fcum = _pre_attention(xc, tables, w, seq)
        b3 = lambda a: a.reshape(batch, seq, a.shape[-1])
        o_mla = _attention(qt, b3(k), vt, w["bound_mla"], None, chunked=True).reshape(n, -1)
        o_fox = _attention(fqt, b3(fk), fvt, w["bound_fox"], fcum, chunked=False).reshape(n, -1)
        x1, h2, route, counts = _post_attention(xc, o_mla, o_fox, w)
        plan = _route_plan(route, counts, n)
        parts = d // LANES
        xs = _dispatch(h2, plan, parts)
        ys = _experts(xs, plan, i, w_exp_gate, w_exp_up, w_exp_down, parts)
        xc = _combine_ple(x1, route, plan["dest"], ys, p.reshape(depth, n, -1), i, w)
    return xc.reshape(batch, seq, d)
```

```python
import functools
import math

import numpy as np
import jax
import jax.numpy as jnp
from jax import lax
from jax.experimental import pallas as pl
from jax.experimental.pallas import tpu as pltpu

F32 = jnp.float32
BF16 = jnp.bfloat16

EPS = 1e-6
NEG_INF = -1e30
ROPE_THETA = 10000.0
LOG2E = math.log2(math.e)

LANES = 128
CHUNK = 64
HEADS = 8
MLA_NOPE = 64
MLA_ROPE = 32
MLA_QK = MLA_NOPE + MLA_ROPE
HEAD_V = 64
FOX_DIM = 64
Q_LORA = 256
KV_LORA = 128
N_GROUPS = 4
EXPERTS_PER_GROUP = 8
N_EXPERTS = N_GROUPS * EXPERTS_PER_GROUP
EXPERT_FF = 256
ROUTE_LANE0 = N_GROUPS

AUG_Q = FOX_DIM
AUG_K = FOX_DIM + 3
PIECE_STRIDE = 16
ONE_LANE = 3 * PIECE_STRIDE

TOKEN_TILE = 512
POST_TILE = 1024
ATTN_TQ = 1024
ATTN_TK = 1024
SUM_ROWS = 16
EXPERT_BLOCK = 256
DISPATCH_TILE = 1024
PLE_TILE = 256
ISSUE_UNROLL = 8
VMEM_LIMIT = 56 * 1024 * 1024
SAFE_SCORE_BOUND = 40.0
SKIP_LOG2 = 160.0

_NT = (((1,), (1,)), ((), ()))


def _cparams(n_axes):
    return pltpu.CompilerParams(dimension_semantics=("arbitrary",) * n_axes,
                                vmem_limit_bytes=VMEM_LIMIT)


def _rms(x, g):
    return x * lax.rsqrt(jnp.mean(x * x, axis=-1, keepdims=True) + EPS) * g


def _split3(x):
    hi = x.astype(BF16)
    r1 = x - hi.astype(F32)
    mid = r1.astype(BF16)
    lo = (r1 - mid.astype(F32)).astype(BF16)
    return hi, mid, lo


def _store_row_tiles(ref, v):
    parts = v.shape[1] // LANES
    for s in range(parts):
        ref[pl.ds(s, v.shape[0], stride=parts), :] = v[:, s * LANES:(s + 1) * LANES]


def _load_row_tiles(ref, rows, parts):
    return jnp.concatenate([ref[pl.ds(s, rows, stride=parts), :] for s in range(parts)], axis=1)


def _lane_tile(a, width):
    return jnp.tile(a, (1, width // LANES))


def _rope_kernel(pos_col_ref, pos_row_ref, invf_lane_ref, sign_ref, invf_rep_ref,
                 cos_ref, sin_ref, cost_ref, sint_ref):
    ang = pos_col_ref[...] * invf_lane_ref[...]
    cos_ref[...] = jnp.cos(ang)
    sin_ref[...] = jnp.sin(ang) * sign_ref[...]
    t = pos_row_ref.shape[-1]
    ang_t = _lane_tile(invf_rep_ref[...], t) * pos_row_ref[0]
    cost_ref[...] = jnp.cos(ang_t)
    sint_ref[...] = jnp.sin(ang_t)


def _rope_tables(positions):
    n = positions.size
    half = MLA_ROPE // 2
    inv_freq = ROPE_THETA ** (-np.arange(0, MLA_ROPE, 2, dtype=np.float32) / MLA_ROPE)
    invf = np.zeros((1, LANES), np.float32)
    sign = np.zeros((1, LANES), np.float32)
    invf[0, MLA_NOPE:MLA_NOPE + half] = inv_freq
    invf[0, MLA_NOPE + half:MLA_QK] = inv_freq
    sign[0, MLA_NOPE:MLA_NOPE + half] = -1.0
    sign[0, MLA_NOPE + half:MLA_QK] = 1.0
    invf_rep = np.broadcast_to(inv_freq[:, None], (half, LANES)).astype(np.float32)
    pos = positions.astype(F32)
    t = TOKEN_TILE
    const = lambda i: (0, 0)
    return pl.pallas_call(
        _rope_kernel,
        out_shape=(jax.ShapeDtypeStruct((n, LANES), F32), jax.ShapeDtypeStruct((n, LANES), F32),
                   jax.ShapeDtypeStruct((half, n), F32), jax.ShapeDtypeStruct((half, n), F32)),
        grid=(n // t,),
        in_specs=[pl.BlockSpec((t, 1), lambda i: (i, 0)),
                  pl.BlockSpec((1, 1, t), lambda i: (i, 0, 0)),
                  pl.BlockSpec((1, LANES), const), pl.BlockSpec((1, LANES), const),
                  pl.BlockSpec((half, LANES), const)],
        out_specs=(pl.BlockSpec((t, LANES), lambda i: (i, 0)), pl.BlockSpec((t, LANES), lambda i: (i, 0)),
                   pl.BlockSpec((half, t), lambda i: (0, i)), pl.BlockSpec((half, t), lambda i: (0, i))),
        compiler_params=_cparams(1),
        name="rope_tables",
    )(pos.reshape(n, 1), pos.reshape(n // t, 1, t), jnp.asarray(invf), jnp.asarray(sign),
      jnp.asarray(invf_rep))


def _pre_kernel(x_ref, cos_ref, sin_ref, cost_ref, sint_ref, gattn_ref, wa_ref, wfk_ref, wfqt_ref, wfvt_ref,
                bfl_ref, tri_ref, plk_ref, plq_ref, gql_ref, wuqt_ref, gkvl_ref, wuk_ref, wuvt_ref,
                gq_ref, gk_ref, gfq_ref, gfk_ref,
                qt_out, k_out, vt_out, fqt_out, fk_out, fvt_out, fcum_out, carry_ref, *, tiles_per_seq):
    step = pl.program_id(0)
    x = x_ref[...]
    t = x.shape[0]
    h = _rms(x, gattn_ref[...]).astype(BF16)
    pa = jnp.dot(h, wa_ref[...], preferred_element_type=F32)
    half = MLA_ROPE // 2
    lane = lax.broadcasted_iota(jnp.int32, (1, LANES), 1)
    low = lane < FOX_DIM

    def head_lanes(m, hd):
        v = m[:, (hd // 2) * LANES:(hd // 2 + 1) * LANES]
        if hd % 2:
            v = pltpu.roll(v, LANES // 2, 1)
        return jnp.where(low, v, 0.0)

    qn = _rms(pa[:, :Q_LORA], gql_ref[...]).astype(BF16)
    kvn = _rms(pa[:, Q_LORA:Q_LORA + KV_LORA], gkvl_ref[...]).astype(BF16)
    vt_out[0] = lax.dot_general(wuvt_ref[...], kvn, _NT, preferred_element_type=F32).astype(BF16)

    kn = jnp.dot(kvn, wuk_ref[...], preferred_element_type=F32)
    kpe = pa[:, Q_LORA + KV_LORA:Q_LORA + KV_LORA + LANES]
    cos_l = cos_ref[...]
    sin_l = sin_ref[...]
    gk = gk_ref[...]
    for hd in range(HEADS):
        v = head_lanes(kn, hd) + kpe
        v = v * lax.rsqrt(jnp.sum(v * v, axis=-1, keepdims=True) * (1.0 / MLA_QK) + EPS) * gk
        swapped = jnp.where(lane < MLA_NOPE + half, pltpu.roll(v, LANES - half, 1), pltpu.roll(v, half, 1))
        k_out[:, hd * LANES:(hd + 1) * LANES] = (v * cos_l + swapped * sin_l).astype(BF16)

    qt = lax.dot_general(wuqt_ref[...], qn, _NT, preferred_element_type=F32)
    cos_r = cost_ref[...]
    sin_r = sint_ref[...]
    gq = _lane_tile(gq_ref[...], t)[:MLA_QK]
    pad_q = jnp.zeros((LANES - MLA_QK, t), BF16)
    for hd in range(HEADS):
        blk = qt[hd * MLA_QK:(hd + 1) * MLA_QK]
        r = lax.rsqrt(jnp.sum(blk * blk, axis=0, keepdims=True) * (1.0 / MLA_QK) + EPS)
        blk = blk * r * gq
        x1, x2 = blk[MLA_NOPE:MLA_NOPE + half], blk[MLA_NOPE + half:]
        blk = jnp.concatenate([blk[:MLA_NOPE], x1 * cos_r - x2 * sin_r, x2 * cos_r + x1 * sin_r], axis=0)
        qt_out[0, hd * LANES:(hd + 1) * LANES, :] = jnp.concatenate([blk.astype(BF16), pad_q], axis=0)

    fvt_out[0] = lax.dot_general(wfvt_ref[...], h, _NT, preferred_element_type=F32).astype(BF16)

    def by_group(a, b, c):
        return jnp.where(lane < PIECE_STRIDE, a, jnp.where(lane < 2 * PIECE_STRIDE, b, c))

    z = pa[:, Q_LORA + KV_LORA + LANES:] + bfl_ref[...]
    logf = jnp.minimum(z, 0.0) - jnp.log1p(jnp.exp(-jnp.abs(z)))
    parts = jnp.dot(tri_ref[...], by_group(*_split3(logf)), preferred_element_type=F32)
    tot = parts + pltpu.roll(parts, PIECE_STRIDE, 1) + pltpu.roll(parts, 2 * PIECE_STRIDE, 1)
    cum = by_group(pltpu.roll(tot, LANES - 2 * PIECE_STRIDE, 1), pltpu.roll(tot, LANES - PIECE_STRIDE, 1), tot)

    @pl.when(step % tiles_per_seq == 0)
    def _():
        carry_ref[...] = jnp.zeros_like(carry_ref)

    cum = cum + carry_ref[:1, :]
    carry_ref[...] = jnp.broadcast_to(cum[t - 1:t, :], carry_ref.shape)
    cum2 = cum * LOG2E
    fcum_out[...] = cum2
    packed = by_group(*_split3(cum2))
    packed = jnp.where(lane == ONE_LANE, jnp.ones_like(packed), packed)
    aug_k = jnp.dot(packed, plk_ref[...], preferred_element_type=F32)
    aug_q = lax.dot_general(plq_ref[...], packed, _NT, preferred_element_type=F32)

    fk = jnp.dot(h, wfk_ref[...], preferred_element_type=F32)
    gfk = gfk_ref[...]
    for hd in range(HEADS):
        sl = slice(hd * LANES, (hd + 1) * LANES)
        v = head_lanes(fk, hd)
        v = v * lax.rsqrt(jnp.sum(v * v, axis=-1, keepdims=True) * (1.0 / FOX_DIM) + EPS) * gfk
        fk_out[:, sl] = (v + aug_k[:, sl]).astype(BF16)

    fqt = lax.dot_general(wfqt_ref[...], h, _NT, preferred_element_type=F32)
    gfq = _lane_tile(gfq_ref[...], t)[:FOX_DIM]
    for hd in range(HEADS):
        blk = fqt[hd * FOX_DIM:(hd + 1) * FOX_DIM]
        r = lax.rsqrt(jnp.sum(blk * blk, axis=0, keepdims=True) * (1.0 / FOX_DIM) + EPS)
        aug = aug_q[hd * LANES + FOX_DIM:(hd + 1) * LANES]
        fqt_out[0, hd * LANES:(hd + 1) * LANES, :] = jnp.concatenate([blk * r * gfq, aug], axis=0).astype(BF16)


def _pre_attention(x2d, tables, w, seq):
    n, d = x2d.shape
    t = TOKEN_TILE
    tiles_per_seq = seq // t
    batch = n // seq
    cos_l, sin_l, cos_t, sin_t = tables
    half = MLA_ROPE // 2
    row = lambda i: (i, 0)
    seq_t = lambda i: (i // tiles_per_seq, 0, i % tiles_per_seq)

    def full(a):
        return pl.BlockSpec(a.shape, lambda i, nd=a.ndim: (0,) * nd)

    weights = [w["g_attn"], w["w_a"], w["w_fk"], w["w_fq_t"], w["w_fv_t"], w["b_fl"], w["tri_incl"],
               w["place_k"], w["place_q"], w["g_q_lora"], w["w_uq_t"], w["g_kv_lora"], w["w_uk"], w["w_uv_t"],
               w["g_mla_q"], w["g_mla_k"], w["g_fox_q"], w["g_fox_k"]]
    in_specs = [pl.BlockSpec((t, d), row), pl.BlockSpec((t, LANES), row), pl.BlockSpec((t, LANES), row),
                pl.BlockSpec((half, t), lambda i: (0, i)), pl.BlockSpec((half, t), lambda i: (0, i))]
    in_specs += [full(a) for a in weights]
    wq = HEADS * LANES
    wv = HEADS * HEAD_V
    out_shape = (jax.ShapeDtypeStruct((batch, wq, seq), BF16), jax.ShapeDtypeStruct((n, wq), BF16),
                 jax.ShapeDtypeStruct((batch, wv, seq), BF16), jax.ShapeDtypeStruct((batch, wq, seq), BF16),
                 jax.ShapeDtypeStruct((n, wq), BF16), jax.ShapeDtypeStruct((batch, wv, seq), BF16),
                 jax.ShapeDtypeStruct((n, LANES), F32))
    out_specs = (pl.BlockSpec((1, wq, t), seq_t), pl.BlockSpec((t, wq), row), pl.BlockSpec((1, wv, t), seq_t),
                 pl.BlockSpec((1, wq, t), seq_t), pl.BlockSpec((t, wq), row), pl.BlockSpec((1, wv, t), seq_t),
                 pl.BlockSpec((t, LANES), row))
    return pl.pallas_call(
        functools.partial(_pre_kernel, tiles_per_seq=tiles_per_seq),
        out_shape=out_shape,
        grid=(n // t,),
        in_specs=in_specs,
        out_specs=out_specs,
        scratch_shapes=[pltpu.VMEM((8, LANES), F32)],
        compiler_params=_cparams(1),
        name="pre_attention",
    )(x2d, cos_l, sin_l, cos_t, sin_t, *weights)


def _allowed(k0, q0, tk, tq, chunked):
    key = k0 + lax.broadcasted_iota(jnp.int32, (tk, tq), 0)
    qry = q0 + lax.broadcasted_iota(jnp.int32, (tk, tq), 1)
    if chunked:
        return (key // CHUNK) <= (qry // CHUNK)
    return key <= qry


def _rowmax_kernel(qt_ref, k_ref, m_ref, *, chunked, tq, tk):
    qi = pl.program_id(2)
    n_diag = tq // tk

    def block(kb, carry, masked):
        k0 = pl.multiple_of(kb * tk, tk)
        kblk = k_ref[0, pl.ds(k0, tk), :]
        out = []
        for i in range(2):
            s = jnp.dot(kblk[:, i * LANES:(i + 1) * LANES], qt_ref[0, i * LANES:(i + 1) * LANES, :],
                        preferred_element_type=F32)
            if masked:
                s = jnp.where(_allowed(k0, qi * tq, tk, tq, chunked), s, NEG_INF)
            out.append(jnp.maximum(carry[i], jnp.max(s, axis=0, keepdims=True)))
        return tuple(out)

    init = tuple(jnp.full((1, tq), NEG_INF, F32) for _ in range(2))
    carry = lax.fori_loop(0, qi * n_diag, lambda kb, c: block(kb, c, False), init)
    for j in range(n_diag):
        carry = block(qi * n_diag + j, carry, True)
    m_ref[0, 0] = jnp.concatenate(carry, axis=0)


def _row_max(qt, k, *, chunked):
    batch, seq, wk = k.shape
    tq, tk = ATTN_TQ, ATTN_TK
    pairs = wk // (2 * LANES)
    out = pl.pallas_call(
        functools.partial(_rowmax_kernel, chunked=chunked, tq=tq, tk=tk),
        out_shape=jax.ShapeDtypeStruct((batch, pairs, 2, seq), F32),
        grid=(batch, pairs, seq // tq),
        in_specs=[pl.BlockSpec((1, 2 * LANES, tq), lambda b, h, i: (b, h, i)),
                  pl.BlockSpec((1, seq, 2 * LANES), lambda b, h, i: (b, 0, h))],
        out_specs=pl.BlockSpec((1, 1, 2, tq), lambda b, h, i: (b, h, 0, i)),
        compiler_params=_cparams(3),
        name="attn_rowmax",
    )(qt, k)
    return out


def _attn_kernel(start_ref, qt_ref, k_ref, vt_ref, shift_ref, mask_ref, o_ref, acc_ref, *, tq, tk):
    pairs = pl.num_programs(1)
    n_q = pl.num_programs(2)
    qi = pl.program_id(2)
    acc_ref[...] = jnp.zeros_like(acc_ref)

    def block(k0, nk, q_lo, masked):
        qs = slice(q_lo, tq)
        kblk = k_ref[0, pl.ds(k0, nk), :]
        vt = vt_ref[0, :, pl.ds(k0, nk)]
        ones = jnp.ones((SUM_ROWS, nk), BF16)
        for i in range(2):
            s = jnp.dot(kblk[:, i * LANES:(i + 1) * LANES], qt_ref[0, i * LANES:(i + 1) * LANES, qs],
                        preferred_element_type=F32)
            s = s - shift_ref[0, 0, i:i + 1, qs]
            if masked:
                s = s + mask_ref[q_lo:q_lo + nk, qs]
            p = jnp.exp2(s).astype(BF16)
            lhs = jnp.concatenate([vt[i * HEAD_V:(i + 1) * HEAD_V], ones], axis=0)
            acc_ref[i, :, qs] += jnp.dot(lhs, p, preferred_element_type=F32)

    def body(kb, c):
        block(pl.multiple_of(kb * tk, tk), tk, 0, False)
        return c

    first = start_ref[(pl.program_id(0) * pairs + pl.program_id(1)) * n_q + qi]
    lax.fori_loop(first, qi, body, 0)
    half = tk // 2
    q0 = pl.multiple_of(qi * tq, tq)
    block(q0, half, 0, True)
    block(pl.multiple_of(q0 + half, half), half, half, True)
    o_t = jnp.concatenate([acc_ref[i, :HEAD_V, :] / acc_ref[i, HEAD_V:HEAD_V + 1, :] for i in range(2)], axis=0)
    o_ref[0] = o_t.T.astype(o_ref.dtype)


def _diag_mask(tq, chunked):
    key = np.arange(tq)[:, None]
    qry = np.arange(tq)[None, :]
    ok = (key // CHUNK) <= (qry // CHUNK) if chunked else key <= qry
    return jnp.asarray(np.where(ok, 0.0, NEG_INF).astype(np.float32))


def _first_block(fcum, batch, seq, tq, tk):
    f = fcum.reshape(batch, seq, LANES)[:, :, :HEADS]
    f_q0 = f[:, ::tq, :]
    f_kl = f[:, tk - 1::tk, :]
    dead = (f_q0[:, :, None, :] - f_kl[:, None, :, :]) < -SKIP_LOG2
    dead = dead.reshape(batch, seq // tq, seq // tk, HEADS // 2, 2).all(axis=-1)
    lead = jnp.cumprod(dead.astype(jnp.int32), axis=2).sum(axis=2)
    limit = (jnp.arange(seq // tq, dtype=jnp.int32) * (tq // tk))[None, :, None]
    return jnp.minimum(lead, limit).transpose(0, 2, 1).reshape(-1).astype(jnp.int32)


def _attention(qt, k, vt, bound, fcum, *, chunked):
    batch, seq, wk = k.shape
    tq, tk = ATTN_TQ, ATTN_TK
    pairs = wk // (2 * LANES)
    n_q = seq // tq
    fast = bound <= SAFE_SCORE_BOUND * LOG2E
    shift = lax.cond(fast, lambda: jnp.full((batch, pairs, 2, seq), bound, F32),
                     lambda: _row_max(qt, k, chunked=chunked))
    first = jnp.zeros((batch * pairs * n_q,), jnp.int32)
    if fcum is not None:
        first = jnp.where(fast, _first_block(fcum, batch, seq, tq, tk), first)
    return pl.pallas_call(
        functools.partial(_attn_kernel, tq=tq, tk=tk),
        out_shape=jax.ShapeDtypeStruct((batch, seq, pairs * 2 * HEAD_V), BF16),
        grid_spec=pltpu.PrefetchScalarGridSpec(
            num_scalar_prefetch=1,
            grid=(batch, pairs, n_q),
            in_specs=[pl.BlockSpec((1, 2 * LANES, tq), lambda b, h, i, st: (b, h, i)),
                      pl.BlockSpec((1, seq, 2 * LANES), lambda b, h, i, st: (b, 0, h)),
                      pl.BlockSpec((1, 2 * HEAD_V, seq), lambda b, h, i, st: (b, h, 0)),
                      pl.BlockSpec((1, 1, 2, tq), lambda b, h, i, st: (b, h, 0, i)),
                      pl.BlockSpec((tq, tq), lambda b, h, i, st: (0, 0))],
            out_specs=pl.BlockSpec((1, tq, 2 * HEAD_V), lambda b, h, i, st: (b, i, h)),
            scratch_shapes=[pltpu.VMEM((2, HEAD_V + SUM_ROWS, tq), F32)]),
        compiler_params=_cparams(3),
        name="attn_chunk_causal" if chunked else "attn_frame_causal",
    )(first, qt, k, vt, shift, _diag_mask(tq, chunked))


def _post_kernel(x_ref, om_ref, of_ref, gom_ref, gof_ref, wom_ref, wof_ref, gffn_ref, wrh_ref, wrl_ref,
                 tri_ref, x1_out, h2_out, route_out, count_out, carry_ref):
    step = pl.program_id(0)
    ym = _rms(om_ref[...].astype(F32), gom_ref[...]).astype(BF16)
    yf = _rms(of_ref[...].astype(F32), gof_ref[...]).astype(BF16)
    x1 = (x_ref[...] + jnp.dot(ym, wom_ref[...], preferred_element_type=F32)
          + jnp.dot(yf, wof_ref[...], preferred_element_type=F32))
    x1_out[...] = x1
    h2 = _rms(x1, gffn_ref[...])
    _store_row_tiles(h2_out, h2)

    h_hi = h2.astype(BF16)
    h_lo = (h2 - h_hi.astype(F32)).astype(BF16)
    w_hi = wrh_ref[...]
    logits = (jnp.dot(h_hi, w_hi, preferred_element_type=F32)
              + jnp.dot(h_lo, w_hi, preferred_element_type=F32)
              + jnp.dot(h_hi, wrl_ref[...], preferred_element_type=F32))

    lane = lax.broadcasted_iota(jnp.int32, (1, LANES), 1).astype(F32)
    big = float(LANES)

    def first_argmax(v):
        mx = jnp.max(v, axis=-1, keepdims=True)
        idx = jnp.min(jnp.where(v == mx, lane, big), axis=-1, keepdims=True)
        return mx, idx

    lg = jnp.where(lane < N_GROUPS, logits, NEG_INF)
    mg, g_idx = first_argmax(lg)
    p_g = 1.0 / jnp.sum(jnp.exp(lg - mg), axis=-1, keepdims=True)
    e_lo = ROUTE_LANE0 + EXPERTS_PER_GROUP * g_idx
    le = jnp.where((lane >= e_lo) & (lane < e_lo + EXPERTS_PER_GROUP), logits, NEG_INF)
    m1, i1 = first_argmax(le)
    m2, i2 = first_argmax(jnp.where(lane == i1, NEG_INF, le))
    e2 = jnp.exp(m2 - m1)
    gate1 = p_g / (1.0 + e2)
    gate2 = p_g * e2 / (1.0 + e2)

    oh1 = lane == i1
    oh2 = lane == i2
    oh = (jnp.where(oh1, 1.0, 0.0) + jnp.where(oh2, 1.0, 0.0))
    before = jnp.dot(tri_ref[...], oh.astype(BF16), preferred_element_type=F32)

    @pl.when(step == 0)
    def _():
        carry_ref[...] = jnp.zeros_like(carry_ref)

    base = before + carry_ref[:1, :]
    rank1 = jnp.sum(jnp.where(oh1, base, 0.0), axis=-1, keepdims=True)
    rank2 = jnp.sum(jnp.where(oh2, base, 0.0), axis=-1, keepdims=True)
    total = carry_ref[:1, :] + jnp.sum(oh, axis=0, keepdims=True)
    carry_ref[...] = jnp.broadcast_to(total, carry_ref.shape)
    count_out[...] = jnp.broadcast_to(total, count_out.shape)

    route = jnp.where(lane == 0, i1 - ROUTE_LANE0, 0.0)
    route = jnp.where(lane == 1, i2 - ROUTE_LANE0, route)
    route = jnp.where(lane == 2, rank1, route)
    route = jnp.where(lane == 3, rank2, route)
    route = jnp.where(lane == 4, gate1, route)
    route = jnp.where(lane == 5, gate2, route)
    route_out[...] = route


def _post_attention(x2d, o_mla, o_fox, w):
    n, d = x2d.shape
    t = POST_TILE
    parts = d // LANES
    const = lambda i: (0, 0)
    row = lambda i: (i, 0)
    full = lambda a: pl.BlockSpec(a.shape, const)
    ins = [x2d, o_mla, o_fox, w["g_out_mla"], w["g_out_fox"], w["w_out_mla"], w["w_out_fox"],
           w["g_ffn"], w["w_router_hi"], w["w_router_lo"], w["tri_strict"]]
    in_specs = [pl.BlockSpec((t, d), row), pl.BlockSpec((t, o_mla.shape[1]), row),
                pl.BlockSpec((t, o_fox.shape[1]), row)] + [full(a) for a in ins[3:]]
    return pl.pallas_call(
        _post_kernel,
        out_shape=(jax.ShapeDtypeStruct((n, d), F32), jax.ShapeDtypeStruct((n * parts, LANES), F32),
                   jax.ShapeDtypeStruct((n, LANES), F32), jax.ShapeDtypeStruct((8, LANES), F32)),
        grid=(n // t,),
        in_specs=in_specs,
        out_specs=(pl.BlockSpec((t, d), row), pl.BlockSpec((t * parts, LANES), row),
                   pl.BlockSpec((t, LANES), row), pl.BlockSpec((8, LANES), const)),
        scratch_shapes=[pltpu.VMEM((8, LANES), F32)],
        compiler_params=_cparams(1),
        name="post_attention",
    )(*ins)


def _dispatch_kernel(pad_end_ref, padded_ref, n_used_ref, dest_ref, h_ref, xs_hbm, zero_ref, sem, zsem, *,
                     tile, parts):
    zrows = zero_ref.shape[0]
    n_blk = xs_hbm.shape[0] // zrows
    rblk = zrows // parts

    def zero_copy(blk):
        return pltpu.make_async_copy(zero_ref, xs_hbm.at[pl.ds(pl.multiple_of(blk * zrows, zrows), zrows)], zsem)

    @pl.when(pl.program_id(0) == 0)
    def _():
        zero_ref[...] = jnp.zeros_like(zero_ref)
        for wait in (False, True):
            for e in range(N_EXPERTS):
                @pl.when(padded_ref[e] > 0)
                def _():
                    cp = zero_copy(pad_end_ref[e] // rblk - 1)
                    cp.wait() if wait else cp.start()

            def tail(blk, c):
                cp = zero_copy(blk)
                cp.wait() if wait else cp.start()
                return c

            lax.fori_loop(n_used_ref[0], n_blk, tail, 0)

    def issue(jo, c):
        for ji in range(ISSUE_UNROLL):
            j = jo * ISSUE_UNROLL + ji
            src = h_ref.at[pl.ds(pl.multiple_of(j * parts, parts), parts)]
            for k in range(2):
                d = pl.multiple_of(dest_ref[0, 0, 2 * j + k], parts)
                pltpu.make_async_copy(src, xs_hbm.at[pl.ds(d, parts)], sem).start(priority=k)
        return c

    lax.fori_loop(0, tile // ISSUE_UNROLL, issue, 0)
    for _ in range(2):
        pltpu.make_async_copy(h_ref, xs_hbm.at[pl.ds(0, tile * parts)], sem).wait()


def _dispatch(h2t, plan, parts):
    n = h2t.shape[0] // parts
    tile = DISPATCH_TILE
    dest3 = (plan["dest"] * parts).reshape(n // tile, 1, 2 * tile)
    return pl.pallas_call(
        functools.partial(_dispatch_kernel, tile=tile, parts=parts),
        out_shape=jax.ShapeDtypeStruct((plan["n_rows"] * parts, LANES), h2t.dtype),
        grid_spec=pltpu.PrefetchScalarGridSpec(
            num_scalar_prefetch=3,
            grid=(n // tile,),
            in_specs=[pl.BlockSpec((1, 1, 2 * tile), lambda i, *_: (i, 0, 0), memory_space=pltpu.SMEM),
                      pl.BlockSpec((tile * parts, LANES), lambda i, *_: (i, 0))],
            out_specs=pl.BlockSpec(memory_space=pl.ANY),
            scratch_shapes=[pltpu.VMEM((EXPERT_BLOCK * parts, LANES), h2t.dtype), pltpu.SemaphoreType.DMA,
                            pltpu.SemaphoreType.DMA]),
        compiler_params=_cparams(1),
        name="moe_dispatch",
    )(plan["pad_end"], plan["padded"], plan["n_used"], dest3, h2t)


def _expert_kernel(blk_e_ref, n_used_ref, x_ref, wg_ref, wu_ref, wdn_ref, y_ref, wgu_bf, wd_bf, *, parts):
    i = pl.program_id(0)
    used = i < n_used_ref[0]
    rows = x_ref.shape[0] // parts
    fresh = jnp.logical_or(i == 0, blk_e_ref[i] != blk_e_ref[jnp.maximum(i - 1, 0)])

    @pl.when(jnp.logical_and(used, fresh))
    def _():
        wgu_bf[:, :EXPERT_FF] = wg_ref[0, 0].astype(BF16)
        wgu_bf[:, EXPERT_FF:] = wu_ref[0, 0].astype(BF16)
        wd_bf[...] = wdn_ref[0, 0].astype(BF16)

    @pl.when(used)
    def _():
        xb = _load_row_tiles(x_ref, rows, parts).astype(BF16)
        gu = jnp.dot(xb, wgu_bf[...], preferred_element_type=F32)
        g = gu[:, :EXPERT_FF]
        act = (g / (1.0 + jnp.exp(-g))) * gu[:, EXPERT_FF:]
        _store_row_tiles(y_ref, jnp.dot(act.astype(BF16), wd_bf[...], preferred_element_type=F32))

    @pl.when(jnp.logical_not(used))
    def _():
        y_ref[...] = jnp.zeros_like(y_ref)


def _experts(xs, plan, layer, w_gate, w_up, w_down, parts):
    r = EXPERT_BLOCK
    d = parts * LANES
    rows = lambda i, be, nu: (jnp.minimum(i, nu[0] - 1), 0)
    wsel = lambda i, be, nu: (layer, be[i], 0, 0)
    return pl.pallas_call(
        functools.partial(_expert_kernel, parts=parts),
        out_shape=jax.ShapeDtypeStruct(xs.shape, F32),
        grid_spec=pltpu.PrefetchScalarGridSpec(
            num_scalar_prefetch=2,
            grid=(xs.shape[0] // (r * parts),),
            in_specs=[pl.BlockSpec((r * parts, LANES), rows),
                      pl.BlockSpec((1, 1, d, EXPERT_FF), wsel),
                      pl.BlockSpec((1, 1, d, EXPERT_FF), wsel),
                      pl.BlockSpec((1, 1, EXPERT_FF, d), wsel)],
            out_specs=pl.BlockSpec((r * parts, LANES), lambda i, be, nu: (i, 0)),
            scratch_shapes=[pltpu.VMEM((d, 2 * EXPERT_FF), BF16), pltpu.VMEM((EXPERT_FF, d), BF16)]),
        compiler_params=_cparams(1),
        name="moe_experts",
    )(plan["blk_e"], plan["n_used"], xs, w_gate, w_up, w_down)


def _ple_kernel(dest_ref, dest_next_ref, x1_ref, route_ref, p_ref, gnorm_ref, wgate_ref, wproj_ref, gout_ref,
                ys_hbm, x_out, rows_a, rows_b, sem, *, tile, parts):
    step = pl.program_id(0)
    n_steps = pl.num_programs(0)

    def row_copy(d, buf, slot, k, j):
        return pltpu.make_async_copy(ys_hbm.at[pl.ds(pl.multiple_of(d, parts), parts)],
                                     buf.at[k, pl.ds(j * parts, parts)], sem.at[slot])

    def wait_rows(buf, slot):
        for k in range(2):
            pltpu.make_async_copy(ys_hbm.at[pl.ds(0, tile * parts)], buf.at[k], sem.at[slot]).wait()

    @pl.when(step == 0)
    def _():
        def issue(jo, c):
            for ji in range(ISSUE_UNROLL):
                j = jo * ISSUE_UNROLL + ji
                for k in range(2):
                    pltpu.make_async_copy(
                        ys_hbm.at[pl.ds(pl.multiple_of(dest_ref[0, 0, 2 * j + k], parts), parts)],
                        rows_a.at[k, pl.ds(pl.multiple_of(j * parts, parts), parts)], sem.at[0]).start(priority=k)
            return c
        lax.fori_loop(0, tile // ISSUE_UNROLL, issue, 0)

    def body(cur, nxt, slot):
        wait_rows(cur, slot)
        for j in range(tile):
            for k in range(2):
                row_copy(dest_next_ref[0, 0, 2 * j + k], nxt, 1 - slot, k, j).start(priority=k)
        route = route_ref[...]
        y1 = _load_row_tiles(cur.at[0], tile, parts)
        y2 = _load_row_tiles(cur.at[1], tile, parts)
        x2 = x1_ref[...] + route[:, 4:5] * y1 + route[:, 5:6] * y2
        ple = _rms(jnp.dot(p_ref[0].astype(BF16), wproj_ref[...], preferred_element_type=F32), gout_ref[...])
        z = jnp.dot(_rms(x2, gnorm_ref[...]).astype(BF16), wgate_ref[...], preferred_element_type=F32)
        x_out[...] = x2 + ple / (1.0 + jnp.exp(-z))

        @pl.when(step == n_steps - 1)
        def _():
            wait_rows(nxt, 1 - slot)

    @pl.when(step % 2 == 0)
    def _():
        body(rows_a, rows_b, 0)

    @pl.when(step % 2 == 1)
    def _():
        body(rows_b, rows_a, 1)


def _combine_ple(x1, route, dest, ys, p3d, layer, w):
    n, d = x1.shape
    t = PLE_TILE
    parts = d // LANES
    steps = n // t
    const = lambda i: (0, 0)
    row = lambda i: (i, 0)
    full = lambda a: pl.BlockSpec(a.shape, const)
    dest3 = (dest * parts).reshape(steps, 1, 2 * t)
    return pl.pallas_call(
        functools.partial(_ple_kernel, tile=t, parts=parts),
        out_shape=jax.ShapeDtypeStruct((n, d), F32),
        grid=(steps,),
        in_specs=[pl.BlockSpec((1, 1, 2 * t), lambda i: (i, 0, 0), memory_space=pltpu.SMEM),
                  pl.BlockSpec((1, 1, 2 * t), lambda i: (jnp.minimum(i + 1, steps - 1), 0, 0),
                               memory_space=pltpu.SMEM),
                  pl.BlockSpec((t, d), row), pl.BlockSpec((t, LANES), row),
                  pl.BlockSpec((1, t, p3d.shape[2]), lambda i: (layer, i, 0)),
                  full(w["g_ple_norm"]), full(w["w_ple_gate"]), full(w["w_ple_proj"]), full(w["g_ple_out"]),
                  pl.BlockSpec(memory_space=pl.ANY)],
        out_specs=pl.BlockSpec((t, d), row),
        scratch_shapes=[pltpu.VMEM((2, t * parts, LANES), F32), pltpu.VMEM((2, t * parts, LANES), F32),
                        pltpu.SemaphoreType.DMA((2,))],
        compiler_params=_cparams(1),
        name="moe_combine_ple",
    )(dest3, dest3, x1, route, p3d, w["g_ple_norm"], w["w_ple_gate"], w["w_ple_proj"], w["g_ple_out"], ys)


def _row(v):
    return v.reshape(1, -1).astype(F32)


def _col_rep(v):
    return jnp.broadcast_to(v.astype(F32)[:, None], (v.shape[0], LANES))


def _pad_heads(wmat, real):
    k = wmat.shape[0]
    return jnp.pad(wmat.reshape(k, HEADS, real), ((0, 0), (0, 0), (0, LANES - real))).reshape(k, HEADS * LANES)


def _placement():
    place_k = np.zeros((LANES, HEADS * LANES), np.float32)
    place_q = np.zeros((HEADS * LANES, LANES), np.float32)
    for hd in range(HEADS):
        base = hd * LANES
        for j in range(3):
            src = j * PIECE_STRIDE + hd
            place_k[src, base + AUG_K + j] = -1.0
            place_q[base + AUG_Q + j, src] = 1.0
            place_k[ONE_LANE, base + AUG_Q + j] = 1.0
            place_q[base + AUG_K + j, ONE_LANE] = 1.0
    return jnp.asarray(place_k, BF16), jnp.asarray(place_q, BF16)


def _layer_weights(i, g_attn_norm, w_in, g_q_lora, w_uq, g_kv_lora, w_ukv, g_mla_q, g_mla_k, g_fox_q,
                   g_fox_k, b_fox_f, g_out_mla, g_out_fox, w_out, g_ffn_norm, w_router_group,
                   w_router_expert, w_exp_gate, w_exp_up, w_exp_down, g_ple_norm, w_ple_gate, w_ple_proj,
                   g_ple_out):
    d = w_in.shape[1]
    wf = HEADS * FOX_DIM
    c_kv = Q_LORA
    c_pe = c_kv + KV_LORA
    c_fq = c_pe + MLA_ROPE
    c_fk = c_fq + wf
    c_fv = c_fk + wf
    c_fl = c_fv + wf
    win = w_in[i]
    pad_pe = jnp.zeros((d, LANES), F32).at[:, MLA_NOPE:MLA_QK].set(win[:, c_pe:c_fq])
    group = lambda v: jnp.concatenate(
        [jnp.pad(v, [(0, 0)] * (v.ndim - 1) + [(0, PIECE_STRIDE - HEADS)])] * 3
        + [jnp.zeros(v.shape[:-1] + (LANES - 3 * PIECE_STRIDE,), v.dtype)], axis=-1)
    pad_fl = group(win[:, c_fl:])
    w = {}
    w["g_attn"] = _row(g_attn_norm[i])
    w["w_a"] = jnp.concatenate([win[:, :c_pe], pad_pe, pad_fl], axis=1).astype(BF16)
    w["w_fk"] = win[:, c_fk:c_fv].astype(BF16)
    w["w_fq_t"] = win[:, c_fq:c_fk].T.astype(BF16)
    w["w_fv_t"] = win[:, c_fv:c_fl].T.astype(BF16)
    w["b_fl"] = group(b_fox_f[i].astype(F32)[None, :])
    w["place_k"], w["place_q"] = _placement()
    w["g_q_lora"] = _row(g_q_lora[i])
    w["w_uq_t"] = w_uq[i].T.astype(BF16)
    w["g_kv_lora"] = _row(g_kv_lora[i])
    ukv = w_ukv[i].reshape(KV_LORA, HEADS, MLA_NOPE + HEAD_V)
    w["w_uk"] = ukv[:, :, :MLA_NOPE].reshape(KV_LORA, -1).astype(BF16)
    w["w_uv_t"] = ukv[:, :, MLA_NOPE:].reshape(KV_LORA, -1).T.astype(BF16)
    w["g_mla_q"] = _col_rep(jnp.pad(g_mla_q[i], (0, LANES - MLA_QK))) * (MLA_QK ** -0.5 * LOG2E)
    w["g_mla_k"] = _row(jnp.pad(g_mla_k[i], (0, LANES - MLA_QK)))
    w["g_fox_q"] = _col_rep(jnp.pad(g_fox_q[i], (0, LANES - FOX_DIM))) * (FOX_DIM ** -0.5 * LOG2E)
    w["g_fox_k"] = _row(jnp.pad(g_fox_k[i], (0, LANES - FOX_DIM)))
    bound = lambda gq, gk, dim: (1.02 * LOG2E * dim ** 0.5) * jnp.max(jnp.abs(gq)) * jnp.max(jnp.abs(gk))
    w["bound_mla"] = bound(g_mla_q[i], g_mla_k[i], MLA_QK).astype(F32)
    w["bound_fox"] = bound(g_fox_q[i], g_fox_k[i], FOX_DIM).astype(F32)
    w["g_out_mla"] = _row(g_out_mla[i])
    w["g_out_fox"] = _row(g_out_fox[i])
    wm = HEADS * HEAD_V
    w["w_out_mla"] = w_out[i, :wm].astype(BF16)
    w["w_out_fox"] = w_out[i, wm:].astype(BF16)
    w["g_ffn"] = _row(g_ffn_norm[i])
    wr = jnp.zeros((d, LANES), F32)
    wr = wr.at[:, :N_GROUPS].set(w_router_group[i]).at[:, ROUTE_LANE0:ROUTE_LANE0 + N_EXPERTS].set(
        w_router_expert[i])
    w["w_router_hi"] = wr.astype(BF16)
    w["w_router_lo"] = (wr - w["w_router_hi"].astype(F32)).astype(BF16)
    w["g_ple_norm"] = _row(g_ple_norm[i])
    w["w_ple_gate"] = w_ple_gate[i].astype(BF16)
    w["w_ple_proj"] = w_ple_proj[i].astype(BF16)
    w["g_ple_out"] = _row(g_ple_out[i])
    def tri(t, strict):
        r = lax.broadcasted_iota(jnp.int32, (t, t), 0)
        c = lax.broadcasted_iota(jnp.int32, (t, t), 1)
        return ((c < r) if strict else (c <= r)).astype(BF16)

    w["tri_incl"] = tri(TOKEN_TILE, False)
    w["tri_strict"] = tri(POST_TILE, True)
    return w


def _route_plan(route, counts, n):
    r = EXPERT_BLOCK
    cnt = counts[0, ROUTE_LANE0:ROUTE_LANE0 + N_EXPERTS].astype(jnp.int32)
    padded = ((cnt + r - 1) // r) * r
    pad_end = jnp.cumsum(padded)
    pad_start = pad_end - padded
    e = route[:, 0:2].astype(jnp.int32)
    rank = route[:, 2:4].astype(jnp.int32)
    onehot = e[:, :, None] == jnp.arange(N_EXPERTS, dtype=jnp.int32)[None, None, :]
    dest = jnp.sum(jnp.where(onehot, pad_start[None, None, :], 0), axis=-1) + rank
    n_rows = 2 * n + N_EXPERTS * r
    blk_start = jnp.arange(n_rows // r, dtype=jnp.int32) * r
    blk_e = jnp.sum((blk_start[:, None] >= pad_end[None, :]).astype(jnp.int32), axis=1)
    blk_e = jnp.minimum(blk_e, N_EXPERTS - 1)
    n_used = (pad_end[-1:] // r).astype(jnp.int32)
    return {"dest": dest.reshape(-1), "blk_e": blk_e, "n_used": n_used, "n_rows": n_rows,
            "pad_end": pad_end.astype(jnp.int32), "padded": padded.astype(jnp.int32)}


def kernel(x, p, positions, g_attn_norm, w_in, g_q_lora, w_uq, g_kv_lora, w_ukv, g_mla_q, g_mla_k, g_fox_q,
           g_fox_k, b_fox_f, g_out_mla, g_out_fox, w_out, g_ffn_norm, w_router_group, w_router_expert,
           w_exp_gate, w_exp_up, w_exp_down, g_ple_norm, w_ple_gate, w_ple_proj, g_ple_out):
    batch, seq, d = x.shape
    n = batch * seq
    depth = w_in.shape[0]
    params = (g_attn_norm, w_in, g_q_lora, w_uq, g_kv_lora, w_ukv, g_mla_q, g_mla_k, g_fox_q, g_fox_k,
              b_fox_f, g_out_mla, g_out_fox, w_out, g_ffn_norm, w_router_group, w_router_expert,
              w_exp_gate, w_exp_up, w_exp_down, g_ple_norm, w_ple_gate, w_ple_proj, g_ple_out)
    tables = _rope_tables(positions)
    xc = x.reshape(n, d)
    for i in range(depth):
        w = _layer_weights(i, *params)
        qt, k, vt, fqt, fk, fvt, fcum = _pre_attention(xc, tables, w, seq)
        b3 = lambda a: a.reshape(batch, seq, a.shape[-1])
        o_mla = _attention(qt, b3(k), vt, w["bound_mla"], None, chunked=True).reshape(n, -1)
        o_fox = _attention(fqt, b3(fk), fvt, w["bound_fox"], fcum, chunked=False).reshape(n, -1)
        x1, h2, route, counts = _post_attention(xc, o_mla, o_fox, w)
        plan = _route_plan(route, counts, n)
        parts = d // LANES
        xs = _dispatch(h2, plan, parts)
        ys = _experts(xs, plan, i, w_exp_gate, w_exp_up, w_exp_down, parts)
        xc = _combine_ple(x1, route, plan["dest"], ys, p.reshape(depth, n, -1), i, w)
    return xc.reshape(batch, seq, d)
```

```python
import functools
import math

import numpy as np
import jax
import jax.numpy as jnp
from jax import lax
from jax.experimental import pallas as pl
from jax.experimental.pallas import tpu as pltpu

F32 = jnp.float32
BF16 = jnp.bfloat16

EPS = 1e-6
NEG_INF = -1e30
ROPE_THETA = 10000.0
LOG2E = math.log2(math.e)

LANES = 128
CHUNK = 64
HEADS = 8
MLA_NOPE = 64
MLA_ROPE = 32
MLA_QK = MLA_NOPE + MLA_ROPE
HEAD_V = 64
FOX_DIM = 64
Q_LORA = 256
KV_LORA = 128
N_GROUPS = 4
EXPERTS_PER_GROUP = 8
N_EXPERTS = N_GROUPS * EXPERTS_PER_GROUP
EXPERT_FF = 256
ROUTE_LANE0 = N_GROUPS

AUG_Q = FOX_DIM
AUG_K = FOX_DIM + 3
PIECE_STRIDE = 16
ONE_LANE = 3 * PIECE_STRIDE

TOKEN_TILE = 512
PRE_SUB = 512
POST_TILE = 1024
POST_SUB = 256
ATTN_TQ = 1024
ATTN_TK = 1024
SUM_ROWS = 16
EXPERT_BLOCK = 512
EXPERT_SUB = 256
DISPATCH_TILE = 1024
PLE_TILE = 256
ISSUE_UNROLL = 8
VMEM_LIMIT = 56 * 1024 * 1024
SAFE_SCORE_BOUND = 40.0
SKIP_LOG2 = 160.0

_NT = (((1,), (1,)), ((), ()))


def _cparams(n_axes):
    return pltpu.CompilerParams(dimension_semantics=("arbitrary",) * n_axes,
                                vmem_limit_bytes=VMEM_LIMIT)


def _rms(x, g):
    return x * lax.rsqrt(jnp.mean(x * x, axis=-1, keepdims=True) + EPS) * g


def _split3(x):
    hi = x.astype(BF16)
    r1 = x - hi.astype(F32)
    mid = r1.astype(BF16)
    lo = (r1 - mid.astype(F32)).astype(BF16)
    return hi, mid, lo


def _store_row_tiles(ref, v):
    parts = v.shape[1] // LANES
    for s in range(parts):
        ref[pl.ds(s, v.shape[0], stride=parts), :] = v[:, s * LANES:(s + 1) * LANES]


def _load_row_tiles(ref, rows, parts):
    return jnp.concatenate([ref[pl.ds(s, rows, stride=parts), :] for s in range(parts)], axis=1)


def _lane_tile(a, width):
    return jnp.tile(a, (1, width // LANES))


def _rope_kernel(pos_col_ref, pos_row_ref, invf_lane_ref, sign_ref, invf_rep_ref,
                 cos_ref, sin_ref, cost_ref, sint_ref):
    ang = pos_col_ref[...] * invf_lane_ref[...]
    cos_ref[...] = jnp.cos(ang)
    sin_ref[...] = jnp.sin(ang) * sign_ref[...]
    t = pos_row_ref.shape[-1]
    ang_t = _lane_tile(invf_rep_ref[...], t) * pos_row_ref[0]
    cost_ref[...] = jnp.cos(ang_t)
    sint_ref[...] = jnp.sin(ang_t)


def _rope_tables(positions):
    n = positions.size
    half = MLA_ROPE // 2
    inv_freq = ROPE_THETA ** (-np.arange(0, MLA_ROPE, 2, dtype=np.float32) / MLA_ROPE)
    invf = np.zeros((1, LANES), np.float32)
    sign = np.zeros((1, LANES), np.float32)
    invf[0, MLA_NOPE:MLA_NOPE + half] = inv_freq
    invf[0, MLA_NOPE + half:MLA_QK] = inv_freq
    sign[0, MLA_NOPE:MLA_NOPE + half] = -1.0
    sign[0, MLA_NOPE + half:MLA_QK] = 1.0
    invf_rep = np.broadcast_to(inv_freq[:, None], (half, LANES)).astype(np.float32)
    pos = positions.astype(F32)
    t = TOKEN_TILE
    const = lambda i: (0, 0)
    return pl.pallas_call(
        _rope_kernel,
        out_shape=(jax.ShapeDtypeStruct((n, LANES), F32), jax.ShapeDtypeStruct((n, LANES), F32),
                   jax.ShapeDtypeStruct((half, n), F32), jax.ShapeDtypeStruct((half, n), F32)),
        grid=(n // t,),
        in_specs=[pl.BlockSpec((t, 1), lambda i: (i, 0)),
                  pl.BlockSpec((1, 1, t), lambda i: (i, 0, 0)),
                  pl.BlockSpec((1, LANES), const), pl.BlockSpec((1, LANES), const),
                  pl.BlockSpec((half, LANES), const)],
        out_specs=(pl.BlockSpec((t, LANES), lambda i: (i, 0)), pl.BlockSpec((t, LANES), lambda i: (i, 0)),
                   pl.BlockSpec((half, t), lambda i: (0, i)), pl.BlockSpec((half, t), lambda i: (0, i))),
        compiler_params=_cparams(1),
        name="rope_tables",
    )(pos.reshape(n, 1), pos.reshape(n // t, 1, t), jnp.asarray(invf), jnp.asarray(sign),
      jnp.asarray(invf_rep))


def _pre_kernel(x_ref, cos_ref, sin_ref, cost_ref, sint_ref, gattn_ref, wa_ref, wfk_ref, wfqt_ref, wfvt_ref,
                bfl_ref, tri_ref, plk_ref, plq_ref, gql_ref, wuqt_ref, gkvl_ref, wuk_ref, wuvt_ref,
                gq_ref, gk_ref, gfq_ref, gfk_ref,
                qt_out, k_out, vt_out, fqt_out, fk_out, fvt_out, fcum_out, carry_ref, *, tiles_per_seq):
    @pl.when(pl.program_id(0) % tiles_per_seq == 0)
    def _():
        carry_ref[...] = jnp.zeros_like(carry_ref)

    sub = tri_ref.shape[0]
    running = carry_ref[:1, :]
    for sb in range(x_ref.shape[0] // sub):
        running = _pre_rows(slice(sb * sub, (sb + 1) * sub), running, x_ref, cos_ref, sin_ref, cost_ref, sint_ref,
                            gattn_ref, wa_ref, wfk_ref, wfqt_ref, wfvt_ref, bfl_ref, tri_ref, plk_ref, plq_ref,
                            gql_ref, wuqt_ref, gkvl_ref, wuk_ref, wuvt_ref, gq_ref, gk_ref, gfq_ref, gfk_ref,
                            qt_out, k_out, vt_out, fqt_out, fk_out, fvt_out, fcum_out)
    carry_ref[...] = jnp.broadcast_to(running, carry_ref.shape)


def _pre_rows(rs, running, x_ref, cos_ref, sin_ref, cost_ref, sint_ref, gattn_ref, wa_ref, wfk_ref, wfqt_ref,
              wfvt_ref, bfl_ref, tri_ref, plk_ref, plq_ref, gql_ref, wuqt_ref, gkvl_ref, wuk_ref, wuvt_ref,
              gq_ref, gk_ref, gfq_ref, gfk_ref, qt_out, k_out, vt_out, fqt_out, fk_out, fvt_out, fcum_out):
    x = x_ref[rs, :]
    t = x.shape[0]
    h = _rms(x, gattn_ref[...]).astype(BF16)
    pa = jnp.dot(h, wa_ref[...], preferred_element_type=F32)
    half = MLA_ROPE // 2
    lane = lax.broadcasted_iota(jnp.int32, (1, LANES), 1)
    low = lane < FOX_DIM

    def head_lanes(m, hd):
        v = m[:, (hd // 2) * LANES:(hd // 2 + 1) * LANES]
        if hd % 2:
            v = pltpu.roll(v, LANES // 2, 1)
        return jnp.where(low, v, 0.0)

    qn = _rms(pa[:, :Q_LORA], gql_ref[...]).astype(BF16)
    kvn = _rms(pa[:, Q_LORA:Q_LORA + KV_LORA], gkvl_ref[...]).astype(BF16)
    vt_out[0, :, rs] = lax.dot_general(wuvt_ref[...], kvn, _NT, preferred_element_type=F32).astype(BF16)

    kn = jnp.dot(kvn, wuk_ref[...], preferred_element_type=F32)
    kpe = pa[:, Q_LORA + KV_LORA:Q_LORA + KV_LORA + LANES]
    cos_l = cos_ref[rs, :]
    sin_l = sin_ref[rs, :]
    gk = gk_ref[...]
    for hd in range(HEADS):
        v = head_lanes(kn, hd) + kpe
        v = v * lax.rsqrt(jnp.sum(v * v, axis=-1, keepdims=True) * (1.0 / MLA_QK) + EPS) * gk
        swapped = jnp.where(lane < MLA_NOPE + half, pltpu.roll(v, LANES - half, 1), pltpu.roll(v, half, 1))
        k_out[rs, hd * LANES:(hd + 1) * LANES] = (v * cos_l + swapped * sin_l).astype(BF16)

    qt = lax.dot_general(wuqt_ref[...], qn, _NT, preferred_element_type=F32)
    cos_r = cost_ref[:, rs]
    sin_r = sint_ref[:, rs]
    gq = _lane_tile(gq_ref[...], t)[:MLA_QK]
    pad_q = jnp.zeros((LANES - MLA_QK, t), BF16)
    for hd in range(HEADS):
        blk = qt[hd * MLA_QK:(hd + 1) * MLA_QK]
        r = lax.rsqrt(jnp.sum(blk * blk, axis=0, keepdims=True) * (1.0 / MLA_QK) + EPS)
        blk = blk * r * gq
        x1, x2 = blk[MLA_NOPE:MLA_NOPE + half], blk[MLA_NOPE + half:]
        blk = jnp.concatenate([blk[:MLA_NOPE], x1 * cos_r - x2 * sin_r, x2 * cos_r + x1 * sin_r], axis=0)
        qt_out[0, hd * LANES:(hd + 1) * LANES, rs] = jnp.concatenate([blk.astype(BF16), pad_q], axis=0)

    fvt_out[0, :, rs] = lax.dot_general(wfvt_ref[...], h, _NT, preferred_element_type=F32).astype(BF16)

    def by_group(a, b, c):
        return jnp.where(lane < PIECE_STRIDE, a, jnp.where(lane < 2 * PIECE_STRIDE, b, c))

    z = pa[:, Q_LORA + KV_LORA + LANES:] + bfl_ref[...]
    logf = jnp.minimum(z, 0.0) - jnp.log1p(jnp.exp(-jnp.abs(z)))
    parts = jnp.dot(tri_ref[...], by_group(*_split3(logf)), preferred_element_type=F32)
    tot = parts + pltpu.roll(parts, PIECE_STRIDE, 1) + pltpu.roll(parts, 2 * PIECE_STRIDE, 1)
    cum = by_group(pltpu.roll(tot, LANES - 2 * PIECE_STRIDE, 1), pltpu.roll(tot, LANES - PIECE_STRIDE, 1), tot)
    cum = cum + running
    cum2 = cum * LOG2E
    fcum_out[rs, :] = cum2
    packed = by_group(*_split3(cum2))
    packed = jnp.where(lane == ONE_LANE, jnp.ones_like(packed), packed)
    aug_k = jnp.dot(packed, plk_ref[...], preferred_element_type=F32)
    aug_q = lax.dot_general(plq_ref[...], packed, _NT, preferred_element_type=F32)

    fk = jnp.dot(h, wfk_ref[...], preferred_element_type=F32)
    gfk = gfk_ref[...]
    for hd in range(HEADS):
        sl = slice(hd * LANES, (hd + 1) * LANES)
        v = head_lanes(fk, hd)
        v = v * lax.rsqrt(jnp.sum(v * v, axis=-1, keepdims=True) * (1.0 / FOX_DIM) + EPS) * gfk
        fk_out[rs, sl] = (v + aug_k[:, sl]).astype(BF16)

    fqt = lax.dot_general(wfqt_ref[...], h, _NT, preferred_element_type=F32)
    gfq = _lane_tile(gfq_ref[...], t)[:FOX_DIM]
    for hd in range(HEADS):
        blk = fqt[hd * FOX_DIM:(hd + 1) * FOX_DIM]
        r = lax.rsqrt(jnp.sum(blk * blk, axis=0, keepdims=True) * (1.0 / FOX_DIM) + EPS)
        aug = aug_q[hd * LANES + FOX_DIM:(hd + 1) * LANES]
        fqt_out[0, hd * LANES:(hd + 1) * LANES, rs] = jnp.concatenate([blk * r * gfq, aug], axis=0).astype(BF16)
    return cum[t - 1:t, :]


def _pre_attention(x2d, tables, w, seq):
    n, d = x2d.shape
    t = TOKEN_TILE
    tiles_per_seq = seq // t
    batch = n // seq
    cos_l, sin_l, cos_t, sin_t = tables
    half = MLA_ROPE // 2
    row = lambda i: (i, 0)
    seq_t = lambda i: (i // tiles_per_seq, 0, i % tiles_per_seq)

    def full(a):
        return pl.BlockSpec(a.shape, lambda i, nd=a.ndim: (0,) * nd)

    weights = [w["g_attn"], w["w_a"], w["w_fk"], w["w_fq_t"], w["w_fv_t"], w["b_fl"], w["tri_incl"],
               w["place_k"], w["place_q"], w["g_q_lora"], w["w_uq_t"], w["g_kv_lora"], w["w_uk"], w["w_uv_t"],
               w["g_mla_q"], w["g_mla_k"], w["g_fox_q"], w["g_fox_k"]]
    in_specs = [pl.BlockSpec((t, d), row), pl.BlockSpec((t, LANES), row), pl.BlockSpec((t, LANES), row),
                pl.BlockSpec((half, t), lambda i: (0, i)), pl.BlockSpec((half, t), lambda i: (0, i))]
    in_specs += [full(a) for a in weights]
    wq = HEADS * LANES
    wv = HEADS * HEAD_V
    out_shape = (jax.ShapeDtypeStruct((batch, wq, seq), BF16), jax.ShapeDtypeStruct((n, wq), BF16),
                 jax.ShapeDtypeStruct((batch, wv, seq), BF16), jax.ShapeDtypeStruct((batch, wq, seq), BF16),
                 jax.ShapeDtypeStruct((n, wq), BF16), jax.ShapeDtypeStruct((batch, wv, seq), BF16),
                 jax.ShapeDtypeStruct((n, LANES), F32))
    out_specs = (pl.BlockSpec((1, wq, t), seq_t), pl.BlockSpec((t, wq), row), pl.BlockSpec((1, wv, t), seq_t),
                 pl.BlockSpec((1, wq, t), seq_t), pl.BlockSpec((t, wq), row), pl.BlockSpec((1, wv, t), seq_t),
                 pl.BlockSpec((t, LANES), row))
    return pl.pallas_call(
        functools.partial(_pre_kernel, tiles_per_seq=tiles_per_seq),
        out_shape=out_shape,
        grid=(n // t,),
        in_specs=in_specs,
        out_specs=out_specs,
        scratch_shapes=[pltpu.VMEM((8, LANES), F32)],
        compiler_params=_cparams(1),
        name="pre_attention",
    )(x2d, cos_l, sin_l, cos_t, sin_t, *weights)


def _allowed(k0, q0, tk, tq, chunked):
    key = k0 + lax.broadcasted_iota(jnp.int32, (tk, tq), 0)
    qry = q0 + lax.broadcasted_iota(jnp.int32, (tk, tq), 1)
    if chunked:
        return (key // CHUNK) <= (qry // CHUNK)
    return key <= qry


def _rowmax_kernel(qt_ref, k_ref, m_ref, *, chunked, tq, tk):
    qi = pl.program_id(2)
    n_diag = tq // tk

    def block(kb, carry, masked):
        k0 = pl.multiple_of(kb * tk, tk)
        kblk = k_ref[0, pl.ds(k0, tk), :]
        out = []
        for i in range(2):
            s = jnp.dot(kblk[:, i * LANES:(i + 1) * LANES], qt_ref[0, i * LANES:(i + 1) * LANES, :],
                        preferred_element_type=F32)
            if masked:
                s = jnp.where(_allowed(k0, qi * tq, tk, tq, chunked), s, NEG_INF)
            out.append(jnp.maximum(carry[i], jnp.max(s, axis=0, keepdims=True)))
        return tuple(out)

    init = tuple(jnp.full((1, tq), NEG_INF, F32) for _ in range(2))
    carry = lax.fori_loop(0, qi * n_diag, lambda kb, c: block(kb, c, False), init)
    for j in range(n_diag):
        carry = block(qi * n_diag + j, carry, True)
    m_ref[0, 0] = jnp.concatenate(carry, axis=0)


def _row_max(qt, k, *, chunked):
    batch, seq, wk = k.shape
    tq, tk = ATTN_TQ, ATTN_TK
    pairs = wk // (2 * LANES)
    out = pl.pallas_call(
        functools.partial(_rowmax_kernel, chunked=chunked, tq=tq, tk=tk),
        out_shape=jax.ShapeDtypeStruct((batch, pairs, 2, seq), F32),
        grid=(batch, pairs, seq // tq),
        in_specs=[pl.BlockSpec((1, 2 * LANES, tq), lambda b, h, i: (b, h, i)),
                  pl.BlockSpec((1, seq, 2 * LANES), lambda b, h, i: (b, 0, h))],
        out_specs=pl.BlockSpec((1, 1, 2, tq), lambda b, h, i: (b, h, 0, i)),
        compiler_params=_cparams(3),
        name="attn_rowmax",
    )(qt, k)
    return out


def _attn_kernel(start_ref, qt_ref, k_ref, vt_ref, shift_ref, mask_ref, o_ref, acc_ref, *, tq, tk):
    pairs = pl.num_programs(1)
    n_q = pl.num_programs(2)
    qi = pl.program_id(2)
    acc_ref[...] = jnp.zeros_like(acc_ref)

    def block(k0, nk, q_lo, masked):
        qs = slice(q_lo, tq)
        kblk = k_ref[0, pl.ds(k0, nk), :]
        vt = vt_ref[0, :, pl.ds(k0, nk)]
        ones = jnp.ones((SUM_ROWS, nk), BF16)
        for i in range(2):
            s = jnp.dot(kblk[:, i * LANES:(i + 1) * LANES], qt_ref[0, i * LANES:(i + 1) * LANES, qs],
                        preferred_element_type=F32)
            s = s - shift_ref[0, 0, i:i + 1, qs]
            if masked:
                s = s + mask_ref[q_lo:q_lo + nk, qs]
            p = jnp.exp2(s).astype(BF16)
            lhs = jnp.concatenate([vt[i * HEAD_V:(i + 1) * HEAD_V], ones], axis=0)
            acc_ref[i, :, qs] += jnp.dot(lhs, p, preferred_element_type=F32)

    def body(kb, c):
        block(pl.multiple_of(kb * tk, tk), tk, 0, False)
        return c

    first = start_ref[(pl.program_id(0) * pairs + pl.program_id(1)) * n_q + qi]
    lax.fori_loop(first, qi, body, 0)
    half = tk // 2
    q0 = pl.multiple_of(qi * tq, tq)
    block(q0, half, 0, True)
    block(pl.multiple_of(q0 + half, half), half, half, True)
    o_t = jnp.concatenate([acc_ref[i, :HEAD_V, :] / acc_ref[i, HEAD_V:HEAD_V + 1, :] for i in range(2)], axis=0)
    o_ref[0] = o_t.T.astype(o_ref.dtype)


def _diag_mask(tq, chunked):
    key = np.arange(tq)[:, None]
    qry = np.arange(tq)[None, :]
    ok = (key // CHUNK) <= (qry // CHUNK) if chunked else key <= qry
    return jnp.asarray(np.where(ok, 0.0, NEG_INF).astype(np.float32))


def _first_block(fcum, batch, seq, tq, tk):
    f = fcum.reshape(batch, seq, LANES)[:, :, :HEADS]
    f_q0 = f[:, ::tq, :]
    f_kl = f[:, tk - 1::tk, :]
    dead = (f_q0[:, :, None, :] - f_kl[:, None, :, :]) < -SKIP_LOG2
    dead = dead.reshape(batch, seq // tq, seq // tk, HEADS // 2, 2).all(axis=-1)
    lead = jnp.cumprod(dead.astype(jnp.int32), axis=2).sum(axis=2)
    limit = (jnp.arange(seq // tq, dtype=jnp.int32) * (tq // tk))[None, :, None]
    return jnp.minimum(lead, limit).transpose(0, 2, 1).reshape(-1).astype(jnp.int32)


def _attention(qt, k, vt, bound, fcum, *, chunked):
    batch, seq, wk = k.shape
    tq, tk = ATTN_TQ, ATTN_TK
    pairs = wk // (2 * LANES)
    n_q = seq // tq
    fast = bound <= SAFE_SCORE_BOUND * LOG2E
    shift = lax.cond(fast, lambda: jnp.full((batch, pairs, 2, seq), bound, F32),
                     lambda: _row_max(qt, k, chunked=chunked))
    first = jnp.zeros((batch * pairs * n_q,), jnp.int32)
    if fcum is not None:
        first = jnp.where(fast, _first_block(fcum, batch, seq, tq, tk), first)
    return pl.pallas_call(
        functools.partial(_attn_kernel, tq=tq, tk=tk),
        out_shape=jax.ShapeDtypeStruct((batch, seq, pairs * 2 * HEAD_V), BF16),
        grid_spec=pltpu.PrefetchScalarGridSpec(
            num_scalar_prefetch=1,
            grid=(batch, pairs, n_q),
            in_specs=[pl.BlockSpec((1, 2 * LANES, tq), lambda b, h, i, st: (b, h, i)),
                      pl.BlockSpec((1, seq, 2 * LANES), lambda b, h, i, st: (b, 0, h)),
                      pl.BlockSpec((1, 2 * HEAD_V, seq), lambda b, h, i, st: (b, h, 0)),
                      pl.BlockSpec((1, 1, 2, tq), lambda b, h, i, st: (b, h, 0, i)),
                      pl.BlockSpec((tq, tq), lambda b, h, i, st: (0, 0))],
            out_specs=pl.BlockSpec((1, tq, 2 * HEAD_V), lambda b, h, i, st: (b, i, h)),
            scratch_shapes=[pltpu.VMEM((2, HEAD_V + SUM_ROWS, tq), F32)]),
        compiler_params=_cparams(3),
        name="attn_chunk_causal" if chunked else "attn_frame_causal",
    )(first, qt, k, vt, shift, _diag_mask(tq, chunked))


def _post_kernel(x_ref, om_ref, of_ref, gom_ref, gof_ref, wom_ref, wof_ref, gffn_ref, wrh_ref, wrl_ref,
                 tri_ref, x1_out, h2_out, route_out, count_out, carry_ref):
    step = pl.program_id(0)
    sub = tri_ref.shape[0]
    parts = x_ref.shape[1] // LANES

    @pl.when(step == 0)
    def _():
        carry_ref[...] = jnp.zeros_like(carry_ref)

    running = carry_ref[:1, :]
    for sb in range(x_ref.shape[0] // sub):
        running = _post_rows(slice(sb * sub, (sb + 1) * sub), pl.ds(sb * sub * parts, sub * parts), running,
                             x_ref, om_ref, of_ref, gom_ref, gof_ref, wom_ref, wof_ref, gffn_ref, wrh_ref,
                             wrl_ref, tri_ref, x1_out, h2_out, route_out)
    carry_ref[...] = jnp.broadcast_to(running, carry_ref.shape)
    count_out[...] = jnp.broadcast_to(running, count_out.shape)


def _post_rows(rs, tile_rows, running, x_ref, om_ref, of_ref, gom_ref, gof_ref, wom_ref, wof_ref, gffn_ref,
               wrh_ref, wrl_ref, tri_ref, x1_out, h2_out, route_out):
    ym = _rms(om_ref[rs, :].astype(F32), gom_ref[...]).astype(BF16)
    yf = _rms(of_ref[rs, :].astype(F32), gof_ref[...]).astype(BF16)
    x1 = (x_ref[rs, :] + jnp.dot(ym, wom_ref[...], preferred_element_type=F32)
          + jnp.dot(yf, wof_ref[...], preferred_element_type=F32))
    x1_out[rs, :] = x1
    h2 = _rms(x1, gffn_ref[...])
    _store_row_tiles(h2_out.at[tile_rows], h2)

    h_hi = h2.astype(BF16)
    h_lo = (h2 - h_hi.astype(F32)).astype(BF16)
    w_hi = wrh_ref[...]
    logits = (jnp.dot(h_hi, w_hi, preferred_element_type=F32)
              + jnp.dot(h_lo, w_hi, preferred_element_type=F32)
              + jnp.dot(h_hi, wrl_ref[...], preferred_element_type=F32))

    lane = lax.broadcasted_iota(jnp.int32, (1, LANES), 1).astype(F32)
    big = float(LANES)

    def first_argmax(v):
        mx = jnp.max(v, axis=-1, keepdims=True)
        idx = jnp.min(jnp.where(v == mx, lane, big), axis=-1, keepdims=True)
        return mx, idx

    lg = jnp.where(lane < N_GROUPS, logits, NEG_INF)
    mg, g_idx = first_argmax(lg)
    p_g = 1.0 / jnp.sum(jnp.exp(lg - mg), axis=-1, keepdims=True)
    e_lo = ROUTE_LANE0 + EXPERTS_PER_GROUP * g_idx
    le = jnp.where((lane >= e_lo) & (lane < e_lo + EXPERTS_PER_GROUP), logits, NEG_INF)
    m1, i1 = first_argmax(le)
    m2, i2 = first_argmax(jnp.where(lane == i1, NEG_INF, le))
    e2 = jnp.exp(m2 - m1)
    gate1 = p_g / (1.0 + e2)
    gate2 = p_g * e2 / (1.0 + e2)

    oh1 = lane == i1
    oh2 = lane == i2
    oh = (jnp.where(oh1, 1.0, 0.0) + jnp.where(oh2, 1.0, 0.0))
    before = jnp.dot(tri_ref[...], oh.astype(BF16), preferred_element_type=F32)
    base = before + running
    rank1 = jnp.sum(jnp.where(oh1, base, 0.0), axis=-1, keepdims=True)
    rank2 = jnp.sum(jnp.where(oh2, base, 0.0), axis=-1, keepdims=True)

    route = jnp.where(lane == 0, i1 - ROUTE_LANE0, 0.0)
    route = jnp.where(lane == 1, i2 - ROUTE_LANE0, route)
    route = jnp.where(lane == 2, rank1, route)
    route = jnp.where(lane == 3, rank2, route)
    route = jnp.where(lane == 4, gate1, route)
    route = jnp.where(lane == 5, gate2, route)
    route_out[rs, :] = route
    return running + jnp.sum(oh, axis=0, keepdims=True)


def _post_attention(x2d, o_mla, o_fox, w):
    n, d = x2d.shape
    t = POST_TILE
    parts = d // LANES
    const = lambda i: (0, 0)
    row = lambda i: (i, 0)
    full = lambda a: pl.BlockSpec(a.shape, const)
    ins = [x2d, o_mla, o_fox, w["g_out_mla"], w["g_out_fox"], w["w_out_mla"], w["w_out_fox"],
           w["g_ffn"], w["w_router_hi"], w["w_router_lo"], w["tri_strict"]]
    in_specs = [pl.BlockSpec((t, d), row), pl.BlockSpec((t, o_mla.shape[1]), row),
                pl.BlockSpec((t, o_fox.shape[1]), row)] + [full(a) for a in ins[3:]]
    return pl.pallas_call(
        _post_kernel,
        out_shape=(jax.ShapeDtypeStruct((n, d), F32), jax.ShapeDtypeStruct((n * parts, LANES), F32),
                   jax.ShapeDtypeStruct((n, LANES), F32), jax.ShapeDtypeStruct((8, LANES), F32)),
        grid=(n // t,),
        in_specs=in_specs,
        out_specs=(pl.BlockSpec((t, d), row), pl.BlockSpec((t * parts, LANES), row),
                   pl.BlockSpec((t, LANES), row), pl.BlockSpec((8, LANES), const)),
        scratch_shapes=[pltpu.VMEM((8, LANES), F32)],
        compiler_params=_cparams(1),
        name="post_attention",
    )(*ins)


def _dispatch_kernel(pad_end_ref, padded_ref, n_used_ref, dest_ref, h_ref, xs_hbm, zero_ref, sem, zsem, *,
                     tile, parts):
    zrows = zero_ref.shape[0]
    n_blk = xs_hbm.shape[0] // zrows
    rblk = zrows // parts

    def zero_copy(blk):
        return pltpu.make_async_copy(zero_ref, xs_hbm.at[pl.ds(pl.multiple_of(blk * zrows, zrows), zrows)], zsem)

    @pl.when(pl.program_id(0) == 0)
    def _():
        zero_ref[...] = jnp.zeros_like(zero_ref)
        for wait in (False, True):
            for e in range(N_EXPERTS):
                @pl.when(padded_ref[e] > 0)
                def _():
                    cp = zero_copy(pad_end_ref[e] // rblk - 1)
                    cp.wait() if wait else cp.start()

            def tail(blk, c):
                cp = zero_copy(blk)
                cp.wait() if wait else cp.start()
                return c

            lax.fori_loop(n_used_ref[0], n_blk, tail, 0)

    def issue(jo, c):
        for ji in range(ISSUE_UNROLL):
            j = jo * ISSUE_UNROLL + ji
            src = h_ref.at[pl.ds(pl.multiple_of(j * parts, parts), parts)]
            for k in range(2):
                d = pl.multiple_of(dest_ref[0, 0, 2 * j + k], parts)
                pltpu.make_async_copy(src, xs_hbm.at[pl.ds(d, parts)], sem).start(priority=k)
        return c

    lax.fori_loop(0, tile // ISSUE_UNROLL, issue, 0)
    for _ in range(2):
        pltpu.make_async_copy(h_ref, xs_hbm.at[pl.ds(0, tile * parts)], sem).wait()


def _dispatch(h2t, plan, parts):
    n = h2t.shape[0] // parts
    tile = DISPATCH_TILE
    dest3 = (plan["dest"] * parts).reshape(n // tile, 1, 2 * tile)
    return pl.pallas_call(
        functools.partial(_dispatch_kernel, tile=tile, parts=parts),
        out_shape=jax.ShapeDtypeStruct((plan["n_rows"] * parts, LANES), h2t.dtype),
        grid_spec=pltpu.PrefetchScalarGridSpec(
            num_scalar_prefetch=3,
            grid=(n // tile,),
            in_specs=[pl.BlockSpec((1, 1, 2 * tile), lambda i, *_: (i, 0, 0), memory_space=pltpu.SMEM),
                      pl.BlockSpec((tile * parts, LANES), lambda i, *_: (i, 0))],
            out_specs=pl.BlockSpec(memory_space=pl.ANY),
            scratch_shapes=[pltpu.VMEM((EXPERT_BLOCK * parts, LANES), h2t.dtype), pltpu.SemaphoreType.DMA,
                            pltpu.SemaphoreType.DMA]),
        compiler_params=_cparams(1),
        name="moe_dispatch",
    )(plan["pad_end"], plan["padded"], plan["n_used"], dest3, h2t)


def _expert_kernel(blk_e_ref, n_used_ref, x_ref, wg_ref, wu_ref, wdn_ref, y_ref, wgu_bf, wd_bf, *, parts):
    i = pl.program_id(0)
    used = i < n_used_ref[0]
    rows = x_ref.shape[0] // parts
    fresh = jnp.logical_or(i == 0, blk_e_ref[i] != blk_e_ref[jnp.maximum(i - 1, 0)])

    @pl.when(jnp.logical_and(used, fresh))
    def _():
        wgu_bf[:, :EXPERT_FF] = wg_ref[0, 0].astype(BF16)
        wgu_bf[:, EXPERT_FF:] = wu_ref[0, 0].astype(BF16)
        wd_bf[...] = wdn_ref[0, 0].astype(BF16)

    @pl.when(used)
    def _():
        sub = EXPERT_SUB
        for sb in range(rows // sub):
            tiles = pl.ds(sb * sub * parts, sub * parts)
            xb = _load_row_tiles(x_ref.at[tiles], sub, parts).astype(BF16)
            gu = jnp.dot(xb, wgu_bf[...], preferred_element_type=F32)
            g = gu[:, :EXPERT_FF]
            act = (g / (1.0 + jnp.exp(-g))) * gu[:, EXPERT_FF:]
            _store_row_tiles(y_ref.at[tiles], jnp.dot(act.astype(BF16), wd_bf[...], preferred_element_type=F32))

    @pl.when(jnp.logical_not(used))
    def _():
        y_ref[...] = jnp.zeros_like(y_ref)


def _experts(xs, plan, layer, w_gate, w_up, w_down, parts):
    r = EXPERT_BLOCK
    d = parts * LANES
    rows = lambda i, be, nu: (jnp.minimum(i, nu[0] - 1), 0)
    wsel = lambda i, be, nu: (layer, be[i], 0, 0)
    return pl.pallas_call(
        functools.partial(_expert_kernel, parts=parts),
        out_shape=jax.ShapeDtypeStruct(xs.shape, F32),
        grid_spec=pltpu.PrefetchScalarGridSpec(
            num_scalar_prefetch=2,
            grid=(xs.shape[0] // (r * parts),),
            in_specs=[pl.BlockSpec((r * parts, LANES), rows),
                      pl.BlockSpec((1, 1, d, EXPERT_FF), wsel),
                      pl.BlockSpec((1, 1, d, EXPERT_FF), wsel),
                      pl.BlockSpec((1, 1, EXPERT_FF, d), wsel)],
            out_specs=pl.BlockSpec((r * parts, LANES), lambda i, be, nu: (i, 0)),
            scratch_shapes=[pltpu.VMEM((d, 2 * EXPERT_FF), BF16), pltpu.VMEM((EXPERT_FF, d), BF16)]),
        compiler_params=_cparams(1),
        name="moe_experts",
    )(plan["blk_e"], plan["n_used"], xs, w_gate, w_up, w_down)


def _ple_kernel(dest_ref, dest_next_ref, x1_ref, route_ref, p_ref, gnorm_ref, wgate_ref, wproj_ref, gout_ref,
                ys_hbm, x_out, rows_a, rows_b, sem, *, tile, parts):
    step = pl.program_id(0)
    n_steps = pl.num_programs(0)

    def row_copy(d, buf, slot, k, j):
        return pltpu.make_async_copy(ys_hbm.at[pl.ds(pl.multiple_of(d, parts), parts)],
                                     buf.at[k, pl.ds(j * parts, parts)], sem.at[slot])

    def wait_rows(buf, slot):
        for k in range(2):
            pltpu.make_async_copy(ys_hbm.at[pl.ds(0, tile * parts)], buf.at[k], sem.at[slot]).wait()

    @pl.when(step == 0)
    def _():
        def issue(jo, c):
            for ji in range(ISSUE_UNROLL):
                j = jo * ISSUE_UNROLL + ji
                for k in range(2):
                    pltpu.make_async_copy(
                        ys_hbm.at[pl.ds(pl.multiple_of(dest_ref[0, 0, 2 * j + k], parts), parts)],
                        rows_a.at[k, pl.ds(pl.multiple_of(j * parts, parts), parts)], sem.at[0]).start(priority=k)
            return c
        lax.fori_loop(0, tile // ISSUE_UNROLL, issue, 0)

    def body(cur, nxt, slot):
        wait_rows(cur, slot)
        for j in range(tile):
            for k in range(2):
                row_copy(dest_next_ref[0, 0, 2 * j + k], nxt, 1 - slot, k, j).start(priority=k)
        route = route_ref[...]
        y1 = _load_row_tiles(cur.at[0], tile, parts)
        y2 = _load_row_tiles(cur.at[1], tile, parts)
        x2 = x1_ref[...] + route[:, 4:5] * y1 + route[:, 5:6] * y2
        ple = _rms(jnp.dot(p_ref[0].astype(BF16), wproj_ref[...], preferred_element_type=F32), gout_ref[...])
        z = jnp.dot(_rms(x2, gnorm_ref[...]).astype(BF16), wgate_ref[...], preferred_element_type=F32)
        x_out[...] = x2 + ple / (1.0 + jnp.exp(-z))

        @pl.when(step == n_steps - 1)
        def _():
            wait_rows(nxt, 1 - slot)

    @pl.when(step % 2 == 0)
    def _():
        body(rows_a, rows_b, 0)

    @pl.when(step % 2 == 1)
    def _():
        body(rows_b, rows_a, 1)


def _combine_ple(x1, route, dest, ys, p3d, layer, w):
    n, d = x1.shape
    t = PLE_TILE
    parts = d // LANES
    steps = n // t
    const = lambda i: (0, 0)
    row = lambda i: (i, 0)
    full = lambda a: pl.BlockSpec(a.shape, const)
    dest3 = (dest * parts).reshape(steps, 1, 2 * t)
    return pl.pallas_call(
        functools.partial(_ple_kernel, tile=t, parts=parts),
        out_shape=jax.ShapeDtypeStruct((n, d), F32),
        grid=(steps,),
        in_specs=[pl.BlockSpec((1, 1, 2 * t), lambda i: (i, 0, 0), memory_space=pltpu.SMEM),
                  pl.BlockSpec((1, 1, 2 * t), lambda i: (jnp.minimum(i + 1, steps - 1), 0, 0),
                               memory_space=pltpu.SMEM),
                  pl.BlockSpec((t, d), row), pl.BlockSpec((t, LANES), row),
                  pl.BlockSpec((1, t, p3d.shape[2]), lambda i: (layer, i, 0)),
                  full(w["g_ple_norm"]), full(w["w_ple_gate"]), full(w["w_ple_proj"]), full(w["g_ple_out"]),
                  pl.BlockSpec(memory_space=pl.ANY)],
        out_specs=pl.BlockSpec((t, d), row),
        scratch_shapes=[pltpu.VMEM((2, t * parts, LANES), F32), pltpu.VMEM((2, t * parts, LANES), F32),
                        pltpu.SemaphoreType.DMA((2,))],
        compiler_params=_cparams(1),
        name="moe_combine_ple",
    )(dest3, dest3, x1, route, p3d, w["g_ple_norm"], w["w_ple_gate"], w["w_ple_proj"], w["g_ple_out"], ys)


def _row(v):
    return v.reshape(1, -1).astype(F32)


def _col_rep(v):
    return jnp.broadcast_to(v.astype(F32)[:, None], (v.shape[0], LANES))


def _pad_heads(wmat, real):
    k = wmat.shape[0]
    return jnp.pad(wmat.reshape(k, HEADS, real), ((0, 0), (0, 0), (0, LANES - real))).reshape(k, HEADS * LANES)


def _placement():
    place_k = np.zeros((LANES, HEADS * LANES), np.float32)
    place_q = np.zeros((HEADS * LANES, LANES), np.float32)
    for hd in range(HEADS):
        base = hd * LANES
        for j in range(3):
            src = j * PIECE_STRIDE + hd
            place_k[src, base + AUG_K + j] = -1.0
            place_q[base + AUG_Q + j, src] = 1.0
            place_k[ONE_LANE, base + AUG_Q + j] = 1.0
            place_q[base + AUG_K + j, ONE_LANE] = 1.0
    return jnp.asarray(place_k, BF16), jnp.asarray(place_q, BF16)


def _layer_weights(i, g_attn_norm, w_in, g_q_lora, w_uq, g_kv_lora, w_ukv, g_mla_q, g_mla_k, g_fox_q,
                   g_fox_k, b_fox_f, g_out_mla, g_out_fox, w_out, g_ffn_norm, w_router_group,
                   w_router_expert, w_exp_gate, w_exp_up, w_exp_down, g_ple_norm, w_ple_gate, w_ple_proj,
                   g_ple_out):
    d = w_in.shape[1]
    wf = HEADS * FOX_DIM
    c_kv = Q_LORA
    c_pe = c_kv + KV_LORA
    c_fq = c_pe + MLA_ROPE
    c_fk = c_fq + wf
    c_fv = c_fk + wf
    c_fl = c_fv + wf
    win = w_in[i]
    pad_pe = jnp.zeros((d, LANES), F32).at[:, MLA_NOPE:MLA_QK].set(win[:, c_pe:c_fq])
    group = lambda v: jnp.concatenate(
        [jnp.pad(v, [(0, 0)] * (v.ndim - 1) + [(0, PIECE_STRIDE - HEADS)])] * 3
        + [jnp.zeros(v.shape[:-1] + (LANES - 3 * PIECE_STRIDE,), v.dtype)], axis=-1)
    pad_fl = group(win[:, c_fl:])
    w = {}
    w["g_attn"] = _row(g_attn_norm[i])
    w["w_a"] = jnp.concatenate([win[:, :c_pe], pad_pe, pad_fl], axis=1).astype(BF16)
    w["w_fk"] = win[:, c_fk:c_fv].astype(BF16)
    w["w_fq_t"] = win[:, c_fq:c_fk].T.astype(BF16)
    w["w_fv_t"] = win[:, c_fv:c_fl].T.astype(BF16)
    w["b_fl"] = group(b_fox_f[i].astype(F32)[None, :])
    w["place_k"], w["place_q"] = _placement()
    w["g_q_lora"] = _row(g_q_lora[i])
    w["w_uq_t"] = w_uq[i].T.astype(BF16)
    w["g_kv_lora"] = _row(g_kv_lora[i])
    ukv = w_ukv[i].reshape(KV_LORA, HEADS, MLA_NOPE + HEAD_V)
    w["w_uk"] = ukv[:, :, :MLA_NOPE].reshape(KV_LORA, -1).astype(BF16)
    w["w_uv_t"] = ukv[:, :, MLA_NOPE:].reshape(KV_LORA, -1).T.astype(BF16)
    w["g_mla_q"] = _col_rep(jnp.pad(g_mla_q[i], (0, LANES - MLA_QK))) * (MLA_QK ** -0.5 * LOG2E)
    w["g_mla_k"] = _row(jnp.pad(g_mla_k[i], (0, LANES - MLA_QK)))
    w["g_fox_q"] = _col_rep(jnp.pad(g_fox_q[i], (0, LANES - FOX_DIM))) * (FOX_DIM ** -0.5 * LOG2E)
    w["g_fox_k"] = _row(jnp.pad(g_fox_k[i], (0, LANES - FOX_DIM)))
    bound = lambda gq, gk, dim: (1.02 * LOG2E * dim ** 0.5) * jnp.max(jnp.abs(gq)) * jnp.max(jnp.abs(gk))
    w["bound_mla"] = bound(g_mla_q[i], g_mla_k[i], MLA_QK).astype(F32)
    w["bound_fox"] = bound(g_fox_q[i], g_fox_k[i], FOX_DIM).astype(F32)
    w["g_out_mla"] = _row(g_out_mla[i])
    w["g_out_fox"] = _row(g_out_fox[i])
    wm = HEADS * HEAD_V
    w["w_out_mla"] = w_out[i, :wm].astype(BF16)
    w["w_out_fox"] = w_out[i, wm:].astype(BF16)
    w["g_ffn"] = _row(g_ffn_norm[i])
    wr = jnp.zeros((d, LANES), F32)
    wr = wr.at[:, :N_GROUPS].set(w_router_group[i]).at[:, ROUTE_LANE0:ROUTE_LANE0 + N_EXPERTS].set(
        w_router_expert[i])
    w["w_router_hi"] = wr.astype(BF16)
    w["w_router_lo"] = (wr - w["w_router_hi"].astype(F32)).astype(BF16)
    w["g_ple_norm"] = _row(g_ple_norm[i])
    w["w_ple_gate"] = w_ple_gate[i].astype(BF16)
    w["w_ple_proj"] = w_ple_proj[i].astype(BF16)
    w["g_ple_out"] = _row(g_ple_out[i])
    def tri(t, strict):
        r = lax.broadcasted_iota(jnp.int32, (t, t), 0)
        c = lax.broadcasted_iota(jnp.int32, (t, t), 1)
        return ((c < r) if strict else (c <= r)).astype(BF16)

    w["tri_incl"] = tri(PRE_SUB, False)
    w["tri_strict"] = tri(POST_SUB, True)
    return w


def _route_plan(route, counts, n):
    r = EXPERT_BLOCK
    cnt = counts[0, ROUTE_LANE0:ROUTE_LANE0 + N_EXPERTS].astype(jnp.int32)
    padded = ((cnt + r - 1) // r) * r
    pad_end = jnp.cumsum(padded)
    pad_start = pad_end - padded
    e = route[:, 0:2].astype(jnp.int32)
    rank = route[:, 2:4].astype(jnp.int32)
    onehot = e[:, :, None] == jnp.arange(N_EXPERTS, dtype=jnp.int32)[None, None, :]
    dest = jnp.sum(jnp.where(onehot, pad_start[None, None, :], 0), axis=-1) + rank
    n_rows = 2 * n + N_EXPERTS * r
    blk_start = jnp.arange(n_rows // r, dtype=jnp.int32) * r
    blk_e = jnp.sum((blk_start[:, None] >= pad_end[None, :]).astype(jnp.int32), axis=1)
    blk_e = jnp.minimum(blk_e, N_EXPERTS - 1)
    n_used = (pad_end[-1:] // r).astype(jnp.int32)
    return {"dest": dest.reshape(-1), "blk_e": blk_e, "n_used": n_used, "n_rows": n_rows,
            "pad_end": pad_end.astype(jnp.int32), "padded": padded.astype(jnp.int32)}


def kernel(x, p, positions, g_attn_norm, w_in, g_q_lora, w_uq, g_kv_lora, w_ukv, g_mla_q, g_mla_k, g_fox_q,
           g_fox_k, b_fox_f, g_out_mla, g_out_fox, w_out, g_ffn_norm, w_router_group, w_router_expert,
           w_exp_gate, w_exp_up, w_exp_down, g_ple_norm, w_ple_gate, w_ple_proj, g_ple_out):
    batch, seq, d = x.shape
    n = batch * seq
    depth = w_in.shape[0]
    params = (g_attn_norm, w_in, g_q_lora, w_uq, g_kv_lora, w_ukv, g_mla_q, g_mla_k, g_fox_q, g_fox_k,
              b_fox_f, g_out_mla, g_out_fox, w_out, g_ffn_norm, w_router_group, w_router_expert,
              w_exp_gate, w_exp_up, w_exp_down, g_ple_norm, w_ple_gate, w_ple_proj, g_ple_out)
    tables = _rope_tables(positions)
    xc = x.reshape(n, d)
    for i in range(depth):
        w = _layer_weights(i, *params)
        qt, k, vt, fqt, fk, fvt, fcum = _pre_attention(xc, tables, w, seq)
        b3 = lambda a: a.reshape(batch, seq, a.shape[-1])
        o_mla = _attention(qt, b3(k), vt, w["bound_mla"], None, chunked=True).reshape(n, -1)
        o_fox = _attention(fqt, b3(fk), fvt, w["bound_fox"], fcum, chunked=False).reshape(n, -1)
        x1, h2, route, counts = _post_attention(xc, o_mla, o_fox, w)
        plan = _route_plan(route, counts, n)
        parts = d // LANES
        xs = _dispatch(h2, plan, parts)
        ys = _experts(xs, plan, i, w_exp_gate, w_exp_up, w_exp_down, parts)
        xc = _combine_ple(x1, route, plan["dest"], ys, p.reshape(depth, n, -1), i, w)
    return xc.reshape(batch, seq, d)
```

```python
import functools
import math

import numpy as np
import jax
import jax.numpy as jnp
from jax import lax
from jax.experimental import pallas as pl
from jax.experimental.pallas import tpu as pltpu

F32 = jnp.float32
BF16 = jnp.bfloat16

EPS = 1e-6
NEG_INF = -1e30
ROPE_THETA = 10000.0
LOG2E = math.log2(math.e)

LANES = 128
CHUNK = 64
HEADS = 8
MLA_NOPE = 64
MLA_ROPE = 32
MLA_QK = MLA_NOPE + MLA_ROPE
HEAD_V = 64
FOX_DIM = 64
Q_LORA = 256
KV_LORA = 128
N_GROUPS = 4
EXPERTS_PER_GROUP = 8
N_EXPERTS = N_GROUPS * EXPERTS_PER_GROUP
EXPERT_FF = 256
ROUTE_LANE0 = N_GROUPS

AUG_Q = FOX_DIM
AUG_K = FOX_DIM + 3
PIECE_STRIDE = 16
ONE_LANE = 3 * PIECE_STRIDE

TOKEN_TILE = 512
PRE_SUB = 512
POST_TILE = 1024
POST_SUB = 256
ATTN_TQ = 1024
ATTN_TK = 1024
SUM_ROWS = 16
EXPERT_BLOCK = 512
EXPERT_SUB = 256
DISPATCH_TILE = 1024
PLE_TILE = 256
ISSUE_UNROLL = 8
VMEM_LIMIT = 56 * 1024 * 1024
SAFE_SCORE_BOUND = 40.0
SKIP_LOG2 = 160.0

_NT = (((1,), (1,)), ((), ()))


def _cparams(n_axes):
    return pltpu.CompilerParams(dimension_semantics=("arbitrary",) * n_axes,
                                vmem_limit_bytes=VMEM_LIMIT)


def _rms(x, g):
    return x * lax.rsqrt(jnp.mean(x * x, axis=-1, keepdims=True) + EPS) * g


def _split3(x):
    hi = x.astype(BF16)
    r1 = x - hi.astype(F32)
    mid = r1.astype(BF16)
    lo = (r1 - mid.astype(F32)).astype(BF16)
    return hi, mid, lo


def _store_row_tiles(ref, v):
    parts = v.shape[1] // LANES
    for s in range(parts):
        ref[pl.ds(s, v.shape[0], stride=parts), :] = v[:, s * LANES:(s + 1) * LANES]


def _load_row_tiles(ref, rows, parts):
    return jnp.concatenate([ref[pl.ds(s, rows, stride=parts), :] for s in range(parts)], axis=1)


def _lane_tile(a, width):
    return jnp.tile(a, (1, width // LANES))


def _rope_kernel(pos_col_ref, pos_row_ref, invf_lane_ref, sign_ref, invf_rep_ref,
                 cos_ref, sin_ref, cost_ref, sint_ref):
    ang = pos_col_ref[...] * invf_lane_ref[...]
    cos_ref[...] = jnp.cos(ang)
    sin_ref[...] = jnp.sin(ang) * sign_ref[...]
    t = pos_row_ref.shape[-1]
    ang_t = _lane_tile(invf_rep_ref[...], t) * pos_row_ref[0]
    cost_ref[...] = jnp.cos(ang_t)
    sint_ref[...] = jnp.sin(ang_t)


def _rope_tables(positions):
    n = positions.size
    half = MLA_ROPE // 2
    inv_freq = ROPE_THETA ** (-np.arange(0, MLA_ROPE, 2, dtype=np.float32) / MLA_ROPE)
    invf = np.zeros((1, LANES), np.float32)
    sign = np.zeros((1, LANES), np.float32)
    invf[0, MLA_NOPE:MLA_NOPE + half] = inv_freq
    invf[0, MLA_NOPE + half:MLA_QK] = inv_freq
    sign[0, MLA_NOPE:MLA_NOPE + half] = -1.0
    sign[0, MLA_NOPE + half:MLA_QK] = 1.0
    invf_rep = np.broadcast_to(inv_freq[:, None], (half, LANES)).astype(np.float32)
    pos = positions.astype(F32)
    t = TOKEN_TILE
    const = lambda i: (0, 0)
    return pl.pallas_call(
        _rope_kernel,
        out_shape=(jax.ShapeDtypeStruct((n, LANES), F32), jax.ShapeDtypeStruct((n, LANES), F32),
                   jax.ShapeDtypeStruct((half, n), F32), jax.ShapeDtypeStruct((half, n), F32)),
        grid=(n // t,),
        in_specs=[pl.BlockSpec((t, 1), lambda i: (i, 0)),
                  pl.BlockSpec((1, 1, t), lambda i: (i, 0, 0)),
                  pl.BlockSpec((1, LANES), const), pl.BlockSpec((1, LANES), const),
                  pl.BlockSpec((half, LANES), const)],
        out_specs=(pl.BlockSpec((t, LANES), lambda i: (i, 0)), pl.BlockSpec((t, LANES), lambda i: (i, 0)),
                   pl.BlockSpec((half, t), lambda i: (0, i)), pl.BlockSpec((half, t), lambda i: (0, i))),
        compiler_params=_cparams(1),
        name="rope_tables",
    )(pos.reshape(n, 1), pos.reshape(n // t, 1, t), jnp.asarray(invf), jnp.asarray(sign),
      jnp.asarray(invf_rep))


def _pre_kernel(x_ref, cos_ref, sin_ref, cost_ref, sint_ref, gattn_ref, wa_ref, wfk_ref, wfqt_ref, wfvt_ref,
                bfl_ref, tri_ref, plk_ref, plq_ref, gql_ref, wuqt_ref, gkvl_ref, wuk_ref, wuvt_ref,
                gq_ref, gk_ref, gfq_ref, gfk_ref,
                qt_out, k_out, vt_out, fqt_out, fk_out, fvt_out, fcum_out, carry_ref, *, tiles_per_seq):
    @pl.when(pl.program_id(0) % tiles_per_seq == 0)
    def _():
        carry_ref[...] = jnp.zeros_like(carry_ref)

    sub = tri_ref.shape[0]
    running = carry_ref[:1, :]
    for sb in range(x_ref.shape[0] // sub):
        running = _pre_rows(slice(sb * sub, (sb + 1) * sub), running, x_ref, cos_ref, sin_ref, cost_ref, sint_ref,
                            gattn_ref, wa_ref, wfk_ref, wfqt_ref, wfvt_ref, bfl_ref, tri_ref, plk_ref, plq_ref,
                            gql_ref, wuqt_ref, gkvl_ref, wuk_ref, wuvt_ref, gq_ref, gk_ref, gfq_ref, gfk_ref,
                            qt_out, k_out, vt_out, fqt_out, fk_out, fvt_out, fcum_out)
    carry_ref[...] = jnp.broadcast_to(running, carry_ref.shape)


def _pre_rows(rs, running, x_ref, cos_ref, sin_ref, cost_ref, sint_ref, gattn_ref, wa_ref, wfk_ref, wfqt_ref,
              wfvt_ref, bfl_ref, tri_ref, plk_ref, plq_ref, gql_ref, wuqt_ref, gkvl_ref, wuk_ref, wuvt_ref,
              gq_ref, gk_ref, gfq_ref, gfk_ref, qt_out, k_out, vt_out, fqt_out, fk_out, fvt_out, fcum_out):
    x = x_ref[rs, :]
    t = x.shape[0]
    h = _rms(x, gattn_ref[...]).astype(BF16)
    pa = jnp.dot(h, wa_ref[...], preferred_element_type=F32)
    fk = jnp.dot(h, wfk_ref[...], preferred_element_type=F32)
    fqt = lax.dot_general(wfqt_ref[...], h, _NT, preferred_element_type=F32)
    fvt_out[0, :, rs] = lax.dot_general(wfvt_ref[...], h, _NT, preferred_element_type=F32).astype(BF16)
    half = MLA_ROPE // 2
    lane = lax.broadcasted_iota(jnp.int32, (1, LANES), 1)
    low = lane < FOX_DIM

    def head_lanes(m, hd):
        v = m[:, (hd // 2) * LANES:(hd // 2 + 1) * LANES]
        if hd % 2:
            v = pltpu.roll(v, LANES // 2, 1)
        return jnp.where(low, v, 0.0)

    qn = _rms(pa[:, :Q_LORA], gql_ref[...]).astype(BF16)
    kvn = _rms(pa[:, Q_LORA:Q_LORA + KV_LORA], gkvl_ref[...]).astype(BF16)
    vt_out[0, :, rs] = lax.dot_general(wuvt_ref[...], kvn, _NT, preferred_element_type=F32).astype(BF16)

    kn = jnp.dot(kvn, wuk_ref[...], preferred_element_type=F32)
    kpe = pa[:, Q_LORA + KV_LORA:Q_LORA + KV_LORA + LANES]
    cos_l = cos_ref[rs, :]
    sin_l = sin_ref[rs, :]
    gk = gk_ref[...]
    for hd in range(HEADS):
        v = head_lanes(kn, hd) + kpe
        v = v * lax.rsqrt(jnp.sum(v * v, axis=-1, keepdims=True) * (1.0 / MLA_QK) + EPS) * gk
        swapped = jnp.where(lane < MLA_NOPE + half, pltpu.roll(v, LANES - half, 1), pltpu.roll(v, half, 1))
        k_out[rs, hd * LANES:(hd + 1) * LANES] = (v * cos_l + swapped * sin_l).astype(BF16)

    qt = lax.dot_general(wuqt_ref[...], qn, _NT, preferred_element_type=F32)
    cos_r = cost_ref[:, rs]
    sin_r = sint_ref[:, rs]
    gq = _lane_tile(gq_ref[...], t)[:MLA_QK]
    pad_q = jnp.zeros((LANES - MLA_QK, t), BF16)
    for hd in range(HEADS):
        blk = qt[hd * MLA_QK:(hd + 1) * MLA_QK]
        r = lax.rsqrt(jnp.sum(blk * blk, axis=0, keepdims=True) * (1.0 / MLA_QK) + EPS)
        blk = blk * r * gq
        x1, x2 = blk[MLA_NOPE:MLA_NOPE + half], blk[MLA_NOPE + half:]
        blk = jnp.concatenate([blk[:MLA_NOPE], x1 * cos_r - x2 * sin_r, x2 * cos_r + x1 * sin_r], axis=0)
        qt_out[0, hd * LANES:(hd + 1) * LANES, rs] = jnp.concatenate([blk.astype(BF16), pad_q], axis=0)


    def by_group(a, b, c):
        return jnp.where(lane < PIECE_STRIDE, a, jnp.where(lane < 2 * PIECE_STRIDE, b, c))

    z = pa[:, Q_LORA + KV_LORA + LANES:] + bfl_ref[...]
    logf = jnp.minimum(z, 0.0) - jnp.log1p(jnp.exp(-jnp.abs(z)))
    parts = jnp.dot(tri_ref[...], by_group(*_split3(logf)), preferred_element_type=F32)
    tot = parts + pltpu.roll(parts, PIECE_STRIDE, 1) + pltpu.roll(parts, 2 * PIECE_STRIDE, 1)
    cum = by_group(pltpu.roll(tot, LANES - 2 * PIECE_STRIDE, 1), pltpu.roll(tot, LANES - PIECE_STRIDE, 1), tot)
    cum = cum + running
    cum2 = cum * LOG2E
    fcum_out[rs, :] = cum2
    packed = by_group(*_split3(cum2))
    packed = jnp.where(lane == ONE_LANE, jnp.ones_like(packed), packed)
    aug_k = jnp.dot(packed, plk_ref[...], preferred_element_type=F32)
    aug_q = lax.dot_general(plq_ref[...], packed, _NT, preferred_element_type=F32)

    gfk = gfk_ref[...]
    for hd in range(HEADS):
        sl = slice(hd * LANES, (hd + 1) * LANES)
        v = head_lanes(fk, hd)
        v = v * lax.rsqrt(jnp.sum(v * v, axis=-1, keepdims=True) * (1.0 / FOX_DIM) + EPS) * gfk
        fk_out[rs, sl] = (v + aug_k[:, sl]).astype(BF16)

    gfq = _lane_tile(gfq_ref[...], t)[:FOX_DIM]
    for hd in range(HEADS):
        blk = fqt[hd * FOX_DIM:(hd + 1) * FOX_DIM]
        r = lax.rsqrt(jnp.sum(blk * blk, axis=0, keepdims=True) * (1.0 / FOX_DIM) + EPS)
        aug = aug_q[hd * LANES + FOX_DIM:(hd + 1) * LANES]
        fqt_out[0, hd * LANES:(hd + 1) * LANES, rs] = jnp.concatenate([blk * r * gfq, aug], axis=0).astype(BF16)
    return cum[t - 1:t, :]


def _pre_attention(x2d, tables, w, seq):
    n, d = x2d.shape
    t = TOKEN_TILE
    tiles_per_seq = seq // t
    batch = n // seq
    cos_l, sin_l, cos_t, sin_t = tables
    half = MLA_ROPE // 2
    row = lambda i: (i, 0)
    seq_t = lambda i: (i // tiles_per_seq, 0, i % tiles_per_seq)

    def full(a):
        return pl.BlockSpec(a.shape, lambda i, nd=a.ndim: (0,) * nd)

    weights = [w["g_attn"], w["w_a"], w["w_fk"], w["w_fq_t"], w["w_fv_t"], w["b_fl"], w["tri_incl"],
               w["place_k"], w["place_q"], w["g_q_lora"], w["w_uq_t"], w["g_kv_lora"], w["w_uk"], w["w_uv_t"],
               w["g_mla_q"], w["g_mla_k"], w["g_fox_q"], w["g_fox_k"]]
    in_specs = [pl.BlockSpec((t, d), row), pl.BlockSpec((t, LANES), row), pl.BlockSpec((t, LANES), row),
                pl.BlockSpec((half, t), lambda i: (0, i)), pl.BlockSpec((half, t), lambda i: (0, i))]
    in_specs += [full(a) for a in weights]
    wq = HEADS * LANES
    wv = HEADS * HEAD_V
    out_shape = (jax.ShapeDtypeStruct((batch, wq, seq), BF16), jax.ShapeDtypeStruct((n, wq), BF16),
                 jax.ShapeDtypeStruct((batch, wv, seq), BF16), jax.ShapeDtypeStruct((batch, wq, seq), BF16),
                 jax.ShapeDtypeStruct((n, wq), BF16), jax.ShapeDtypeStruct((batch, wv, seq), BF16),
                 jax.ShapeDtypeStruct((n, LANES), F32))
    out_specs = (pl.BlockSpec((1, wq, t), seq_t), pl.BlockSpec((t, wq), row), pl.BlockSpec((1, wv, t), seq_t),
                 pl.BlockSpec((1, wq, t), seq_t), pl.BlockSpec((t, wq), row), pl.BlockSpec((1, wv, t), seq_t),
                 pl.BlockSpec((t, LANES), row))
    return pl.pallas_call(
        functools.partial(_pre_kernel, tiles_per_seq=tiles_per_seq),
        out_shape=out_shape,
        grid=(n // t,),
        in_specs=in_specs,
        out_specs=out_specs,
        scratch_shapes=[pltpu.VMEM((8, LANES), F32)],
        compiler_params=_cparams(1),
        name="pre_attention",
    )(x2d, cos_l, sin_l, cos_t, sin_t, *weights)


def _allowed(k0, q0, tk, tq, chunked):
    key = k0 + lax.broadcasted_iota(jnp.int32, (tk, tq), 0)
    qry = q0 + lax.broadcasted_iota(jnp.int32, (tk, tq), 1)
    if chunked:
        return (key // CHUNK) <= (qry // CHUNK)
    return key <= qry


def _rowmax_kernel(qt_ref, k_ref, m_ref, *, chunked, tq, tk):
    qi = pl.program_id(2)
    n_diag = tq // tk

    def block(kb, carry, masked):
        k0 = pl.multiple_of(kb * tk, tk)
        kblk = k_ref[0, pl.ds(k0, tk), :]
        out = []
        for i in range(2):
            s = jnp.dot(kblk[:, i * LANES:(i + 1) * LANES], qt_ref[0, i * LANES:(i + 1) * LANES, :],
                        preferred_element_type=F32)
            if masked:
                s = jnp.where(_allowed(k0, qi * tq, tk, tq, chunked), s, NEG_INF)
            out.append(jnp.maximum(carry[i], jnp.max(s, axis=0, keepdims=True)))
        return tuple(out)

    init = tuple(jnp.full((1, tq), NEG_INF, F32) for _ in range(2))
    carry = lax.fori_loop(0, qi * n_diag, lambda kb, c: block(kb, c, False), init)
    for j in range(n_diag):
        carry = block(qi * n_diag + j, carry, True)
    m_ref[0, 0] = jnp.concatenate(carry, axis=0)


def _row_max(qt, k, *, chunked):
    batch, seq, wk = k.shape
    tq, tk = ATTN_TQ, ATTN_TK
    pairs = wk // (2 * LANES)
    out = pl.pallas_call(
        functools.partial(_rowmax_kernel, chunked=chunked, tq=tq, tk=tk),
        out_shape=jax.ShapeDtypeStruct((batch, pairs, 2, seq), F32),
        grid=(batch, pairs, seq // tq),
        in_specs=[pl.BlockSpec((1, 2 * LANES, tq), lambda b, h, i: (b, h, i)),
                  pl.BlockSpec((1, seq, 2 * LANES), lambda b, h, i: (b, 0, h))],
        out_specs=pl.BlockSpec((1, 1, 2, tq), lambda b, h, i: (b, h, 0, i)),
        compiler_params=_cparams(3),
        name="attn_rowmax",
    )(qt, k)
    return out


def _attn_kernel(start_ref, qt_ref, k_ref, vt_ref, shift_ref, mask_ref, o_ref, acc_ref, *, tq, tk):
    pairs = pl.num_programs(1)
    n_q = pl.num_programs(2)
    qi = pl.program_id(2)
    acc_ref[...] = jnp.zeros_like(acc_ref)

    def block(k0, nk, q_lo, masked):
        qs = slice(q_lo, tq)
        kblk = k_ref[0, pl.ds(k0, nk), :]
        vt = vt_ref[0, :, pl.ds(k0, nk)]
        ones = jnp.ones((SUM_ROWS, nk), BF16)
        for i in range(2):
            s = jnp.dot(kblk[:, i * LANES:(i + 1) * LANES], qt_ref[0, i * LANES:(i + 1) * LANES, qs],
                        preferred_element_type=F32)
            s = s - shift_ref[0, 0, i:i + 1, qs]
            if masked:
                s = s + mask_ref[q_lo:q_lo + nk, qs]
            p = jnp.exp2(s).astype(BF16)
            lhs = jnp.concatenate([vt[i * HEAD_V:(i + 1) * HEAD_V], ones], axis=0)
            acc_ref[i, :, qs] += jnp.dot(lhs, p, preferred_element_type=F32)

    def full_block(kb):
        block(pl.multiple_of(kb * tk, tk), tk, 0, False)

    first = start_ref[(pl.program_id(0) * pairs + pl.program_id(1)) * n_q + qi]
    count = qi - first

    @pl.when(count % 2 == 1)
    def _():
        full_block(first)

    def body(it, c):
        kb = first + count % 2 + 2 * it
        full_block(kb)
        full_block(kb + 1)
        return c

    lax.fori_loop(0, count // 2, body, 0)
    half = tk // 2
    q0 = pl.multiple_of(qi * tq, tq)
    block(q0, half, 0, True)
    block(pl.multiple_of(q0 + half, half), half, half, True)
    o_t = jnp.concatenate([acc_ref[i, :HEAD_V, :] / acc_ref[i, HEAD_V:HEAD_V + 1, :] for i in range(2)], axis=0)
    o_ref[0] = o_t.T.astype(o_ref.dtype)


def _diag_mask(tq, chunked):
    key = np.arange(tq)[:, None]
    qry = np.arange(tq)[None, :]
    ok = (key // CHUNK) <= (qry // CHUNK) if chunked else key <= qry
    return jnp.asarray(np.where(ok, 0.0, NEG_INF).astype(np.float32))


def _first_block(fcum, batch, seq, tq, tk):
    f = fcum.reshape(batch, seq, LANES)[:, :, :HEADS]
    f_q0 = f[:, ::tq, :]
    f_kl = f[:, tk - 1::tk, :]
    dead = (f_q0[:, :, None, :] - f_kl[:, None, :, :]) < -SKIP_LOG2
    dead = dead.reshape(batch, seq // tq, seq // tk, HEADS // 2, 2).all(axis=-1)
    lead = jnp.cumprod(dead.astype(jnp.int32), axis=2).sum(axis=2)
    limit = (jnp.arange(seq // tq, dtype=jnp.int32) * (tq // tk))[None, :, None]
    return jnp.minimum(lead, limit).transpose(0, 2, 1).reshape(-1).astype(jnp.int32)


def _attention(qt, k, vt, bound, fcum, *, chunked):
    batch, seq, wk = k.shape
    tq, tk = ATTN_TQ, ATTN_TK
    pairs = wk // (2 * LANES)
    n_q = seq // tq
    fast = bound <= SAFE_SCORE_BOUND * LOG2E
    shift = lax.cond(fast, lambda: jnp.full((batch, pairs, 2, seq), bound, F32),
                     lambda: _row_max(qt, k, chunked=chunked))
    first = jnp.zeros((batch * pairs * n_q,), jnp.int32)
    if fcum is not None:
        first = jnp.where(fast, _first_block(fcum, batch, seq, tq, tk), first)
    return pl.pallas_call(
        functools.partial(_attn_kernel, tq=tq, tk=tk),
        out_shape=jax.ShapeDtypeStruct((batch, seq, pairs * 2 * HEAD_V), BF16),
        grid_spec=pltpu.PrefetchScalarGridSpec(
            num_scalar_prefetch=1,
            grid=(batch, pairs, n_q),
            in_specs=[pl.BlockSpec((1, 2 * LANES, tq), lambda b, h, i, st: (b, h, i)),
                      pl.BlockSpec((1, seq, 2 * LANES), lambda b, h, i, st: (b, 0, h)),
                      pl.BlockSpec((1, 2 * HEAD_V, seq), lambda b, h, i, st: (b, h, 0)),
                      pl.BlockSpec((1, 1, 2, tq), lambda b, h, i, st: (b, h, 0, i)),
                      pl.BlockSpec((tq, tq), lambda b, h, i, st: (0, 0))],
            out_specs=pl.BlockSpec((1, tq, 2 * HEAD_V), lambda b, h, i, st: (b, i, h)),
            scratch_shapes=[pltpu.VMEM((2, HEAD_V + SUM_ROWS, tq), F32)]),
        compiler_params=_cparams(3),
        name="attn_chunk_causal" if chunked else "attn_frame_causal",
    )(first, qt, k, vt, shift, _diag_mask(tq, chunked))


def _post_kernel(x_ref, om_ref, of_ref, gom_ref, gof_ref, wom_ref, wof_ref, gffn_ref, wrh_ref, wrl_ref,
                 tri_ref, x1_out, h2_out, route_out, count_out, carry_ref):
    step = pl.program_id(0)
    sub = tri_ref.shape[0]
    parts = x_ref.shape[1] // LANES

    @pl.when(step == 0)
    def _():
        carry_ref[...] = jnp.zeros_like(carry_ref)

    running = carry_ref[:1, :]
    for sb in range(x_ref.shape[0] // sub):
        running = _post_rows(slice(sb * sub, (sb + 1) * sub), pl.ds(sb * sub * parts, sub * parts), running,
                             x_ref, om_ref, of_ref, gom_ref, gof_ref, wom_ref, wof_ref, gffn_ref, wrh_ref,
                             wrl_ref, tri_ref, x1_out, h2_out, route_out)
    carry_ref[...] = jnp.broadcast_to(running, carry_ref.shape)
    count_out[...] = jnp.broadcast_to(running, count_out.shape)


def _post_rows(rs, tile_rows, running, x_ref, om_ref, of_ref, gom_ref, gof_ref, wom_ref, wof_ref, gffn_ref,
               wrh_ref, wrl_ref, tri_ref, x1_out, h2_out, route_out):
    ym = _rms(om_ref[rs, :].astype(F32), gom_ref[...]).astype(BF16)
    yf = _rms(of_ref[rs, :].astype(F32), gof_ref[...]).astype(BF16)
    x1 = (x_ref[rs, :] + jnp.dot(ym, wom_ref[...], preferred_element_type=F32)
          + jnp.dot(yf, wof_ref[...], preferred_element_type=F32))
    x1_out[rs, :] = x1
    h2 = _rms(x1, gffn_ref[...])
    _store_row_tiles(h2_out.at[tile_rows], h2)

    h_hi = h2.astype(BF16)
    h_lo = (h2 - h_hi.astype(F32)).astype(BF16)
    w_hi = wrh_ref[...]
    logits = (jnp.dot(h_hi, w_hi, preferred_element_type=F32)
              + jnp.dot(h_lo, w_hi, preferred_element_type=F32)
              + jnp.dot(h_hi, wrl_ref[...], preferred_element_type=F32))

    lane = lax.broadcasted_iota(jnp.int32, (1, LANES), 1).astype(F32)
    big = float(LANES)

    def first_argmax(v):
        mx = jnp.max(v, axis=-1, keepdims=True)
        idx = jnp.min(jnp.where(v == mx, lane, big), axis=-1, keepdims=True)
        return mx, idx

    lg = jnp.where(lane < N_GROUPS, logits, NEG_INF)
    mg, g_idx = first_argmax(lg)
    p_g = 1.0 / jnp.sum(jnp.exp(lg - mg), axis=-1, keepdims=True)
    e_lo = ROUTE_LANE0 + EXPERTS_PER_GROUP * g_idx
    le = jnp.where((lane >= e_lo) & (lane < e_lo + EXPERTS_PER_GROUP), logits, NEG_INF)
    m1, i1 = first_argmax(le)
    m2, i2 = first_argmax(jnp.where(lane == i1, NEG_INF, le))
    e2 = jnp.exp(m2 - m1)
    gate1 = p_g / (1.0 + e2)
    gate2 = p_g * e2 / (1.0 + e2)

    oh1 = lane == i1
    oh2 = lane == i2
    oh = (jnp.where(oh1, 1.0, 0.0) + jnp.where(oh2, 1.0, 0.0))
    before = jnp.dot(tri_ref[...], oh.astype(BF16), preferred_element_type=F32)
    base = before + running
    rank1 = jnp.sum(jnp.where(oh1, base, 0.0), axis=-1, keepdims=True)
    rank2 = jnp.sum(jnp.where(oh2, base, 0.0), axis=-1, keepdims=True)

    route = jnp.where(lane == 0, i1 - ROUTE_LANE0, 0.0)
    route = jnp.where(lane == 1, i2 - ROUTE_LANE0, route)
    route = jnp.where(lane == 2, rank1, route)
    route = jnp.where(lane == 3, rank2, route)
    route = jnp.where(lane == 4, gate1, route)
    route = jnp.where(lane == 5, gate2, route)
    route_out[rs, :] = route
    return running + jnp.sum(oh, axis=0, keepdims=True)


def _post_attention(x2d, o_mla, o_fox, w):
    n, d = x2d.shape
    t = POST_TILE
    parts = d // LANES
    const = lambda i: (0, 0)
    row = lambda i: (i, 0)
    full = lambda a: pl.BlockSpec(a.shape, const)
    ins = [x2d, o_mla, o_fox, w["g_out_mla"], w["g_out_fox"], w["w_out_mla"], w["w_out_fox"],
           w["g_ffn"], w["w_router_hi"], w["w_router_lo"], w["tri_strict"]]
    in_specs = [pl.BlockSpec((t, d), row), pl.BlockSpec((t, o_mla.shape[1]), row),
                pl.BlockSpec((t, o_fox.shape[1]), row)] + [full(a) for a in ins[3:]]
    return pl.pallas_call(
        _post_kernel,
        out_shape=(jax.ShapeDtypeStruct((n, d), F32), jax.ShapeDtypeStruct((n * parts, LANES), F32),
                   jax.ShapeDtypeStruct((n, LANES), F32), jax.ShapeDtypeStruct((8, LANES), F32)),
        grid=(n // t,),
        in_specs=in_specs,
        out_specs=(pl.BlockSpec((t, d), row), pl.BlockSpec((t * parts, LANES), row),
                   pl.BlockSpec((t, LANES), row), pl.BlockSpec((8, LANES), const)),
        scratch_shapes=[pltpu.VMEM((8, LANES), F32)],
        compiler_params=_cparams(1),
        name="post_attention",
    )(*ins)


def _dispatch_kernel(pad_end_ref, padded_ref, n_used_ref, dest_ref, h_ref, xs_hbm, zero_ref, sem, zsem, *,
                     tile, parts):
    zrows = zero_ref.shape[0]
    n_blk = xs_hbm.shape[0] // zrows
    rblk = zrows // parts

    def zero_copy(blk):
        return pltpu.make_async_copy(zero_ref, xs_hbm.at[pl.ds(pl.multiple_of(blk * zrows, zrows), zrows)], zsem)

    @pl.when(pl.program_id(0) == 0)
    def _():
        zero_ref[...] = jnp.zeros_like(zero_ref)
        for wait in (False, True):
            for e in range(N_EXPERTS):
                @pl.when(padded_ref[e] > 0)
                def _():
                    cp = zero_copy(pad_end_ref[e] // rblk - 1)
                    cp.wait() if wait else cp.start()

            def tail(blk, c):
                cp = zero_copy(blk)
                cp.wait() if wait else cp.start()
                return c

            lax.fori_loop(n_used_ref[0], n_blk, tail, 0)

    def issue(jo, c):
        for ji in range(ISSUE_UNROLL):
            j = jo * ISSUE_UNROLL + ji
            src = h_ref.at[pl.ds(pl.multiple_of(j * parts, parts), parts)]
            for k in range(2):
                d = pl.multiple_of(dest_ref[0, 0, 2 * j + k], parts)
                pltpu.make_async_copy(src, xs_hbm.at[pl.ds(d, parts)], sem).start(priority=k)
        return c

    lax.fori_loop(0, tile // ISSUE_UNROLL, issue, 0)
    for _ in range(2):
        pltpu.make_async_copy(h_ref, xs_hbm.at[pl.ds(0, tile * parts)], sem).wait()


def _dispatch(h2t, plan, parts):
    n = h2t.shape[0] // parts
    tile = DISPATCH_TILE
    dest3 = (plan["dest"] * parts).reshape(n // tile, 1, 2 * tile)
    return pl.pallas_call(
        functools.partial(_dispatch_kernel, tile=tile, parts=parts),
        out_shape=jax.ShapeDtypeStruct((plan["n_rows"] * parts, LANES), h2t.dtype),
        grid_spec=pltpu.PrefetchScalarGridSpec(
            num_scalar_prefetch=3,
            grid=(n // tile,),
            in_specs=[pl.BlockSpec((1, 1, 2 * tile), lambda i, *_: (i, 0, 0), memory_space=pltpu.SMEM),
                      pl.BlockSpec((tile * parts, LANES), lambda i, *_: (i, 0))],
            out_specs=pl.BlockSpec(memory_space=pl.ANY),
            scratch_shapes=[pltpu.VMEM((EXPERT_BLOCK * parts, LANES), h2t.dtype), pltpu.SemaphoreType.DMA,
                            pltpu.SemaphoreType.DMA]),
        compiler_params=_cparams(1),
        name="moe_dispatch",
    )(plan["pad_end"], plan["padded"], plan["n_used"], dest3, h2t)


def _expert_kernel(blk_e_ref, n_used_ref, x_ref, wg_ref, wu_ref, wdn_ref, y_ref, wgu_bf, wd_bf, *, parts):
    i = pl.program_id(0)
    used = i < n_used_ref[0]
    rows = x_ref.shape[0] // parts
    fresh = jnp.logical_or(i == 0, blk_e_ref[i] != blk_e_ref[jnp.maximum(i - 1, 0)])

    @pl.when(jnp.logical_and(used, fresh))
    def _():
        wgu_bf[:, :EXPERT_FF] = wg_ref[0, 0].astype(BF16)
        wgu_bf[:, EXPERT_FF:] = wu_ref[0, 0].astype(BF16)
        wd_bf[...] = wdn_ref[0, 0].astype(BF16)

    @pl.when(used)
    def _():
        sub = EXPERT_SUB
        for sb in range(rows // sub):
            tiles = pl.ds(sb * sub * parts, sub * parts)
            xb = _load_row_tiles(x_ref.at[tiles], sub, parts).astype(BF16)
            gu = jnp.dot(xb, wgu_bf[...], preferred_element_type=F32)
            g = gu[:, :EXPERT_FF]
            act = (g / (1.0 + jnp.exp(-g))) * gu[:, EXPERT_FF:]
            _store_row_tiles(y_ref.at[tiles], jnp.dot(act.astype(BF16), wd_bf[...], preferred_element_type=F32))

    @pl.when(jnp.logical_not(used))
    def _():
        y_ref[...] = jnp.zeros_like(y_ref)


def _experts(xs, plan, layer, w_gate, w_up, w_down, parts):
    r = EXPERT_BLOCK
    d = parts * LANES
    rows = lambda i, be, nu: (jnp.minimum(i, nu[0] - 1), 0)
    wsel = lambda i, be, nu: (layer, be[i], 0, 0)
    return pl.pallas_call(
        functools.partial(_expert_kernel, parts=parts),
        out_shape=jax.ShapeDtypeStruct(xs.shape, F32),
        grid_spec=pltpu.PrefetchScalarGridSpec(
            num_scalar_prefetch=2,
            grid=(xs.shape[0] // (r * parts),),
            in_specs=[pl.BlockSpec((r * parts, LANES), rows),
                      pl.BlockSpec((1, 1, d, EXPERT_FF), wsel),
                      pl.BlockSpec((1, 1, d, EXPERT_FF), wsel),
                      pl.BlockSpec((1, 1, EXPERT_FF, d), wsel)],
            out_specs=pl.BlockSpec((r * parts, LANES), lambda i, be, nu: (i, 0)),
            scratch_shapes=[pltpu.VMEM((d, 2 * EXPERT_FF), BF16), pltpu.VMEM((EXPERT_FF, d), BF16)]),
        compiler_params=_cparams(1),
        name="moe_experts",
    )(plan["blk_e"], plan["n_used"], xs, w_gate, w_up, w_down)


def _ple_kernel(dest_ref, dest_next_ref, x1_ref, route_ref, p_ref, gnorm_ref, wgate_ref, wproj_ref, gout_ref,
                ys_hbm, x_out, rows_a, rows_b, sem, *, tile, parts):
    step = pl.program_id(0)
    n_steps = pl.num_programs(0)

    def row_copy(d, buf, slot, k, j):
        return pltpu.make_async_copy(ys_hbm.at[pl.ds(pl.multiple_of(d, parts), parts)],
                                     buf.at[k, pl.ds(j * parts, parts)], sem.at[slot])

    def wait_rows(buf, slot):
        for k in range(2):
            pltpu.make_async_copy(ys_hbm.at[pl.ds(0, tile * parts)], buf.at[k], sem.at[slot]).wait()

    @pl.when(step == 0)
    def _():
        def issue(jo, c):
            for ji in range(ISSUE_UNROLL):
                j = jo * ISSUE_UNROLL + ji
                for k in range(2):
                    pltpu.make_async_copy(
                        ys_hbm.at[pl.ds(pl.multiple_of(dest_ref[0, 0, 2 * j + k], parts), parts)],
                        rows_a.at[k, pl.ds(pl.multiple_of(j * parts, parts), parts)], sem.at[0]).start(priority=k)
            return c
        lax.fori_loop(0, tile // ISSUE_UNROLL, issue, 0)

    def body(cur, nxt, slot):
        wait_rows(cur, slot)
        for j in range(tile):
            for k in range(2):
                row_copy(dest_next_ref[0, 0, 2 * j + k], nxt, 1 - slot, k, j).start(priority=k)
        route = route_ref[...]
        y1 = _load_row_tiles(cur.at[0], tile, parts)
        y2 = _load_row_tiles(cur.at[1], tile, parts)
        x2 = x1_ref[...] + route[:, 4:5] * y1 + route[:, 5:6] * y2
        ple = _rms(jnp.dot(p_ref[0].astype(BF16), wproj_ref[...], preferred_element_type=F32), gout_ref[...])
        z = jnp.dot(_rms(x2, gnorm_ref[...]).astype(BF16), wgate_ref[...], preferred_element_type=F32)
        x_out[...] = x2 + ple / (1.0 + jnp.exp(-z))

        @pl.when(step == n_steps - 1)
        def _():
            wait_rows(nxt, 1 - slot)

    @pl.when(step % 2 == 0)
    def _():
        body(rows_a, rows_b, 0)

    @pl.when(step % 2 == 1)
    def _():
        body(rows_b, rows_a, 1)


def _combine_ple(x1, route, dest, ys, p3d, layer, w):
    n, d = x1.shape
    t = PLE_TILE
    parts = d // LANES
    steps = n // t
    const = lambda i: (0, 0)
    row = lambda i: (i, 0)
    full = lambda a: pl.BlockSpec(a.shape, const)
    dest3 = (dest * parts).reshape(steps, 1, 2 * t)
    return pl.pallas_call(
        functools.partial(_ple_kernel, tile=t, parts=parts),
        out_shape=jax.ShapeDtypeStruct((n, d), F32),
        grid=(steps,),
        in_specs=[pl.BlockSpec((1, 1, 2 * t), lambda i: (i, 0, 0), memory_space=pltpu.SMEM),
                  pl.BlockSpec((1, 1, 2 * t), lambda i: (jnp.minimum(i + 1, steps - 1), 0, 0),
                               memory_space=pltpu.SMEM),
                  pl.BlockSpec((t, d), row), pl.BlockSpec((t, LANES), row),
                  pl.BlockSpec((1, t, p3d.shape[2]), lambda i: (layer, i, 0)),
                  full(w["g_ple_norm"]), full(w["w_ple_gate"]), full(w["w_ple_proj"]), full(w["g_ple_out"]),
                  pl.BlockSpec(memory_space=pl.ANY)],
        out_specs=pl.BlockSpec((t, d), row),
        scratch_shapes=[pltpu.VMEM((2, t * parts, LANES), F32), pltpu.VMEM((2, t * parts, LANES), F32),
                        pltpu.SemaphoreType.DMA((2,))],
        compiler_params=_cparams(1),
        name="moe_combine_ple",
    )(dest3, dest3, x1, route, p3d, w["g_ple_norm"], w["w_ple_gate"], w["w_ple_proj"], w["g_ple_out"], ys)


def _row(v):
    return v.reshape(1, -1).astype(F32)


def _col_rep(v):
    return jnp.broadcast_to(v.astype(F32)[:, None], (v.shape[0], LANES))


def _pad_heads(wmat, real):
    k = wmat.shape[0]
    return jnp.pad(wmat.reshape(k, HEADS, real), ((0, 0), (0, 0), (0, LANES - real))).reshape(k, HEADS * LANES)


def _placement():
    place_k = np.zeros((LANES, HEADS * LANES), np.float32)
    place_q = np.zeros((HEADS * LANES, LANES), np.float32)
    for hd in range(HEADS):
        base = hd * LANES
        for j in range(3):
            src = j * PIECE_STRIDE + hd
            place_k[src, base + AUG_K + j] = -1.0
            place_q[base + AUG_Q + j, src] = 1.0
            place_k[ONE_LANE, base + AUG_Q + j] = 1.0
            place_q[base + AUG_K + j, ONE_LANE] = 1.0
    return jnp.asarray(place_k, BF16), jnp.asarray(place_q, BF16)


def _layer_weights(i, g_attn_norm, w_in, g_q_lora, w_uq, g_kv_lora, w_ukv, g_mla_q, g_mla_k, g_fox_q,
                   g_fox_k, b_fox_f, g_out_mla, g_out_fox, w_out, g_ffn_norm, w_router_group,
                   w_router_expert, w_exp_gate, w_exp_up, w_exp_down, g_ple_norm, w_ple_gate, w_ple_proj,
                   g_ple_out):
    d = w_in.shape[1]
    wf = HEADS * FOX_DIM
    c_kv = Q_LORA
    c_pe = c_kv + KV_LORA
    c_fq = c_pe + MLA_ROPE
    c_fk = c_fq + wf
    c_fv = c_fk + wf
    c_fl = c_fv + wf
    win = w_in[i]
    pad_pe = jnp.zeros((d, LANES), F32).at[:, MLA_NOPE:MLA_QK].set(win[:, c_pe:c_fq])
    group = lambda v: jnp.concatenate(
        [jnp.pad(v, [(0, 0)] * (v.ndim - 1) + [(0, PIECE_STRIDE - HEADS)])] * 3
        + [jnp.zeros(v.shape[:-1] + (LANES - 3 * PIECE_STRIDE,), v.dtype)], axis=-1)
    pad_fl = group(win[:, c_fl:])
    w = {}
    w["g_attn"] = _row(g_attn_norm[i])
    w["w_a"] = jnp.concatenate([win[:, :c_pe], pad_pe, pad_fl], axis=1).astype(BF16)
    w["w_fk"] = win[:, c_fk:c_fv].astype(BF16)
    w["w_fq_t"] = win[:, c_fq:c_fk].T.astype(BF16)
    w["w_fv_t"] = win[:, c_fv:c_fl].T.astype(BF16)
    w["b_fl"] = group(b_fox_f[i].astype(F32)[None, :])
    w["place_k"], w["place_q"] = _placement()
    w["g_q_lora"] = _row(g_q_lora[i])
    w["w_uq_t"] = w_uq[i].T.astype(BF16)
    w["g_kv_lora"] = _row(g_kv_lora[i])
    ukv = w_ukv[i].reshape(KV_LORA, HEADS, MLA_NOPE + HEAD_V)
    w["w_uk"] = ukv[:, :, :MLA_NOPE].reshape(KV_LORA, -1).astype(BF16)
    w["w_uv_t"] = ukv[:, :, MLA_NOPE:].reshape(KV_LORA, -1).T.astype(BF16)
    w["g_mla_q"] = _col_rep(jnp.pad(g_mla_q[i], (0, LANES - MLA_QK))) * (MLA_QK ** -0.5 * LOG2E)
    w["g_mla_k"] = _row(jnp.pad(g_mla_k[i], (0, LANES - MLA_QK)))
    w["g_fox_q"] = _col_rep(jnp.pad(g_fox_q[i], (0, LANES - FOX_DIM))) * (FOX_DIM ** -0.5 * LOG2E)
    w["g_fox_k"] = _row(jnp.pad(g_fox_k[i], (0, LANES - FOX_DIM)))
    bound = lambda gq, gk, dim: (1.02 * LOG2E * dim ** 0.5) * jnp.max(jnp.abs(gq)) * jnp.max(jnp.abs(gk))
    w["bound_mla"] = bound(g_mla_q[i], g_mla_k[i], MLA_QK).astype(F32)
    w["bound_fox"] = bound(g_fox_q[i], g_fox_k[i], FOX_DIM).astype(F32)
    w["g_out_mla"] = _row(g_out_mla[i])
    w["g_out_fox"] = _row(g_out_fox[i])
    wm = HEADS * HEAD_V
    w["w_out_mla"] = w_out[i, :wm].astype(BF16)
    w["w_out_fox"] = w_out[i, wm:].astype(BF16)
    w["g_ffn"] = _row(g_ffn_norm[i])
    wr = jnp.zeros((d, LANES), F32)
    wr = wr.at[:, :N_GROUPS].set(w_router_group[i]).at[:, ROUTE_LANE0:ROUTE_LANE0 + N_EXPERTS].set(
        w_router_expert[i])
    w["w_router_hi"] = wr.astype(BF16)
    w["w_router_lo"] = (wr - w["w_router_hi"].astype(F32)).astype(BF16)
    w["g_ple_norm"] = _row(g_ple_norm[i])
    w["w_ple_gate"] = w_ple_gate[i].astype(BF16)
    w["w_ple_proj"] = w_ple_proj[i].astype(BF16)
    w["g_ple_out"] = _row(g_ple_out[i])
    def tri(t, strict):
        r = lax.broadcasted_iota(jnp.int32, (t, t), 0)
        c = lax.broadcasted_iota(jnp.int32, (t, t), 1)
        return ((c < r) if strict else (c <= r)).astype(BF16)

    w["tri_incl"] = tri(PRE_SUB, False)
    w["tri_strict"] = tri(POST_SUB, True)
    return w


def _route_plan(route, counts, n):
    r = EXPERT_BLOCK
    cnt = counts[0, ROUTE_LANE0:ROUTE_LANE0 + N_EXPERTS].astype(jnp.int32)
    padded = ((cnt + r - 1) // r) * r
    pad_end = jnp.cumsum(padded)
    pad_start = pad_end - padded
    e = route[:, 0:2].astype(jnp.int32)
    rank = route[:, 2:4].astype(jnp.int32)
    onehot = e[:, :, None] == jnp.arange(N_EXPERTS, dtype=jnp.int32)[None, None, :]
    dest = jnp.sum(jnp.where(onehot, pad_start[None, None, :], 0), axis=-1) + rank
    n_rows = 2 * n + N_EXPERTS * r
    blk_start = jnp.arange(n_rows // r, dtype=jnp.int32) * r
    blk_e = jnp.sum((blk_start[:, None] >= pad_end[None, :]).astype(jnp.int32), axis=1)
    blk_e = jnp.minimum(blk_e, N_EXPERTS - 1)
    n_used = (pad_end[-1:] // r).astype(jnp.int32)
    return {"dest": dest.reshape(-1), "blk_e": blk_e, "n_used": n_used, "n_rows": n_rows,
            "pad_end": pad_end.astype(jnp.int32), "padded": padded.astype(jnp.int32)}


def kernel(x, p, positions, g_attn_norm, w_in, g_q_lora, w_uq, g_kv_lora, w_ukv, g_mla_q, g_mla_k, g_fox_q,
           g_fox_k, b_fox_f, g_out_mla, g_out_fox, w_out, g_ffn_norm, w_router_group, w_router_expert,
           w_exp_gate, w_exp_up, w_exp_down, g_ple_norm, w_ple_gate, w_ple_proj, g_ple_out):
    batch, seq, d = x.shape
    n = batch * seq
    depth = w_in.shape[0]
    params = (g_attn_norm, w_in, g_q_lora, w_uq, g_kv_lora, w_ukv, g_mla_q, g_mla_k, g_fox_q, g_fox_k,
              b_fox_f, g_out_mla, g_out_fox, w_out, g_ffn_norm, w_router_group, w_router_expert,
              w_exp_gate, w_exp_up, w_exp_down, g_ple_norm, w_ple_gate, w_ple_proj, g_ple_out)
    tables = _rope_tables(positions)
    xc = x.reshape(n, d)
    for i in range(depth):
        w = _layer_weights(i, *params)
        qt, k, vt, fqt, fk, fvt, fcum = _pre_attention(xc, tables, w, seq)
        b3 = lambda a: a.reshape(batch, seq, a.shape[-1])
        o_mla = _attention(qt, b3(k), vt, w["bound_mla"], None, chunked=True).reshape(n, -1)
        o_fox = _attention(fqt, b3(fk), fvt, w["bound_fox"], fcum, chunked=False).reshape(n, -1)
        x1, h2, route, counts = _post_attention(xc, o_mla, o_fox, w)
        plan = _route_plan(route, counts, n)
        parts = d // LANES
        xs = _dispatch(h2, plan, parts)
        ys = _experts(xs, plan, i, w_exp_gate, w_exp_up, w_exp_down, parts)
        xc = _combine_ple(x1, route, plan["dest"], ys, p.reshape(depth, n, -1), i, w)
    return xc.reshape(batch, seq, d)
```

```python
import functools
import math

import numpy as np
import jax
import jax.numpy as jnp
from jax import lax
from jax.experimental import pallas as pl
from jax.experimental.pallas import tpu as pltpu

F32 = jnp.float32
BF16 = jnp.bfloat16

EPS = 1e-6
NEG_INF = -1e30
ROPE_THETA = 10000.0
LOG2E = math.log2(math.e)

LANES = 128
CHUNK = 64
HEADS = 8
MLA_NOPE = 64
MLA_ROPE = 32
MLA_QK = MLA_NOPE + MLA_ROPE
HEAD_V = 64
FOX_DIM = 64
Q_LORA = 256
KV_LORA = 128
N_GROUPS = 4
EXPERTS_PER_GROUP = 8
N_EXPERTS = N_GROUPS * EXPERTS_PER_GROUP
EXPERT_FF = 256
ROUTE_LANE0 = N_GROUPS

AUG_Q = FOX_DIM
AUG_K = FOX_DIM + 3
PIECE_STRIDE = 16
ONE_LANE = 3 * PIECE_STRIDE

TOKEN_TILE = 512
PRE_SUB = 512
POST_TILE = 1024
POST_SUB = 256
ATTN_TQ = 1024
ATTN_TK = 1024
SUM_ROWS = 16
EXPERT_BLOCK = 512
EXPERT_SUB = 256
DISPATCH_TILE = 1024
PLE_TILE = 256
ISSUE_UNROLL = 8
VMEM_LIMIT = 56 * 1024 * 1024
SAFE_SCORE_BOUND = 40.0
SKIP_LOG2 = 160.0

_NT = (((1,), (1,)), ((), ()))


def _cparams(n_axes):
    return pltpu.CompilerParams(dimension_semantics=("arbitrary",) * n_axes,
                                vmem_limit_bytes=VMEM_LIMIT)


def _rms(x, g):
    return x * lax.rsqrt(jnp.mean(x * x, axis=-1, keepdims=True) + EPS) * g


def _split3(x):
    hi = x.astype(BF16)
    r1 = x - hi.astype(F32)
    mid = r1.astype(BF16)
    lo = (r1 - mid.astype(F32)).astype(BF16)
    return hi, mid, lo


def _store_row_tiles(ref, v):
    parts = v.shape[1] // LANES
    for s in range(parts):
        ref[pl.ds(s, v.shape[0], stride=parts), :] = v[:, s * LANES:(s + 1) * LANES]


def _load_row_tiles(ref, rows, parts):
    return jnp.concatenate([ref[pl.ds(s, rows, stride=parts), :] for s in range(parts)], axis=1)


def _lane_tile(a, width):
    return jnp.tile(a, (1, width // LANES))


def _rope_kernel(pos_col_ref, pos_row_ref, invf_lane_ref, sign_ref, invf_rep_ref,
                 cos_ref, sin_ref, cost_ref, sint_ref):
    ang = pos_col_ref[...] * invf_lane_ref[...]
    cos_ref[...] = jnp.cos(ang)
    sin_ref[...] = jnp.sin(ang) * sign_ref[...]
    t = pos_row_ref.shape[-1]
    ang_t = _lane_tile(invf_rep_ref[...], t) * pos_row_ref[0]
    cost_ref[...] = jnp.cos(ang_t)
    sint_ref[...] = jnp.sin(ang_t)


def _rope_tables(positions):
    n = positions.size
    half = MLA_ROPE // 2
    inv_freq = ROPE_THETA ** (-np.arange(0, MLA_ROPE, 2, dtype=np.float32) / MLA_ROPE)
    invf = np.zeros((1, LANES), np.float32)
    sign = np.zeros((1, LANES), np.float32)
    invf[0, MLA_NOPE:MLA_NOPE + half] = inv_freq
    invf[0, MLA_NOPE + half:MLA_QK] = inv_freq
    sign[0, MLA_NOPE:MLA_NOPE + half] = -1.0
    sign[0, MLA_NOPE + half:MLA_QK] = 1.0
    invf_rep = np.broadcast_to(inv_freq[:, None], (half, LANES)).astype(np.float32)
    pos = positions.astype(F32)
    t = TOKEN_TILE
    const = lambda i: (0, 0)
    return pl.pallas_call(
        _rope_kernel,
        out_shape=(jax.ShapeDtypeStruct((n, LANES), F32), jax.ShapeDtypeStruct((n, LANES), F32),
                   jax.ShapeDtypeStruct((half, n), F32), jax.ShapeDtypeStruct((half, n), F32)),
        grid=(n // t,),
        in_specs=[pl.BlockSpec((t, 1), lambda i: (i, 0)),
                  pl.BlockSpec((1, 1, t), lambda i: (i, 0, 0)),
                  pl.BlockSpec((1, LANES), const), pl.BlockSpec((1, LANES), const),
                  pl.BlockSpec((half, LANES), const)],
        out_specs=(pl.BlockSpec((t, LANES), lambda i: (i, 0)), pl.BlockSpec((t, LANES), lambda i: (i, 0)),
                   pl.BlockSpec((half, t), lambda i: (0, i)), pl.BlockSpec((half, t), lambda i: (0, i))),
        compiler_params=_cparams(1),
        name="rope_tables",
    )(pos.reshape(n, 1), pos.reshape(n // t, 1, t), jnp.asarray(invf), jnp.asarray(sign),
      jnp.asarray(invf_rep))


def _pre_kernel(x_ref, cos_ref, sin_ref, cost_ref, sint_ref, gattn_ref, wa_ref, wfk_ref, wfqt_ref, wfvt_ref,
                bfl_ref, tri_ref, plk_ref, plq_ref, gql_ref, wuqt_ref, gkvl_ref, wuk_ref, wuvt_ref,
                gq_ref, gk_ref, gfq_ref, gfk_ref,
                qt_out, k_out, vt_out, fqt_out, fk_out, fvt_out, fcum_out, carry_ref, *, tiles_per_seq):
    @pl.when(pl.program_id(0) % tiles_per_seq == 0)
    def _():
        carry_ref[...] = jnp.zeros_like(carry_ref)

    sub = tri_ref.shape[0]
    running = carry_ref[:1, :]
    for sb in range(x_ref.shape[0] // sub):
        running = _pre_rows(slice(sb * sub, (sb + 1) * sub), running, x_ref, cos_ref, sin_ref, cost_ref, sint_ref,
                            gattn_ref, wa_ref, wfk_ref, wfqt_ref, wfvt_ref, bfl_ref, tri_ref, plk_ref, plq_ref,
                            gql_ref, wuqt_ref, gkvl_ref, wuk_ref, wuvt_ref, gq_ref, gk_ref, gfq_ref, gfk_ref,
                            qt_out, k_out, vt_out, fqt_out, fk_out, fvt_out, fcum_out)
    carry_ref[...] = jnp.broadcast_to(running, carry_ref.shape)


def _pre_rows(rs, running, x_ref, cos_ref, sin_ref, cost_ref, sint_ref, gattn_ref, wa_ref, wfk_ref, wfqt_ref,
              wfvt_ref, bfl_ref, tri_ref, plk_ref, plq_ref, gql_ref, wuqt_ref, gkvl_ref, wuk_ref, wuvt_ref,
              gq_ref, gk_ref, gfq_ref, gfk_ref, qt_out, k_out, vt_out, fqt_out, fk_out, fvt_out, fcum_out):
    x = x_ref[rs, :]
    t = x.shape[0]
    h = _rms(x, gattn_ref[...]).astype(BF16)
    pa = jnp.dot(h, wa_ref[...], preferred_element_type=F32)
    fk = jnp.dot(h, wfk_ref[...], preferred_element_type=F32)
    fqt = lax.dot_general(wfqt_ref[...], h, _NT, preferred_element_type=F32)
    fvt_out[0, :, rs] = lax.dot_general(wfvt_ref[...], h, _NT, preferred_element_type=F32).astype(BF16)
    half = MLA_ROPE // 2
    lane = lax.broadcasted_iota(jnp.int32, (1, LANES), 1)
    low = lane < FOX_DIM

    def head_lanes(m, hd):
        v = m[:, (hd // 2) * LANES:(hd // 2 + 1) * LANES]
        if hd % 2:
            v = pltpu.roll(v, LANES // 2, 1)
        return jnp.where(low, v, 0.0)

    qn = _rms(pa[:, :Q_LORA], gql_ref[...]).astype(BF16)
    kvn = _rms(pa[:, Q_LORA:Q_LORA + KV_LORA], gkvl_ref[...]).astype(BF16)
    vt_out[0, :, rs] = lax.dot_general(wuvt_ref[...], kvn, _NT, preferred_element_type=F32).astype(BF16)

    kn = jnp.dot(kvn, wuk_ref[...], preferred_element_type=F32)
    kpe = pa[:, Q_LORA + KV_LORA:Q_LORA + KV_LORA + LANES]
    cos_l = cos_ref[rs, :]
    sin_l = sin_ref[rs, :]
    gk = gk_ref[...]
    for hd in range(HEADS):
        v = head_lanes(kn, hd) + kpe
        v = v * lax.rsqrt(jnp.sum(v * v, axis=-1, keepdims=True) * (1.0 / MLA_QK) + EPS) * gk
        swapped = jnp.where(lane < MLA_NOPE + half, pltpu.roll(v, LANES - half, 1), pltpu.roll(v, half, 1))
        k_out[rs, hd * LANES:(hd + 1) * LANES] = (v * cos_l + swapped * sin_l).astype(BF16)

    qt = lax.dot_general(wuqt_ref[...], qn, _NT, preferred_element_type=F32)
    cos_r = cost_ref[:, rs]
    sin_r = sint_ref[:, rs]
    gq = _lane_tile(gq_ref[...], t)[:MLA_QK]
    pad_q = jnp.zeros((LANES - MLA_QK, t), BF16)
    for hd in range(HEADS):
        blk = qt[hd * MLA_QK:(hd + 1) * MLA_QK]
        r = lax.rsqrt(jnp.sum(blk * blk, axis=0, keepdims=True) * (1.0 / MLA_QK) + EPS)
        blk = blk * r * gq
        x1, x2 = blk[MLA_NOPE:MLA_NOPE + half], blk[MLA_NOPE + half:]
        blk = jnp.concatenate([blk[:MLA_NOPE], x1 * cos_r - x2 * sin_r, x2 * cos_r + x1 * sin_r], axis=0)
        qt_out[0, hd * LANES:(hd + 1) * LANES, rs] = jnp.concatenate([blk.astype(BF16), pad_q], axis=0)


    def by_group(a, b, c):
        return jnp.where(lane < PIECE_STRIDE, a, jnp.where(lane < 2 * PIECE_STRIDE, b, c))

    z = pa[:, Q_LORA + KV_LORA + LANES:] + bfl_ref[...]
    logf = jnp.minimum(z, 0.0) - jnp.log1p(jnp.exp(-jnp.abs(z)))
    parts = jnp.dot(tri_ref[...], by_group(*_split3(logf)), preferred_element_type=F32)
    tot = parts + pltpu.roll(parts, PIECE_STRIDE, 1) + pltpu.roll(parts, 2 * PIECE_STRIDE, 1)
    cum = by_group(pltpu.roll(tot, LANES - 2 * PIECE_STRIDE, 1), pltpu.roll(tot, LANES - PIECE_STRIDE, 1), tot)
    cum = cum + running
    cum2 = cum * LOG2E
    fcum_out[rs, :] = cum2
    packed = by_group(*_split3(cum2))
    packed = jnp.where(lane == ONE_LANE, jnp.ones_like(packed), packed)
    aug_k = jnp.dot(packed, plk_ref[...], preferred_element_type=F32)
    aug_q = lax.dot_general(plq_ref[...], packed, _NT, preferred_element_type=F32)

    gfk = gfk_ref[...]
    for hd in range(HEADS):
        sl = slice(hd * LANES, (hd + 1) * LANES)
        v = head_lanes(fk, hd)
        v = v * lax.rsqrt(jnp.sum(v * v, axis=-1, keepdims=True) * (1.0 / FOX_DIM) + EPS) * gfk
        fk_out[rs, sl] = (v + aug_k[:, sl]).astype(BF16)

    gfq = _lane_tile(gfq_ref[...], t)[:FOX_DIM]
    for hd in range(HEADS):
        blk = fqt[hd * FOX_DIM:(hd + 1) * FOX_DIM]
        r = lax.rsqrt(jnp.sum(blk * blk, axis=0, keepdims=True) * (1.0 / FOX_DIM) + EPS)
        aug = aug_q[hd * LANES + FOX_DIM:(hd + 1) * LANES]
        fqt_out[0, hd * LANES:(hd + 1) * LANES, rs] = jnp.concatenate([blk * r * gfq, aug], axis=0).astype(BF16)
    return cum[t - 1:t, :]


def _pre_attention(x2d, tables, w, seq):
    n, d = x2d.shape
    t = TOKEN_TILE
    tiles_per_seq = seq // t
    batch = n // seq
    cos_l, sin_l, cos_t, sin_t = tables
    half = MLA_ROPE // 2
    row = lambda i: (i, 0)
    seq_t = lambda i: (i // tiles_per_seq, 0, i % tiles_per_seq)

    def full(a):
        return pl.BlockSpec(a.shape, lambda i, nd=a.ndim: (0,) * nd)

    weights = [w["g_attn"], w["w_a"], w["w_fk"], w["w_fq_t"], w["w_fv_t"], w["b_fl"], w["tri_incl"],
               w["place_k"], w["place_q"], w["g_q_lora"], w["w_uq_t"], w["g_kv_lora"], w["w_uk"], w["w_uv_t"],
               w["g_mla_q"], w["g_mla_k"], w["g_fox_q"], w["g_fox_k"]]
    in_specs = [pl.BlockSpec((t, d), row), pl.BlockSpec((t, LANES), row), pl.BlockSpec((t, LANES), row),
                pl.BlockSpec((half, t), lambda i: (0, i)), pl.BlockSpec((half, t), lambda i: (0, i))]
    in_specs += [full(a) for a in weights]
    wq = HEADS * LANES
    wv = HEADS * HEAD_V
    out_shape = (jax.ShapeDtypeStruct((batch, wq, seq), BF16), jax.ShapeDtypeStruct((n, wq), BF16),
                 jax.ShapeDtypeStruct((batch, wv, seq), BF16), jax.ShapeDtypeStruct((batch, wq, seq), BF16),
                 jax.ShapeDtypeStruct((n, wq), BF16), jax.ShapeDtypeStruct((batch, wv, seq), BF16),
                 jax.ShapeDtypeStruct((n, LANES), F32))
    out_specs = (pl.BlockSpec((1, wq, t), seq_t), pl.BlockSpec((t, wq), row), pl.BlockSpec((1, wv, t), seq_t),
                 pl.BlockSpec((1, wq, t), seq_t), pl.BlockSpec((t, wq), row), pl.BlockSpec((1, wv, t), seq_t),
                 pl.BlockSpec((t, LANES), row))
    return pl.pallas_call(
        functools.partial(_pre_kernel, tiles_per_seq=tiles_per_seq),
        out_shape=out_shape,
        grid=(n // t,),
        in_specs=in_specs,
        out_specs=out_specs,
        scratch_shapes=[pltpu.VMEM((8, LANES), F32)],
        compiler_params=_cparams(1),
        name="pre_attention",
    )(x2d, cos_l, sin_l, cos_t, sin_t, *weights)


def _allowed(k0, q0, tk, tq, chunked):
    key = k0 + lax.broadcasted_iota(jnp.int32, (tk, tq), 0)
    qry = q0 + lax.broadcasted_iota(jnp.int32, (tk, tq), 1)
    if chunked:
        return (key // CHUNK) <= (qry // CHUNK)
    return key <= qry


def _rowmax_kernel(qt_ref, k_ref, m_ref, *, chunked, tq, tk):
    qi = pl.program_id(2)
    n_diag = tq // tk

    def block(kb, carry, masked):
        k0 = pl.multiple_of(kb * tk, tk)
        kblk = k_ref[0, pl.ds(k0, tk), :]
        out = []
        for i in range(2):
            s = jnp.dot(kblk[:, i * LANES:(i + 1) * LANES], qt_ref[0, i * LANES:(i + 1) * LANES, :],
                        preferred_element_type=F32)
            if masked:
                s = jnp.where(_allowed(k0, qi * tq, tk, tq, chunked), s, NEG_INF)
            out.append(jnp.maximum(carry[i], jnp.max(s, axis=0, keepdims=True)))
        return tuple(out)

    init = tuple(jnp.full((1, tq), NEG_INF, F32) for _ in range(2))
    carry = lax.fori_loop(0, qi * n_diag, lambda kb, c: block(kb, c, False), init)
    for j in range(n_diag):
        carry = block(qi * n_diag + j, carry, True)
    m_ref[0, 0] = jnp.concatenate(carry, axis=0)


def _row_max(qt, k, *, chunked):
    batch, seq, wk = k.shape
    tq, tk = ATTN_TQ, ATTN_TK
    pairs = wk // (2 * LANES)
    out = pl.pallas_call(
        functools.partial(_rowmax_kernel, chunked=chunked, tq=tq, tk=tk),
        out_shape=jax.ShapeDtypeStruct((batch, pairs, 2, seq), F32),
        grid=(batch, pairs, seq // tq),
        in_specs=[pl.BlockSpec((1, 2 * LANES, tq), lambda b, h, i: (b, h, i)),
                  pl.BlockSpec((1, seq, 2 * LANES), lambda b, h, i: (b, 0, h))],
        out_specs=pl.BlockSpec((1, 1, 2, tq), lambda b, h, i: (b, h, 0, i)),
        compiler_params=_cparams(3),
        name="attn_rowmax",
    )(qt, k)
    return out


def _attn_kernel(start_ref, qt_ref, k_ref, vt_ref, shift_ref, mask_ref, o_ref, acc_ref, *, tq, tk):
    pairs = pl.num_programs(1)
    n_q = pl.num_programs(2)
    qi = pl.program_id(2)
    acc_ref[...] = jnp.zeros_like(acc_ref)

    def block(k0, nk, q_lo, masked):
        qs = slice(q_lo, tq)
        kblk = k_ref[0, pl.ds(k0, nk), :]
        vt = vt_ref[0, :, pl.ds(k0, nk)]
        ones = jnp.ones((SUM_ROWS, nk), BF16)
        for i in range(2):
            s = jnp.dot(kblk[:, i * LANES:(i + 1) * LANES], qt_ref[0, i * LANES:(i + 1) * LANES, qs],
                        preferred_element_type=F32)
            s = s - shift_ref[0, 0, i:i + 1, qs]
            if masked:
                s = s + mask_ref[q_lo:q_lo + nk, qs]
            p = jnp.exp2(s).astype(BF16)
            lhs = jnp.concatenate([vt[i * HEAD_V:(i + 1) * HEAD_V], ones], axis=0)
            acc_ref[i, :, qs] += jnp.dot(lhs, p, preferred_element_type=F32)

    def full_block(kb):
        block(pl.multiple_of(kb * tk, tk), tk, 0, False)

    first = start_ref[(pl.program_id(0) * pairs + pl.program_id(1)) * n_q + qi]
    count = qi - first

    @pl.when(count % 2 == 1)
    def _():
        full_block(first)

    def body(it, c):
        kb = first + count % 2 + 2 * it
        full_block(kb)
        full_block(kb + 1)
        return c

    lax.fori_loop(0, count // 2, body, 0)
    half = tk // 2
    q0 = pl.multiple_of(qi * tq, tq)
    block(q0, half, 0, True)
    block(pl.multiple_of(q0 + half, half), half, half, True)
    o_t = jnp.concatenate([acc_ref[i, :HEAD_V, :] / acc_ref[i, HEAD_V:HEAD_V + 1, :] for i in range(2)], axis=0)
    o_ref[0] = o_t.T.astype(o_ref.dtype)


def _diag_mask(tq, chunked):
    key = np.arange(tq)[:, None]
    qry = np.arange(tq)[None, :]
    ok = (key // CHUNK) <= (qry // CHUNK) if chunked else key <= qry
    return jnp.asarray(np.where(ok, 0.0, NEG_INF).astype(np.float32))


def _first_block(fcum, batch, seq, tq, tk):
    f = fcum.reshape(batch, seq, LANES)[:, :, :HEADS]
    f_q0 = f[:, ::tq, :]
    f_kl = f[:, tk - 1::tk, :]
    dead = (f_q0[:, :, None, :] - f_kl[:, None, :, :]) < -SKIP_LOG2
    dead = dead.reshape(batch, seq // tq, seq // tk, HEADS // 2, 2).all(axis=-1)
    lead = jnp.cumprod(dead.astype(jnp.int32), axis=2).sum(axis=2)
    limit = (jnp.arange(seq // tq, dtype=jnp.int32) * (tq // tk))[None, :, None]
    return jnp.minimum(lead, limit).transpose(0, 2, 1).reshape(-1).astype(jnp.int32)


def _attention(qt, k, vt, bound, fcum, *, chunked):
    batch, seq, wk = k.shape
    tq, tk = ATTN_TQ, ATTN_TK
    pairs = wk // (2 * LANES)
    n_q = seq // tq
    fast = bound <= SAFE_SCORE_BOUND * LOG2E
    shift = lax.cond(fast, lambda: jnp.full((batch, pairs, 2, seq), bound, F32),
                     lambda: _row_max(qt, k, chunked=chunked))
    first = jnp.zeros((batch * pairs * n_q,), jnp.int32)
    if fcum is not None:
        first = jnp.where(fast, _first_block(fcum, batch, seq, tq, tk), first)
    return pl.pallas_call(
        functools.partial(_attn_kernel, tq=tq, tk=tk),
        out_shape=jax.ShapeDtypeStruct((batch, seq, pairs * 2 * HEAD_V), BF16),
        grid_spec=pltpu.PrefetchScalarGridSpec(
            num_scalar_prefetch=1,
            grid=(batch, pairs, n_q),
            in_specs=[pl.BlockSpec((1, 2 * LANES, tq), lambda b, h, i, st: (b, h, i)),
                      pl.BlockSpec((1, seq, 2 * LANES), lambda b, h, i, st: (b, 0, h)),
                      pl.BlockSpec((1, 2 * HEAD_V, seq), lambda b, h, i, st: (b, h, 0)),
                      pl.BlockSpec((1, 1, 2, tq), lambda b, h, i, st: (b, h, 0, i)),
                      pl.BlockSpec((tq, tq), lambda b, h, i, st: (0, 0))],
            out_specs=pl.BlockSpec((1, tq, 2 * HEAD_V), lambda b, h, i, st: (b, i, h)),
            scratch_shapes=[pltpu.VMEM((2, HEAD_V + SUM_ROWS, tq), F32)]),
        compiler_params=_cparams(3),
        name="attn_chunk_causal" if chunked else "attn_frame_causal",
    )(first, qt, k, vt, shift, _diag_mask(tq, chunked))


def _post_kernel(x_ref, om_ref, of_ref, gom_ref, gof_ref, wom_ref, wof_ref, gffn_ref, wrh_ref, wrl_ref,
                 tri_ref, x1_out, h2_out, route_out, route_t_out, count_out, carry_ref):
    step = pl.program_id(0)
    sub = tri_ref.shape[0]
    parts = x_ref.shape[1] // LANES

    @pl.when(step == 0)
    def _():
        carry_ref[...] = jnp.zeros_like(carry_ref)

    running = carry_ref[:1, :]
    for sb in range(x_ref.shape[0] // sub):
        running = _post_rows(slice(sb * sub, (sb + 1) * sub), pl.ds(sb * sub * parts, sub * parts), running,
                             x_ref, om_ref, of_ref, gom_ref, gof_ref, wom_ref, wof_ref, gffn_ref, wrh_ref,
                             wrl_ref, tri_ref, x1_out, h2_out, route_out, route_t_out)
    carry_ref[...] = jnp.broadcast_to(running, carry_ref.shape)
    count_out[...] = jnp.broadcast_to(running, count_out.shape)


def _post_rows(rs, tile_rows, running, x_ref, om_ref, of_ref, gom_ref, gof_ref, wom_ref, wof_ref, gffn_ref,
               wrh_ref, wrl_ref, tri_ref, x1_out, h2_out, route_out, route_t_out):
    ym = _rms(om_ref[rs, :].astype(F32), gom_ref[...]).astype(BF16)
    yf = _rms(of_ref[rs, :].astype(F32), gof_ref[...]).astype(BF16)
    x1 = (x_ref[rs, :] + jnp.dot(ym, wom_ref[...], preferred_element_type=F32)
          + jnp.dot(yf, wof_ref[...], preferred_element_type=F32))
    x1_out[rs, :] = x1
    h2 = _rms(x1, gffn_ref[...])
    _store_row_tiles(h2_out.at[tile_rows], h2)

    h_hi = h2.astype(BF16)
    h_lo = (h2 - h_hi.astype(F32)).astype(BF16)
    w_hi = wrh_ref[...]
    logits = (jnp.dot(h_hi, w_hi, preferred_element_type=F32)
              + jnp.dot(h_lo, w_hi, preferred_element_type=F32)
              + jnp.dot(h_hi, wrl_ref[...], preferred_element_type=F32))

    lane = lax.broadcasted_iota(jnp.int32, (1, LANES), 1).astype(F32)
    big = float(LANES)

    def first_argmax(v):
        mx = jnp.max(v, axis=-1, keepdims=True)
        idx = jnp.min(jnp.where(v == mx, lane, big), axis=-1, keepdims=True)
        return mx, idx

    lg = jnp.where(lane < N_GROUPS, logits, NEG_INF)
    mg, g_idx = first_argmax(lg)
    p_g = 1.0 / jnp.sum(jnp.exp(lg - mg), axis=-1, keepdims=True)
    e_lo = ROUTE_LANE0 + EXPERTS_PER_GROUP * g_idx
    le = jnp.where((lane >= e_lo) & (lane < e_lo + EXPERTS_PER_GROUP), logits, NEG_INF)
    m1, i1 = first_argmax(le)
    m2, i2 = first_argmax(jnp.where(lane == i1, NEG_INF, le))
    e2 = jnp.exp(m2 - m1)
    gate1 = p_g / (1.0 + e2)
    gate2 = p_g * e2 / (1.0 + e2)

    oh1 = lane == i1
    oh2 = lane == i2
    oh = (jnp.where(oh1, 1.0, 0.0) + jnp.where(oh2, 1.0, 0.0))
    before = jnp.dot(tri_ref[...], oh.astype(BF16), preferred_element_type=F32)
    base = before + running
    rank1 = jnp.sum(jnp.where(oh1, base, 0.0), axis=-1, keepdims=True)
    rank2 = jnp.sum(jnp.where(oh2, base, 0.0), axis=-1, keepdims=True)

    route = jnp.where(lane == 0, i1 - ROUTE_LANE0, 0.0)
    route = jnp.where(lane == 1, i2 - ROUTE_LANE0, route)
    route = jnp.where(lane == 2, rank1, route)
    route = jnp.where(lane == 3, rank2, route)
    route = jnp.where(lane == 4, gate1, route)
    route = jnp.where(lane == 5, gate2, route)
    route_out[rs, :] = route
    route_t_out[:, rs] = route.T[:route_t_out.shape[0]]
    return running + jnp.sum(oh, axis=0, keepdims=True)


def _post_attention(x2d, o_mla, o_fox, w):
    n, d = x2d.shape
    t = POST_TILE
    parts = d // LANES
    const = lambda i: (0, 0)
    row = lambda i: (i, 0)
    full = lambda a: pl.BlockSpec(a.shape, const)
    ins = [x2d, o_mla, o_fox, w["g_out_mla"], w["g_out_fox"], w["w_out_mla"], w["w_out_fox"],
           w["g_ffn"], w["w_router_hi"], w["w_router_lo"], w["tri_strict"]]
    in_specs = [pl.BlockSpec((t, d), row), pl.BlockSpec((t, o_mla.shape[1]), row),
                pl.BlockSpec((t, o_fox.shape[1]), row)] + [full(a) for a in ins[3:]]
    return pl.pallas_call(
        _post_kernel,
        out_shape=(jax.ShapeDtypeStruct((n, d), F32), jax.ShapeDtypeStruct((n * parts, LANES), F32),
                   jax.ShapeDtypeStruct((n, LANES), F32), jax.ShapeDtypeStruct((8, n), F32),
                   jax.ShapeDtypeStruct((8, LANES), F32)),
        grid=(n // t,),
        in_specs=in_specs,
        out_specs=(pl.BlockSpec((t, d), row), pl.BlockSpec((t * parts, LANES), row),
                   pl.BlockSpec((t, LANES), row), pl.BlockSpec((8, t), lambda i: (0, i)),
                   pl.BlockSpec((8, LANES), const)),
        scratch_shapes=[pltpu.VMEM((8, LANES), F32)],
        compiler_params=_cparams(1),
        name="post_attention",
    )(*ins)


def _tile_rows(dest, tile, parts):
    return (dest * parts).reshape(2, -1, tile).transpose(1, 0, 2)


def _dispatch_kernel(pad_end_ref, padded_ref, n_used_ref, dest_ref, h_ref, xs_hbm, zero_ref, sem, zsem, *,
                     tile, parts):
    zrows = zero_ref.shape[0]
    n_blk = xs_hbm.shape[0] // zrows
    rblk = zrows // parts

    def zero_copy(blk):
        return pltpu.make_async_copy(zero_ref, xs_hbm.at[pl.ds(pl.multiple_of(blk * zrows, zrows), zrows)], zsem)

    @pl.when(pl.program_id(0) == 0)
    def _():
        zero_ref[...] = jnp.zeros_like(zero_ref)
        for wait in (False, True):
            for e in range(N_EXPERTS):
                @pl.when(padded_ref[e] > 0)
                def _():
                    cp = zero_copy(pad_end_ref[e] // rblk - 1)
                    cp.wait() if wait else cp.start()

            def tail(blk, c):
                cp = zero_copy(blk)
                cp.wait() if wait else cp.start()
                return c

            lax.fori_loop(n_used_ref[0], n_blk, tail, 0)

    def issue(jo, c):
        for ji in range(ISSUE_UNROLL):
            j = jo * ISSUE_UNROLL + ji
            src = h_ref.at[pl.ds(pl.multiple_of(j * parts, parts), parts)]
            for k in range(2):
                d = pl.multiple_of(dest_ref[0, k, j], parts)
                pltpu.make_async_copy(src, xs_hbm.at[pl.ds(d, parts)], sem).start(priority=k)
        return c

    lax.fori_loop(0, tile // ISSUE_UNROLL, issue, 0)
    for _ in range(2):
        pltpu.make_async_copy(h_ref, xs_hbm.at[pl.ds(0, tile * parts)], sem).wait()


def _dispatch(h2t, plan, parts):
    n = h2t.shape[0] // parts
    tile = DISPATCH_TILE
    dest3 = _tile_rows(plan["dest"], tile, parts)
    return pl.pallas_call(
        functools.partial(_dispatch_kernel, tile=tile, parts=parts),
        out_shape=jax.ShapeDtypeStruct((plan["n_rows"] * parts, LANES), h2t.dtype),
        grid_spec=pltpu.PrefetchScalarGridSpec(
            num_scalar_prefetch=3,
            grid=(n // tile,),
            in_specs=[pl.BlockSpec((1, 2, tile), lambda i, *_: (i, 0, 0), memory_space=pltpu.SMEM),
                      pl.BlockSpec((tile * parts, LANES), lambda i, *_: (i, 0))],
            out_specs=pl.BlockSpec(memory_space=pl.ANY),
            scratch_shapes=[pltpu.VMEM((EXPERT_BLOCK * parts, LANES), h2t.dtype), pltpu.SemaphoreType.DMA,
                            pltpu.SemaphoreType.DMA]),
        compiler_params=_cparams(1),
        name="moe_dispatch",
    )(plan["pad_end"], plan["padded"], plan["n_used"], dest3, h2t)


def _expert_kernel(blk_e_ref, n_used_ref, x_ref, wg_ref, wu_ref, wdn_ref, y_ref, wgu_bf, wd_bf, *, parts):
    i = pl.program_id(0)
    used = i < n_used_ref[0]
    rows = x_ref.shape[0] // parts
    fresh = jnp.logical_or(i == 0, blk_e_ref[i] != blk_e_ref[jnp.maximum(i - 1, 0)])

    @pl.when(jnp.logical_and(used, fresh))
    def _():
        wgu_bf[:, :EXPERT_FF] = wg_ref[0, 0].astype(BF16)
        wgu_bf[:, EXPERT_FF:] = wu_ref[0, 0].astype(BF16)
        wd_bf[...] = wdn_ref[0, 0].astype(BF16)

    @pl.when(used)
    def _():
        sub = EXPERT_SUB
        for sb in range(rows // sub):
            tiles = pl.ds(sb * sub * parts, sub * parts)
            xb = _load_row_tiles(x_ref.at[tiles], sub, parts).astype(BF16)
            gu = jnp.dot(xb, wgu_bf[...], preferred_element_type=F32)
            g = gu[:, :EXPERT_FF]
            act = (g / (1.0 + jnp.exp(-g))) * gu[:, EXPERT_FF:]
            _store_row_tiles(y_ref.at[tiles], jnp.dot(act.astype(BF16), wd_bf[...], preferred_element_type=F32))

    @pl.when(jnp.logical_not(used))
    def _():
        y_ref[...] = jnp.zeros_like(y_ref)


def _experts(xs, plan, layer, w_gate, w_up, w_down, parts):
    r = EXPERT_BLOCK
    d = parts * LANES
    rows = lambda i, be, nu: (jnp.minimum(i, nu[0] - 1), 0)
    wsel = lambda i, be, nu: (layer, be[i], 0, 0)
    return pl.pallas_call(
        functools.partial(_expert_kernel, parts=parts),
        out_shape=jax.ShapeDtypeStruct(xs.shape, F32),
        grid_spec=pltpu.PrefetchScalarGridSpec(
            num_scalar_prefetch=2,
            grid=(xs.shape[0] // (r * parts),),
            in_specs=[pl.BlockSpec((r * parts, LANES), rows),
                      pl.BlockSpec((1, 1, d, EXPERT_FF), wsel),
                      pl.BlockSpec((1, 1, d, EXPERT_FF), wsel),
                      pl.BlockSpec((1, 1, EXPERT_FF, d), wsel)],
            out_specs=pl.BlockSpec((r * parts, LANES), lambda i, be, nu: (i, 0)),
            scratch_shapes=[pltpu.VMEM((d, 2 * EXPERT_FF), BF16), pltpu.VMEM((EXPERT_FF, d), BF16)]),
        compiler_params=_cparams(1),
        name="moe_experts",
    )(plan["blk_e"], plan["n_used"], xs, w_gate, w_up, w_down)


def _ple_kernel(dest_ref, dest_next_ref, x1_ref, route_ref, p_ref, gnorm_ref, wgate_ref, wproj_ref, gout_ref,
                ys_hbm, x_out, rows_a, rows_b, sem, *, tile, parts):
    step = pl.program_id(0)
    n_steps = pl.num_programs(0)

    def row_copy(d, buf, slot, k, j):
        return pltpu.make_async_copy(ys_hbm.at[pl.ds(pl.multiple_of(d, parts), parts)],
                                     buf.at[k, pl.ds(j * parts, parts)], sem.at[slot])

    def wait_rows(buf, slot):
        for k in range(2):
            pltpu.make_async_copy(ys_hbm.at[pl.ds(0, tile * parts)], buf.at[k], sem.at[slot]).wait()

    @pl.when(step == 0)
    def _():
        def issue(jo, c):
            for ji in range(ISSUE_UNROLL):
                j = jo * ISSUE_UNROLL + ji
                for k in range(2):
                    pltpu.make_async_copy(
                        ys_hbm.at[pl.ds(pl.multiple_of(dest_ref[0, k, j], parts), parts)],
                        rows_a.at[k, pl.ds(pl.multiple_of(j * parts, parts), parts)], sem.at[0]).start(priority=k)
            return c
        lax.fori_loop(0, tile // ISSUE_UNROLL, issue, 0)

    def body(cur, nxt, slot):
        wait_rows(cur, slot)
        for j in range(tile):
            for k in range(2):
                row_copy(dest_next_ref[0, k, j], nxt, 1 - slot, k, j).start(priority=k)
        route = route_ref[...]
        y1 = _load_row_tiles(cur.at[0], tile, parts)
        y2 = _load_row_tiles(cur.at[1], tile, parts)
        x2 = x1_ref[...] + route[:, 4:5] * y1 + route[:, 5:6] * y2
        ple = _rms(jnp.dot(p_ref[0].astype(BF16), wproj_ref[...], preferred_element_type=F32), gout_ref[...])
        z = jnp.dot(_rms(x2, gnorm_ref[...]).astype(BF16), wgate_ref[...], preferred_element_type=F32)
        x_out[...] = x2 + ple / (1.0 + jnp.exp(-z))

        @pl.when(step == n_steps - 1)
        def _():
            wait_rows(nxt, 1 - slot)

    @pl.when(step % 2 == 0)
    def _():
        body(rows_a, rows_b, 0)

    @pl.when(step % 2 == 1)
    def _():
        body(rows_b, rows_a, 1)


def _combine_ple(x1, route, dest, ys, p3d, layer, w):
    n, d = x1.shape
    t = PLE_TILE
    parts = d // LANES
    steps = n // t
    const = lambda i: (0, 0)
    row = lambda i: (i, 0)
    full = lambda a: pl.BlockSpec(a.shape, const)
    dest3 = _tile_rows(dest, t, parts)
    return pl.pallas_call(
        functools.partial(_ple_kernel, tile=t, parts=parts),
        out_shape=jax.ShapeDtypeStruct((n, d), F32),
        grid=(steps,),
        in_specs=[pl.BlockSpec((1, 2, t), lambda i: (i, 0, 0), memory_space=pltpu.SMEM),
                  pl.BlockSpec((1, 2, t), lambda i: (jnp.minimum(i + 1, steps - 1), 0, 0),
                               memory_space=pltpu.SMEM),
                  pl.BlockSpec((t, d), row), pl.BlockSpec((t, LANES), row),
                  pl.BlockSpec((1, t, p3d.shape[2]), lambda i: (layer, i, 0)),
                  full(w["g_ple_norm"]), full(w["w_ple_gate"]), full(w["w_ple_proj"]), full(w["g_ple_out"]),
                  pl.BlockSpec(memory_space=pl.ANY)],
        out_specs=pl.BlockSpec((t, d), row),
        scratch_shapes=[pltpu.VMEM((2, t * parts, LANES), F32), pltpu.VMEM((2, t * parts, LANES), F32),
                        pltpu.SemaphoreType.DMA((2,))],
        compiler_params=_cparams(1),
        name="moe_combine_ple",
    )(dest3, dest3, x1, route, p3d, w["g_ple_norm"], w["w_ple_gate"], w["w_ple_proj"], w["g_ple_out"], ys)


def _row(v):
    return v.reshape(1, -1).astype(F32)


def _col_rep(v):
    return jnp.broadcast_to(v.astype(F32)[:, None], (v.shape[0], LANES))


def _pad_heads(wmat, real):
    k = wmat.shape[0]
    return jnp.pad(wmat.reshape(k, HEADS, real), ((0, 0), (0, 0), (0, LANES - real))).reshape(k, HEADS * LANES)


def _placement():
    place_k = np.zeros((LANES, HEADS * LANES), np.float32)
    place_q = np.zeros((HEADS * LANES, LANES), np.float32)
    for hd in range(HEADS):
        base = hd * LANES
        for j in range(3):
            src = j * PIECE_STRIDE + hd
            place_k[src, base + AUG_K + j] = -1.0
            place_q[base + AUG_Q + j, src] = 1.0
            place_k[ONE_LANE, base + AUG_Q + j] = 1.0
            place_q[base + AUG_K + j, ONE_LANE] = 1.0
    return jnp.asarray(place_k, BF16), jnp.asarray(place_q, BF16)


def _layer_weights(i, g_attn_norm, w_in, g_q_lora, w_uq, g_kv_lora, w_ukv, g_mla_q, g_mla_k, g_fox_q,
                   g_fox_k, b_fox_f, g_out_mla, g_out_fox, w_out, g_ffn_norm, w_router_group,
                   w_router_expert, w_exp_gate, w_exp_up, w_exp_down, g_ple_norm, w_ple_gate, w_ple_proj,
                   g_ple_out):
    d = w_in.shape[1]
    wf = HEADS * FOX_DIM
    c_kv = Q_LORA
    c_pe = c_kv + KV_LORA
    c_fq = c_pe + MLA_ROPE
    c_fk = c_fq + wf
    c_fv = c_fk + wf
    c_fl = c_fv + wf
    win = w_in[i]
    pad_pe = jnp.zeros((d, LANES), F32).at[:, MLA_NOPE:MLA_QK].set(win[:, c_pe:c_fq])
    group = lambda v: jnp.concatenate(
        [jnp.pad(v, [(0, 0)] * (v.ndim - 1) + [(0, PIECE_STRIDE - HEADS)])] * 3
        + [jnp.zeros(v.shape[:-1] + (LANES - 3 * PIECE_STRIDE,), v.dtype)], axis=-1)
    pad_fl = group(win[:, c_fl:])
    w = {}
    w["g_attn"] = _row(g_attn_norm[i])
    w["w_a"] = jnp.concatenate([win[:, :c_pe], pad_pe, pad_fl], axis=1).astype(BF16)
    w["w_fk"] = win[:, c_fk:c_fv].astype(BF16)
    w["w_fq_t"] = win[:, c_fq:c_fk].T.astype(BF16)
    w["w_fv_t"] = win[:, c_fv:c_fl].T.astype(BF16)
    w["b_fl"] = group(b_fox_f[i].astype(F32)[None, :])
    w["place_k"], w["place_q"] = _placement()
    w["g_q_lora"] = _row(g_q_lora[i])
    w["w_uq_t"] = w_uq[i].T.astype(BF16)
    w["g_kv_lora"] = _row(g_kv_lora[i])
    ukv = w_ukv[i].reshape(KV_LORA, HEADS, MLA_NOPE + HEAD_V)
    w["w_uk"] = ukv[:, :, :MLA_NOPE].reshape(KV_LORA, -1).astype(BF16)
    w["w_uv_t"] = ukv[:, :, MLA_NOPE:].reshape(KV_LORA, -1).T.astype(BF16)
    w["g_mla_q"] = _col_rep(jnp.pad(g_mla_q[i], (0, LANES - MLA_QK))) * (MLA_QK ** -0.5 * LOG2E)
    w["g_mla_k"] = _row(jnp.pad(g_mla_k[i], (0, LANES - MLA_QK)))
    w["g_fox_q"] = _col_rep(jnp.pad(g_fox_q[i], (0, LANES - FOX_DIM))) * (FOX_DIM ** -0.5 * LOG2E)
    w["g_fox_k"] = _row(jnp.pad(g_fox_k[i], (0, LANES - FOX_DIM)))
    bound = lambda gq, gk, dim: (1.02 * LOG2E * dim ** 0.5) * jnp.max(jnp.abs(gq)) * jnp.max(jnp.abs(gk))
    w["bound_mla"] = bound(g_mla_q[i], g_mla_k[i], MLA_QK).astype(F32)
    w["bound_fox"] = bound(g_fox_q[i], g_fox_k[i], FOX_DIM).astype(F32)
    w["g_out_mla"] = _row(g_out_mla[i])
    w["g_out_fox"] = _row(g_out_fox[i])
    wm = HEADS * HEAD_V
    w["w_out_mla"] = w_out[i, :wm].astype(BF16)
    w["w_out_fox"] = w_out[i, wm:].astype(BF16)
    w["g_ffn"] = _row(g_ffn_norm[i])
    wr = jnp.zeros((d, LANES), F32)
    wr = wr.at[:, :N_GROUPS].set(w_router_group[i]).at[:, ROUTE_LANE0:ROUTE_LANE0 + N_EXPERTS].set(
        w_router_expert[i])
    w["w_router_hi"] = wr.astype(BF16)
    w["w_router_lo"] = (wr - w["w_router_hi"].astype(F32)).astype(BF16)
    w["g_ple_norm"] = _row(g_ple_norm[i])
    w["w_ple_gate"] = w_ple_gate[i].astype(BF16)
    w["w_ple_proj"] = w_ple_proj[i].astype(BF16)
    w["g_ple_out"] = _row(g_ple_out[i])
    def tri(t, strict):
        r = lax.broadcasted_iota(jnp.int32, (t, t), 0)
        c = lax.broadcasted_iota(jnp.int32, (t, t), 1)
        return ((c < r) if strict else (c <= r)).astype(BF16)

    w["tri_incl"] = tri(PRE_SUB, False)
    w["tri_strict"] = tri(POST_SUB, True)
    return w


def _route_plan(route_t, counts, n):
    r = EXPERT_BLOCK
    cnt = counts[0, ROUTE_LANE0:ROUTE_LANE0 + N_EXPERTS].astype(jnp.int32)
    padded = ((cnt + r - 1) // r) * r
    pad_end = jnp.cumsum(padded)
    pad_start = pad_end - padded
    e = route_t[0:2].astype(jnp.int32)
    rank = route_t[2:4].astype(jnp.int32)
    onehot = e[:, None, :] == jnp.arange(N_EXPERTS, dtype=jnp.int32)[None, :, None]
    dest = jnp.sum(jnp.where(onehot, pad_start[None, :, None], 0), axis=1) + rank
    n_rows = 2 * n + N_EXPERTS * r
    blk_start = jnp.arange(n_rows // r, dtype=jnp.int32) * r
    blk_e = jnp.sum((blk_start[:, None] >= pad_end[None, :]).astype(jnp.int32), axis=1)
    blk_e = jnp.minimum(blk_e, N_EXPERTS - 1)
    n_used = (pad_end[-1:] // r).astype(jnp.int32)
    return {"dest": dest, "blk_e": blk_e, "n_used": n_used, "n_rows": n_rows,
            "pad_end": pad_end.astype(jnp.int32), "padded": padded.astype(jnp.int32)}


def kernel(x, p, positions, g_attn_norm, w_in, g_q_lora, w_uq, g_kv_lora, w_ukv, g_mla_q, g_mla_k, g_fox_q,
           g_fox_k, b_fox_f, g_out_mla, g_out_fox, w_out, g_ffn_norm, w_router_group, w_router_expert,
           w_exp_gate, w_exp_up, w_exp_down, g_ple_norm, w_ple_gate, w_ple_proj, g_ple_out):
    batch, seq, d = x.shape
    n = batch * seq
    depth = w_in.shape[0]
    params = (g_attn_norm, w_in, g_q_lora, w_uq, g_kv_lora, w_ukv, g_mla_q, g_mla_k, g_fox_q, g_fox_k,
              b_fox_f, g_out_mla, g_out_fox, w_out, g_ffn_norm, w_router_group, w_router_expert,
              w_exp_gate, w_exp_up, w_exp_down, g_ple_norm, w_ple_gate, w_ple_proj, g_ple_out)
    tables = _rope_tables(positions)
    xc = x.reshape(n, d)
    for i in range(depth):
        w = _layer_weights(i, *params)
        qt, k, vt, fqt, fk, fvt, fcum = _pre_attention(xc, tables, w, seq)
        b3 = lambda a: a.reshape(batch, seq, a.shape[-1])
        o_mla = _attention(qt, b3(k), vt, w["bound_mla"], None, chunked=True).reshape(n, -1)
        o_fox = _attention(fqt, b3(fk), fvt, w["bound_fox"], fcum, chunked=False).reshape(n, -1)
        x1, h2, route, route_t, counts = _post_attention(xc, o_mla, o_fox, w)
        plan = _route_plan(route_t, counts, n)
        parts = d // LANES
        xs = _dispatch(h2, plan, parts)
        ys = _experts(xs, plan, i, w_exp_gate, w_exp_up, w_exp_down, parts)
        xc = _combine_ple(x1, route, plan["dest"], ys, p.reshape(depth, n, -1), i, w)
    return xc.reshape(batch, seq, d)
```

```python
import functools
import math

import numpy as np
import jax
import jax.numpy as jnp
from jax import lax
from jax.experimental import pallas as pl
from jax.experimental.pallas import tpu as pltpu

F32 = jnp.float32
BF16 = jnp.bfloat16

EPS = 1e-6
NEG_INF = -1e30
ROPE_THETA = 10000.0
LOG2E = math.log2(math.e)

LANES = 128
CHUNK = 64
HEADS = 8
MLA_NOPE = 64
MLA_ROPE = 32
MLA_QK = MLA_NOPE + MLA_ROPE
HEAD_V = 64
FOX_DIM = 64
Q_LORA = 256
KV_LORA = 128
N_GROUPS = 4
EXPERTS_PER_GROUP = 8
N_EXPERTS = N_GROUPS * EXPERTS_PER_GROUP
EXPERT_FF = 256
ROUTE_LANE0 = N_GROUPS

AUG_Q = FOX_DIM
AUG_K = FOX_DIM + 3
PIECE_STRIDE = 16
ONE_LANE = 3 * PIECE_STRIDE

TOKEN_TILE = 512
PRE_SUB = 512
POST_TILE = 1024
POST_SUB = 256
ATTN_TQ = 1024
ATTN_TK = 1024
SUM_ROWS = 16
EXPERT_BLOCK = 512
EXPERT_SUB = 256
DISPATCH_TILE = 1024
PLE_TILE = 256
ISSUE_UNROLL = 8
VMEM_LIMIT = 56 * 1024 * 1024
SAFE_SCORE_BOUND = 40.0
SKIP_LOG2 = 160.0

_NT = (((1,), (1,)), ((), ()))


def _cparams(n_axes):
    return pltpu.CompilerParams(dimension_semantics=("arbitrary",) * n_axes,
                                vmem_limit_bytes=VMEM_LIMIT)


def _rms(x, g):
    return x * lax.rsqrt(jnp.mean(x * x, axis=-1, keepdims=True) + EPS) * g


def _split3(x):
    hi = x.astype(BF16)
    r1 = x - hi.astype(F32)
    mid = r1.astype(BF16)
    lo = (r1 - mid.astype(F32)).astype(BF16)
    return hi, mid, lo


def _store_row_tiles(ref, v):
    parts = v.shape[1] // LANES
    for s in range(parts):
        ref[pl.ds(s, v.shape[0], stride=parts), :] = v[:, s * LANES:(s + 1) * LANES]


def _load_row_tiles(ref, rows, parts):
    return jnp.concatenate([ref[pl.ds(s, rows, stride=parts), :] for s in range(parts)], axis=1)


def _lane_tile(a, width):
    return jnp.tile(a, (1, width // LANES))


def _rope_kernel(pos_row_ref, invf_rep_ref, cos_ref, sin_ref, cost_ref, sint_ref):
    t = pos_row_ref.shape[-1]
    half = invf_rep_ref.shape[0]
    ang_t = _lane_tile(invf_rep_ref[...], t) * pos_row_ref[0]
    c = jnp.cos(ang_t)
    s = jnp.sin(ang_t)
    cost_ref[...] = c
    sint_ref[...] = s
    ones = jnp.ones((MLA_NOPE, t), F32)
    zeros = jnp.zeros((MLA_NOPE, t), F32)
    pad = LANES - MLA_NOPE - 2 * half
    cos_ref[...] = jnp.concatenate([ones, c, c, ones[:pad]], axis=0).T
    sin_ref[...] = jnp.concatenate([zeros, -s, s, zeros[:pad]], axis=0).T


def _rope_tables(positions):
    n = positions.size
    half = MLA_ROPE // 2
    inv_freq = ROPE_THETA ** (-np.arange(0, MLA_ROPE, 2, dtype=np.float32) / MLA_ROPE)
    invf_rep = np.broadcast_to(inv_freq[:, None], (half, LANES)).astype(np.float32)
    pos = positions.astype(F32)
    t = TOKEN_TILE
    return pl.pallas_call(
        _rope_kernel,
        out_shape=(jax.ShapeDtypeStruct((n, LANES), F32), jax.ShapeDtypeStruct((n, LANES), F32),
                   jax.ShapeDtypeStruct((half, n), F32), jax.ShapeDtypeStruct((half, n), F32)),
        grid=(n // t,),
        in_specs=[pl.BlockSpec((1, 1, t), lambda i: (i, 0, 0)),
                  pl.BlockSpec((half, LANES), lambda i: (0, 0))],
        out_specs=(pl.BlockSpec((t, LANES), lambda i: (i, 0)), pl.BlockSpec((t, LANES), lambda i: (i, 0)),
                   pl.BlockSpec((half, t), lambda i: (0, i)), pl.BlockSpec((half, t), lambda i: (0, i))),
        compiler_params=_cparams(1),
        name="rope_tables",
    )(pos.reshape(n // t, 1, t), jnp.asarray(invf_rep))


def _pre_kernel(x_ref, cos_ref, sin_ref, cost_ref, sint_ref, gattn_ref, wa_ref, wfk_ref, wfqt_ref, wfvt_ref,
                bfl_ref, tri_ref, plk_ref, plq_ref, gql_ref, wuqt_ref, gkvl_ref, wuk_ref, wuvt_ref,
                gq_ref, gk_ref, gfq_ref, gfk_ref,
                qt_out, k_out, vt_out, fqt_out, fk_out, fvt_out, fcum_out, carry_ref, *, tiles_per_seq):
    @pl.when(pl.program_id(0) % tiles_per_seq == 0)
    def _():
        carry_ref[...] = jnp.zeros_like(carry_ref)

    sub = tri_ref.shape[0]
    running = carry_ref[:1, :]
    for sb in range(x_ref.shape[0] // sub):
        running = _pre_rows(slice(sb * sub, (sb + 1) * sub), running, x_ref, cos_ref, sin_ref, cost_ref, sint_ref,
                            gattn_ref, wa_ref, wfk_ref, wfqt_ref, wfvt_ref, bfl_ref, tri_ref, plk_ref, plq_ref,
                            gql_ref, wuqt_ref, gkvl_ref, wuk_ref, wuvt_ref, gq_ref, gk_ref, gfq_ref, gfk_ref,
                            qt_out, k_out, vt_out, fqt_out, fk_out, fvt_out, fcum_out)
    carry_ref[...] = jnp.broadcast_to(running, carry_ref.shape)


def _pre_rows(rs, running, x_ref, cos_ref, sin_ref, cost_ref, sint_ref, gattn_ref, wa_ref, wfk_ref, wfqt_ref,
              wfvt_ref, bfl_ref, tri_ref, plk_ref, plq_ref, gql_ref, wuqt_ref, gkvl_ref, wuk_ref, wuvt_ref,
              gq_ref, gk_ref, gfq_ref, gfk_ref, qt_out, k_out, vt_out, fqt_out, fk_out, fvt_out, fcum_out):
    x = x_ref[rs, :]
    t = x.shape[0]
    h = _rms(x, gattn_ref[...]).astype(BF16)
    pa = jnp.dot(h, wa_ref[...], preferred_element_type=F32)
    fk = jnp.dot(h, wfk_ref[...], preferred_element_type=F32)
    fqt = lax.dot_general(wfqt_ref[...], h, _NT, preferred_element_type=F32)
    fvt_out[0, :, rs] = lax.dot_general(wfvt_ref[...], h, _NT, preferred_element_type=F32).astype(BF16)
    half = MLA_ROPE // 2
    lane = lax.broadcasted_iota(jnp.int32, (1, LANES), 1)
    low = lane < FOX_DIM

    def head_lanes(m, hd):
        v = m[:, (hd // 2) * LANES:(hd // 2 + 1) * LANES]
        if hd % 2:
            v = pltpu.roll(v, LANES // 2, 1)
        return jnp.where(low, v, 0.0)

    qn = _rms(pa[:, :Q_LORA], gql_ref[...]).astype(BF16)
    kvn = _rms(pa[:, Q_LORA:Q_LORA + KV_LORA], gkvl_ref[...]).astype(BF16)
    vt_out[0, :, rs] = lax.dot_general(wuvt_ref[...], kvn, _NT, preferred_element_type=F32).astype(BF16)

    kn = jnp.dot(kvn, wuk_ref[...], preferred_element_type=F32)
    kpe = pa[:, Q_LORA + KV_LORA:Q_LORA + KV_LORA + LANES]
    cos_l = cos_ref[rs, :]
    sin_l = sin_ref[rs, :]
    gk = gk_ref[...]
    for hd in range(HEADS):
        v = head_lanes(kn, hd) + kpe
        v = v * lax.rsqrt(jnp.sum(v * v, axis=-1, keepdims=True) * (1.0 / MLA_QK) + EPS) * gk
        swapped = jnp.where(lane < MLA_NOPE + half, pltpu.roll(v, LANES - half, 1), pltpu.roll(v, half, 1))
        k_out[rs, hd * LANES:(hd + 1) * LANES] = (v * cos_l + swapped * sin_l).astype(BF16)

    qt = lax.dot_general(wuqt_ref[...], qn, _NT, preferred_element_type=F32)
    cos_r = cost_ref[:, rs]
    sin_r = sint_ref[:, rs]
    gq = _lane_tile(gq_ref[...], t)[:MLA_QK]
    pad_q = jnp.zeros((LANES - MLA_QK, t), BF16)
    for hd in range(HEADS):
        blk = qt[hd * MLA_QK:(hd + 1) * MLA_QK]
        r = lax.rsqrt(jnp.sum(blk * blk, axis=0, keepdims=True) * (1.0 / MLA_QK) + EPS)
        blk = blk * r * gq
        x1, x2 = blk[MLA_NOPE:MLA_NOPE + half], blk[MLA_NOPE + half:]
        blk = jnp.concatenate([blk[:MLA_NOPE], x1 * cos_r - x2 * sin_r, x2 * cos_r + x1 * sin_r], axis=0)
        qt_out[0, hd * LANES:(hd + 1) * LANES, rs] = jnp.concatenate([blk.astype(BF16), pad_q], axis=0)


    def by_group(a, b, c):
        return jnp.where(lane < PIECE_STRIDE, a, jnp.where(lane < 2 * PIECE_STRIDE, b, c))

    z = pa[:, Q_LORA + KV_LORA + LANES:] + bfl_ref[...]
    logf = jnp.minimum(z, 0.0) - jnp.log1p(jnp.exp(-jnp.abs(z)))
    parts = jnp.dot(tri_ref[...], by_group(*_split3(logf)), preferred_element_type=F32)
    tot = parts + pltpu.roll(parts, PIECE_STRIDE, 1) + pltpu.roll(parts, 2 * PIECE_STRIDE, 1)
    cum = by_group(pltpu.roll(tot, LANES - 2 * PIECE_STRIDE, 1), pltpu.roll(tot, LANES - PIECE_STRIDE, 1), tot)
    cum = cum + running
    cum2 = cum * LOG2E
    fcum_out[rs, :] = cum2
    packed = by_group(*_split3(cum2))
    packed = jnp.where(lane == ONE_LANE, jnp.ones_like(packed), packed)
    aug_k = jnp.dot(packed, plk_ref[...], preferred_element_type=F32)
    aug_q = lax.dot_general(plq_ref[...], packed, _NT, preferred_element_type=F32)

    gfk = gfk_ref[...]
    for hd in range(HEADS):
        sl = slice(hd * LANES, (hd + 1) * LANES)
        v = head_lanes(fk, hd)
        v = v * lax.rsqrt(jnp.sum(v * v, axis=-1, keepdims=True) * (1.0 / FOX_DIM) + EPS) * gfk
        fk_out[rs, sl] = (v + aug_k[:, sl]).astype(BF16)

    gfq = _lane_tile(gfq_ref[...], t)[:FOX_DIM]
    for hd in range(HEADS):
        blk = fqt[hd * FOX_DIM:(hd + 1) * FOX_DIM]
        r = lax.rsqrt(jnp.sum(blk * blk, axis=0, keepdims=True) * (1.0 / FOX_DIM) + EPS)
        aug = aug_q[hd * LANES + FOX_DIM:(hd + 1) * LANES]
        fqt_out[0, hd * LANES:(hd + 1) * LANES, rs] = jnp.concatenate([blk * r * gfq, aug], axis=0).astype(BF16)
    return cum[t - 1:t, :]


def _pre_attention(x2d, tables, w, seq):
    n, d = x2d.shape
    t = TOKEN_TILE
    tiles_per_seq = seq // t
    batch = n // seq
    cos_l, sin_l, cos_t, sin_t = tables
    half = MLA_ROPE // 2
    row = lambda i: (i, 0)
    seq_t = lambda i: (i // tiles_per_seq, 0, i % tiles_per_seq)

    def full(a):
        return pl.BlockSpec(a.shape, lambda i, nd=a.ndim: (0,) * nd)

    weights = [w["g_attn"], w["w_a"], w["w_fk"], w["w_fq_t"], w["w_fv_t"], w["b_fl"], w["tri_incl"],
               w["place_k"], w["place_q"], w["g_q_lora"], w["w_uq_t"], w["g_kv_lora"], w["w_uk"], w["w_uv_t"],
               w["g_mla_q"], w["g_mla_k"], w["g_fox_q"], w["g_fox_k"]]
    in_specs = [pl.BlockSpec((t, d), row), pl.BlockSpec((t, LANES), row), pl.BlockSpec((t, LANES), row),
                pl.BlockSpec((half, t), lambda i: (0, i)), pl.BlockSpec((half, t), lambda i: (0, i))]
    in_specs += [full(a) for a in weights]
    wq = HEADS * LANES
    wv = HEADS * HEAD_V
    out_shape = (jax.ShapeDtypeStruct((batch, wq, seq), BF16), jax.ShapeDtypeStruct((n, wq), BF16),
                 jax.ShapeDtypeStruct((batch, wv, seq), BF16), jax.ShapeDtypeStruct((batch, wq, seq), BF16),
                 jax.ShapeDtypeStruct((n, wq), BF16), jax.ShapeDtypeStruct((batch, wv, seq), BF16),
                 jax.ShapeDtypeStruct((n, LANES), F32))
    out_specs = (pl.BlockSpec((1, wq, t), seq_t), pl.BlockSpec((t, wq), row), pl.BlockSpec((1, wv, t), seq_t),
                 pl.BlockSpec((1, wq, t), seq_t), pl.BlockSpec((t, wq), row), pl.BlockSpec((1, wv, t), seq_t),
                 pl.BlockSpec((t, LANES), row))
    return pl.pallas_call(
        functools.partial(_pre_kernel, tiles_per_seq=tiles_per_seq),
        out_shape=out_shape,
        grid=(n // t,),
        in_specs=in_specs,
        out_specs=out_specs,
        scratch_shapes=[pltpu.VMEM((8, LANES), F32)],
        compiler_params=_cparams(1),
        name="pre_attention",
    )(x2d, cos_l, sin_l, cos_t, sin_t, *weights)


def _allowed(k0, q0, tk, tq, chunked):
    key = k0 + lax.broadcasted_iota(jnp.int32, (tk, tq), 0)
    qry = q0 + lax.broadcasted_iota(jnp.int32, (tk, tq), 1)
    if chunked:
        return (key // CHUNK) <= (qry // CHUNK)
    return key <= qry


def _rowmax_kernel(qt_ref, k_ref, m_ref, *, chunked, tq, tk):
    qi = pl.program_id(2)
    n_diag = tq // tk

    def block(kb, carry, masked):
        k0 = pl.multiple_of(kb * tk, tk)
        kblk = k_ref[0, pl.ds(k0, tk), :]
        out = []
        for i in range(2):
            s = jnp.dot(kblk[:, i * LANES:(i + 1) * LANES], qt_ref[0, i * LANES:(i + 1) * LANES, :],
                        preferred_element_type=F32)
            if masked:
                s = jnp.where(_allowed(k0, qi * tq, tk, tq, chunked), s, NEG_INF)
            out.append(jnp.maximum(carry[i], jnp.max(s, axis=0, keepdims=True)))
        return tuple(out)

    init = tuple(jnp.full((1, tq), NEG_INF, F32) for _ in range(2))
    carry = lax.fori_loop(0, qi * n_diag, lambda kb, c: block(kb, c, False), init)
    for j in range(n_diag):
        carry = block(qi * n_diag + j, carry, True)
    m_ref[0, 0] = jnp.concatenate(carry, axis=0)


def _row_max(qt, k, *, chunked):
    batch, seq, wk = k.shape
    tq, tk = ATTN_TQ, ATTN_TK
    pairs = wk // (2 * LANES)
    out = pl.pallas_call(
        functools.partial(_rowmax_kernel, chunked=chunked, tq=tq, tk=tk),
        out_shape=jax.ShapeDtypeStruct((batch, pairs, 2, seq), F32),
        grid=(batch, pairs, seq // tq),
        in_specs=[pl.BlockSpec((1, 2 * LANES, tq), lambda b, h, i: (b, h, i)),
                  pl.BlockSpec((1, seq, 2 * LANES), lambda b, h, i: (b, 0, h))],
        out_specs=pl.BlockSpec((1, 1, 2, tq), lambda b, h, i: (b, h, 0, i)),
        compiler_params=_cparams(3),
        name="attn_rowmax",
    )(qt, k)
    return out


def _attn_kernel(start_ref, qt_ref, k_ref, vt_ref, shift_ref, mask_ref, o_ref, acc_ref, *, tq, tk):
    pairs = pl.num_programs(1)
    n_q = pl.num_programs(2)
    qi = pl.program_id(2)
    acc_ref[...] = jnp.zeros_like(acc_ref)

    def block(k0, nk, q_lo, masked):
        qs = slice(q_lo, tq)
        kblk = k_ref[0, pl.ds(k0, nk), :]
        vt = vt_ref[0, :, pl.ds(k0, nk)]
        ones = jnp.ones((SUM_ROWS, nk), BF16)
        for i in range(2):
            s = jnp.dot(kblk[:, i * LANES:(i + 1) * LANES], qt_ref[0, i * LANES:(i + 1) * LANES, qs],
                        preferred_element_type=F32)
            s = s - shift_ref[0, 0, i:i + 1, qs]
            if masked:
                s = s + mask_ref[q_lo:q_lo + nk, qs]
            p = jnp.exp2(s).astype(BF16)
            lhs = jnp.concatenate([vt[i * HEAD_V:(i + 1) * HEAD_V], ones], axis=0)
            acc_ref[i, :, qs] += jnp.dot(lhs, p, preferred_element_type=F32)

    def full_block(kb):
        block(pl.multiple_of(kb * tk, tk), tk, 0, False)

    first = start_ref[(pl.program_id(0) * pairs + pl.program_id(1)) * n_q + qi]
    count = qi - first

    @pl.when(count % 2 == 1)
    def _():
        full_block(first)

    def body(it, c):
        kb = first + count % 2 + 2 * it
        full_block(kb)
        full_block(kb + 1)
        return c

    lax.fori_loop(0, count // 2, body, 0)
    half = tk // 2
    q0 = pl.multiple_of(qi * tq, tq)
    block(q0, half, 0, True)
    block(pl.multiple_of(q0 + half, half), half, half, True)
    o_t = jnp.concatenate([acc_ref[i, :HEAD_V, :] / acc_ref[i, HEAD_V:HEAD_V + 1, :] for i in range(2)], axis=0)
    o_ref[0] = o_t.T.astype(o_ref.dtype)


def _diag_mask(tq, chunked):
    key = np.arange(tq)[:, None]
    qry = np.arange(tq)[None, :]
    ok = (key // CHUNK) <= (qry // CHUNK) if chunked else key <= qry
    return jnp.asarray(np.where(ok, 0.0, NEG_INF).astype(np.float32))


def _first_block(fcum, batch, seq, tq, tk):
    f = fcum.reshape(batch, seq, LANES)[:, :, :HEADS]
    f_q0 = f[:, ::tq, :]
    f_kl = f[:, tk - 1::tk, :]
    dead = (f_q0[:, :, None, :] - f_kl[:, None, :, :]) < -SKIP_LOG2
    dead = dead.reshape(batch, seq // tq, seq // tk, HEADS // 2, 2).all(axis=-1)
    lead = jnp.cumprod(dead.astype(jnp.int32), axis=2).sum(axis=2)
    limit = (jnp.arange(seq // tq, dtype=jnp.int32) * (tq // tk))[None, :, None]
    return jnp.minimum(lead, limit).transpose(0, 2, 1).reshape(-1).astype(jnp.int32)


def _attention(qt, k, vt, bound, fcum, *, chunked):
    batch, seq, wk = k.shape
    tq, tk = ATTN_TQ, ATTN_TK
    pairs = wk // (2 * LANES)
    n_q = seq // tq
    fast = bound <= SAFE_SCORE_BOUND * LOG2E
    shift = lax.cond(fast, lambda: jnp.full((batch, pairs, 2, seq), bound, F32),
                     lambda: _row_max(qt, k, chunked=chunked))
    first = jnp.zeros((batch * pairs * n_q,), jnp.int32)
    if fcum is not None:
        first = jnp.where(fast, _first_block(fcum, batch, seq, tq, tk), first)
    return pl.pallas_call(
        functools.partial(_attn_kernel, tq=tq, tk=tk),
        out_shape=jax.ShapeDtypeStruct((batch, seq, pairs * 2 * HEAD_V), BF16),
        grid_spec=pltpu.PrefetchScalarGridSpec(
            num_scalar_prefetch=1,
            grid=(batch, pairs, n_q),
            in_specs=[pl.BlockSpec((1, 2 * LANES, tq), lambda b, h, i, st: (b, h, i)),
                      pl.BlockSpec((1, seq, 2 * LANES), lambda b, h, i, st: (b, 0, h)),
                      pl.BlockSpec((1, 2 * HEAD_V, seq), lambda b, h, i, st: (b, h, 0)),
                      pl.BlockSpec((1, 1, 2, tq), lambda b, h, i, st: (b, h, 0, i)),
                      pl.BlockSpec((tq, tq), lambda b, h, i, st: (0, 0))],
            out_specs=pl.BlockSpec((1, tq, 2 * HEAD_V), lambda b, h, i, st: (b, i, h)),
            scratch_shapes=[pltpu.VMEM((2, HEAD_V + SUM_ROWS, tq), F32)]),
        compiler_params=_cparams(3),
        name="attn_chunk_causal" if chunked else "attn_frame_causal",
    )(first, qt, k, vt, shift, _diag_mask(tq, chunked))


def _post_kernel(x_ref, om_ref, of_ref, gom_ref, gof_ref, wom_ref, wof_ref, gffn_ref, wrh_ref, wrl_ref,
                 tri_ref, x1_out, h2_out, route_out, route_t_out, count_out, carry_ref):
    step = pl.program_id(0)
    sub = tri_ref.shape[0]
    parts = x_ref.shape[1] // LANES

    @pl.when(step == 0)
    def _():
        carry_ref[...] = jnp.zeros_like(carry_ref)

    running = carry_ref[:1, :]
    for sb in range(x_ref.shape[0] // sub):
        running = _post_rows(slice(sb * sub, (sb + 1) * sub), pl.ds(sb * sub * parts, sub * parts), running,
                             x_ref, om_ref, of_ref, gom_ref, gof_ref, wom_ref, wof_ref, gffn_ref, wrh_ref,
                             wrl_ref, tri_ref, x1_out, h2_out, route_out, route_t_out)
    carry_ref[...] = jnp.broadcast_to(running, carry_ref.shape)
    count_out[...] = jnp.broadcast_to(running, count_out.shape)


def _post_rows(rs, tile_rows, running, x_ref, om_ref, of_ref, gom_ref, gof_ref, wom_ref, wof_ref, gffn_ref,
               wrh_ref, wrl_ref, tri_ref, x1_out, h2_out, route_out, route_t_out):
    ym = _rms(om_ref[rs, :].astype(F32), gom_ref[...]).astype(BF16)
    yf = _rms(of_ref[rs, :].astype(F32), gof_ref[...]).astype(BF16)
    x1 = (x_ref[rs, :] + jnp.dot(ym, wom_ref[...], preferred_element_type=F32)
          + jnp.dot(yf, wof_ref[...], preferred_element_type=F32))
    x1_out[rs, :] = x1
    h2 = _rms(x1, gffn_ref[...])
    _store_row_tiles(h2_out.at[tile_rows], h2)

    h_hi = h2.astype(BF16)
    h_lo = (h2 - h_hi.astype(F32)).astype(BF16)
    w_hi = wrh_ref[...]
    logits = (jnp.dot(h_hi, w_hi, preferred_element_type=F32)
              + jnp.dot(h_lo, w_hi, preferred_element_type=F32)
              + jnp.dot(h_hi, wrl_ref[...], preferred_element_type=F32))

    lane = lax.broadcasted_iota(jnp.int32, (1, LANES), 1).astype(F32)
    big = float(LANES)

    def first_argmax(v):
        mx = jnp.max(v, axis=-1, keepdims=True)
        idx = jnp.min(jnp.where(v == mx, lane, big), axis=-1, keepdims=True)
        return mx, idx

    lg = jnp.where(lane < N_GROUPS, logits, NEG_INF)
    mg, g_idx = first_argmax(lg)
    p_g = 1.0 / jnp.sum(jnp.exp(lg - mg), axis=-1, keepdims=True)
    e_lo = ROUTE_LANE0 + EXPERTS_PER_GROUP * g_idx
    le = jnp.where((lane >= e_lo) & (lane < e_lo + EXPERTS_PER_GROUP), logits, NEG_INF)
    m1, i1 = first_argmax(le)
    m2, i2 = first_argmax(jnp.where(lane == i1, NEG_INF, le))
    e2 = jnp.exp(m2 - m1)
    gate1 = p_g / (1.0 + e2)
    gate2 = p_g * e2 / (1.0 + e2)

    oh1 = lane == i1
    oh2 = lane == i2
    oh = (jnp.where(oh1, 1.0, 0.0) + jnp.where(oh2, 1.0, 0.0))
    before = jnp.dot(tri_ref[...], oh.astype(BF16), preferred_element_type=F32)
    base = before + running
    rank1 = jnp.sum(jnp.where(oh1, base, 0.0), axis=-1, keepdims=True)
    rank2 = jnp.sum(jnp.where(oh2, base, 0.0), axis=-1, keepdims=True)

    route = jnp.where(lane == 0, i1 - ROUTE_LANE0, 0.0)
    route = jnp.where(lane == 1, i2 - ROUTE_LANE0, route)
    route = jnp.where(lane == 2, rank1, route)
    route = jnp.where(lane == 3, rank2, route)
    route = jnp.where(lane == 4, gate1, route)
    route = jnp.where(lane == 5, gate2, route)
    route_out[rs, :] = route
    route_t_out[:, rs] = route.T[:route_t_out.shape[0]]
    return running + jnp.sum(oh, axis=0, keepdims=True)


def _post_attention(x2d, o_mla, o_fox, w):
    n, d = x2d.shape
    t = POST_TILE
    parts = d // LANES
    const = lambda i: (0, 0)
    row = lambda i: (i, 0)
    full = lambda a: pl.BlockSpec(a.shape, const)
    ins = [x2d, o_mla, o_fox, w["g_out_mla"], w["g_out_fox"], w["w_out_mla"], w["w_out_fox"],
           w["g_ffn"], w["w_router_hi"], w["w_router_lo"], w["tri_strict"]]
    in_specs = [pl.BlockSpec((t, d), row), pl.BlockSpec((t, o_mla.shape[1]), row),
                pl.BlockSpec((t, o_fox.shape[1]), row)] + [full(a) for a in ins[3:]]
    return pl.pallas_call(
        _post_kernel,
        out_shape=(jax.ShapeDtypeStruct((n, d), F32), jax.ShapeDtypeStruct((n * parts, LANES), F32),
                   jax.ShapeDtypeStruct((n, LANES), F32), jax.ShapeDtypeStruct((8, n), F32),
                   jax.ShapeDtypeStruct((8, LANES), F32)),
        grid=(n // t,),
        in_specs=in_specs,
        out_specs=(pl.BlockSpec((t, d), row), pl.BlockSpec((t * parts, LANES), row),
                   pl.BlockSpec((t, LANES), row), pl.BlockSpec((8, t), lambda i: (0, i)),
                   pl.BlockSpec((8, LANES), const)),
        scratch_shapes=[pltpu.VMEM((8, LANES), F32)],
        compiler_params=_cparams(1),
        name="post_attention",
    )(*ins)


def _tile_rows(dest, tile, parts):
    return (dest * parts).reshape(2, -1, tile).transpose(1, 0, 2)


def _dispatch_kernel(pad_end_ref, padded_ref, n_used_ref, dest_ref, h_ref, xs_hbm, zero_ref, sem, zsem, *,
                     tile, parts):
    zrows = zero_ref.shape[0]
    n_blk = xs_hbm.shape[0] // zrows
    rblk = zrows // parts

    def zero_copy(blk):
        return pltpu.make_async_copy(zero_ref, xs_hbm.at[pl.ds(pl.multiple_of(blk * zrows, zrows), zrows)], zsem)

    @pl.when(pl.program_id(0) == 0)
    def _():
        zero_ref[...] = jnp.zeros_like(zero_ref)
        for wait in (False, True):
            for e in range(N_EXPERTS):
                @pl.when(padded_ref[e] > 0)
                def _():
                    cp = zero_copy(pad_end_ref[e] // rblk - 1)
                    cp.wait() if wait else cp.start()

            def tail(blk, c):
                cp = zero_copy(blk)
                cp.wait() if wait else cp.start()
                return c

            lax.fori_loop(n_used_ref[0], n_blk, tail, 0)

    def issue(jo, c):
        for ji in range(ISSUE_UNROLL):
            j = jo * ISSUE_UNROLL + ji
            src = h_ref.at[pl.ds(pl.multiple_of(j * parts, parts), parts)]
            for k in range(2):
                d = pl.multiple_of(dest_ref[0, k, j], parts)
                pltpu.make_async_copy(src, xs_hbm.at[pl.ds(d, parts)], sem).start(priority=k)
        return c

    lax.fori_loop(0, tile // ISSUE_UNROLL, issue, 0)
    for _ in range(2):
        pltpu.make_async_copy(h_ref, xs_hbm.at[pl.ds(0, tile * parts)], sem).wait()


def _dispatch(h2t, plan, parts):
    n = h2t.shape[0] // parts
    tile = DISPATCH_TILE
    dest3 = _tile_rows(plan["dest"], tile, parts)
    return pl.pallas_call(
        functools.partial(_dispatch_kernel, tile=tile, parts=parts),
        out_shape=jax.ShapeDtypeStruct((plan["n_rows"] * parts, LANES), h2t.dtype),
        grid_spec=pltpu.PrefetchScalarGridSpec(
            num_scalar_prefetch=3,
            grid=(n // tile,),
            in_specs=[pl.BlockSpec((1, 2, tile), lambda i, *_: (i, 0, 0), memory_space=pltpu.SMEM),
                      pl.BlockSpec((tile * parts, LANES), lambda i, *_: (i, 0))],
            out_specs=pl.BlockSpec(memory_space=pl.ANY),
            scratch_shapes=[pltpu.VMEM((EXPERT_BLOCK * parts, LANES), h2t.dtype), pltpu.SemaphoreType.DMA,
                            pltpu.SemaphoreType.DMA]),
        compiler_params=_cparams(1),
        name="moe_dispatch",
    )(plan["pad_end"], plan["padded"], plan["n_used"], dest3, h2t)


def _expert_kernel(blk_e_ref, n_used_ref, x_ref, wg_ref, wu_ref, wdn_ref, y_ref, wgu_bf, wd_bf, *, parts):
    i = pl.program_id(0)
    used = i < n_used_ref[0]
    rows = x_ref.shape[0] // parts
    fresh = jnp.logical_or(i == 0, blk_e_ref[i] != blk_e_ref[jnp.maximum(i - 1, 0)])

    @pl.when(jnp.logical_and(used, fresh))
    def _():
        wgu_bf[:, :EXPERT_FF] = wg_ref[0, 0].astype(BF16)
        wgu_bf[:, EXPERT_FF:] = wu_ref[0, 0].astype(BF16)
        wd_bf[...] = wdn_ref[0, 0].astype(BF16)

    @pl.when(used)
    def _():
        sub = EXPERT_SUB
        for sb in range(rows // sub):
            tiles = pl.ds(sb * sub * parts, sub * parts)
            xb = _load_row_tiles(x_ref.at[tiles], sub, parts).astype(BF16)
            gu = jnp.dot(xb, wgu_bf[...], preferred_element_type=F32)
            g = gu[:, :EXPERT_FF]
            act = (g / (1.0 + jnp.exp(-g))) * gu[:, EXPERT_FF:]
            _store_row_tiles(y_ref.at[tiles], jnp.dot(act.astype(BF16), wd_bf[...], preferred_element_type=F32))

    @pl.when(jnp.logical_not(used))
    def _():
        y_ref[...] = jnp.zeros_like(y_ref)


def _experts(xs, plan, layer, w_gate, w_up, w_down, parts):
    r = EXPERT_BLOCK
    d = parts * LANES
    rows = lambda i, be, nu: (jnp.minimum(i, nu[0] - 1), 0)
    wsel = lambda i, be, nu: (layer, be[i], 0, 0)
    return pl.pallas_call(
        functools.partial(_expert_kernel, parts=parts),
        out_shape=jax.ShapeDtypeStruct(xs.shape, F32),
        grid_spec=pltpu.PrefetchScalarGridSpec(
            num_scalar_prefetch=2,
            grid=(xs.shape[0] // (r * parts),),
            in_specs=[pl.BlockSpec((r * parts, LANES), rows),
                      pl.BlockSpec((1, 1, d, EXPERT_FF), wsel),
                      pl.BlockSpec((1, 1, d, EXPERT_FF), wsel),
                      pl.BlockSpec((1, 1, EXPERT_FF, d), wsel)],
            out_specs=pl.BlockSpec((r * parts, LANES), lambda i, be, nu: (i, 0)),
            scratch_shapes=[pltpu.VMEM((d, 2 * EXPERT_FF), BF16), pltpu.VMEM((EXPERT_FF, d), BF16)]),
        compiler_params=_cparams(1),
        name="moe_experts",
    )(plan["blk_e"], plan["n_used"], xs, w_gate, w_up, w_down)


def _ple_kernel(dest_ref, dest_next_ref, x1_ref, route_ref, p_ref, gnorm_ref, wgate_ref, wproj_ref, gout_ref,
                ys_hbm, x_out, rows_a, rows_b, sem, *, tile, parts):
    step = pl.program_id(0)
    n_steps = pl.num_programs(0)

    def row_copy(d, buf, slot, k, j):
        return pltpu.make_async_copy(ys_hbm.at[pl.ds(pl.multiple_of(d, parts), parts)],
                                     buf.at[k, pl.ds(j * parts, parts)], sem.at[slot])

    def wait_rows(buf, slot):
        for k in range(2):
            pltpu.make_async_copy(ys_hbm.at[pl.ds(0, tile * parts)], buf.at[k], sem.at[slot]).wait()

    @pl.when(step == 0)
    def _():
        def issue(jo, c):
            for ji in range(ISSUE_UNROLL):
                j = jo * ISSUE_UNROLL + ji
                for k in range(2):
                    pltpu.make_async_copy(
                        ys_hbm.at[pl.ds(pl.multiple_of(dest_ref[0, k, j], parts), parts)],
                        rows_a.at[k, pl.ds(pl.multiple_of(j * parts, parts), parts)], sem.at[0]).start(priority=k)
            return c
        lax.fori_loop(0, tile // ISSUE_UNROLL, issue, 0)

    def body(cur, nxt, slot):
        wait_rows(cur, slot)
        for j in range(tile):
            for k in range(2):
                row_copy(dest_next_ref[0, k, j], nxt, 1 - slot, k, j).start(priority=k)
        route = route_ref[...]
        y1 = _load_row_tiles(cur.at[0], tile, parts)
        y2 = _load_row_tiles(cur.at[1], tile, parts)
        x2 = x1_ref[...] + route[:, 4:5] * y1 + route[:, 5:6] * y2
        ple = _rms(jnp.dot(p_ref[0].astype(BF16), wproj_ref[...], preferred_element_type=F32), gout_ref[...])
        z = jnp.dot(_rms(x2, gnorm_ref[...]).astype(BF16), wgate_ref[...], preferred_element_type=F32)
        x_out[...] = x2 + ple / (1.0 + jnp.exp(-z))

        @pl.when(step == n_steps - 1)
        def _():
            wait_rows(nxt, 1 - slot)

    @pl.when(step % 2 == 0)
    def _():
        body(rows_a, rows_b, 0)

    @pl.when(step % 2 == 1)
    def _():
        body(rows_b, rows_a, 1)


def _combine_ple(x1, route, dest, ys, p3d, layer, w):
    n, d = x1.shape
    t = PLE_TILE
    parts = d // LANES
    steps = n // t
    const = lambda i: (0, 0)
    row = lambda i: (i, 0)
    full = lambda a: pl.BlockSpec(a.shape, const)
    dest3 = _tile_rows(dest, t, parts)
    return pl.pallas_call(
        functools.partial(_ple_kernel, tile=t, parts=parts),
        out_shape=jax.ShapeDtypeStruct((n, d), F32),
        grid=(steps,),
        in_specs=[pl.BlockSpec((1, 2, t), lambda i: (i, 0, 0), memory_space=pltpu.SMEM),
                  pl.BlockSpec((1, 2, t), lambda i: (jnp.minimum(i + 1, steps - 1), 0, 0),
                               memory_space=pltpu.SMEM),
                  pl.BlockSpec((t, d), row), pl.BlockSpec((t, LANES), row),
                  pl.BlockSpec((1, t, p3d.shape[2]), lambda i: (layer, i, 0)),
                  full(w["g_ple_norm"]), full(w["w_ple_gate"]), full(w["w_ple_proj"]), full(w["g_ple_out"]),
                  pl.BlockSpec(memory_space=pl.ANY)],
        out_specs=pl.BlockSpec((t, d), row),
        scratch_shapes=[pltpu.VMEM((2, t * parts, LANES), F32), pltpu.VMEM((2, t * parts, LANES), F32),
                        pltpu.SemaphoreType.DMA((2,))],
        compiler_params=_cparams(1),
        name="moe_combine_ple",
    )(dest3, dest3, x1, route, p3d, w["g_ple_norm"], w["w_ple_gate"], w["w_ple_proj"], w["g_ple_out"], ys)


def _row(v):
    return v.reshape(1, -1).astype(F32)


def _col_rep(v):
    return jnp.broadcast_to(v.astype(F32)[:, None], (v.shape[0], LANES))


def _pad_heads(wmat, real):
    k = wmat.shape[0]
    return jnp.pad(wmat.reshape(k, HEADS, real), ((0, 0), (0, 0), (0, LANES - real))).reshape(k, HEADS * LANES)


def _placement():
    place_k = np.zeros((LANES, HEADS * LANES), np.float32)
    place_q = np.zeros((HEADS * LANES, LANES), np.float32)
    for hd in range(HEADS):
        base = hd * LANES
        for j in range(3):
            src = j * PIECE_STRIDE + hd
            place_k[src, base + AUG_K + j] = -1.0
            place_q[base + AUG_Q + j, src] = 1.0
            place_k[ONE_LANE, base + AUG_Q + j] = 1.0
            place_q[base + AUG_K + j, ONE_LANE] = 1.0
    return jnp.asarray(place_k, BF16), jnp.asarray(place_q, BF16)


def _layer_weights(i, g_attn_norm, w_in, g_q_lora, w_uq, g_kv_lora, w_ukv, g_mla_q, g_mla_k, g_fox_q,
                   g_fox_k, b_fox_f, g_out_mla, g_out_fox, w_out, g_ffn_norm, w_router_group,
                   w_router_expert, w_exp_gate, w_exp_up, w_exp_down, g_ple_norm, w_ple_gate, w_ple_proj,
                   g_ple_out):
    d = w_in.shape[1]
    wf = HEADS * FOX_DIM
    c_kv = Q_LORA
    c_pe = c_kv + KV_LORA
    c_fq = c_pe + MLA_ROPE
    c_fk = c_fq + wf
    c_fv = c_fk + wf
    c_fl = c_fv + wf
    win = w_in[i]
    pad_pe = jnp.zeros((d, LANES), F32).at[:, MLA_NOPE:MLA_QK].set(win[:, c_pe:c_fq])
    group = lambda v: jnp.concatenate(
        [jnp.pad(v, [(0, 0)] * (v.ndim - 1) + [(0, PIECE_STRIDE - HEADS)])] * 3
        + [jnp.zeros(v.shape[:-1] + (LANES - 3 * PIECE_STRIDE,), v.dtype)], axis=-1)
    pad_fl = group(win[:, c_fl:])
    w = {}
    w["g_attn"] = _row(g_attn_norm[i])
    w["w_a"] = jnp.concatenate([win[:, :c_pe], pad_pe, pad_fl], axis=1).astype(BF16)
    w["w_fk"] = win[:, c_fk:c_fv].astype(BF16)
    w["w_fq_t"] = win[:, c_fq:c_fk].T.astype(BF16)
    w["w_fv_t"] = win[:, c_fv:c_fl].T.astype(BF16)
    w["b_fl"] = group(b_fox_f[i].astype(F32)[None, :])
    w["place_k"], w["place_q"] = _placement()
    w["g_q_lora"] = _row(g_q_lora[i])
    w["w_uq_t"] = w_uq[i].T.astype(BF16)
    w["g_kv_lora"] = _row(g_kv_lora[i])
    ukv = w_ukv[i].reshape(KV_LORA, HEADS, MLA_NOPE + HEAD_V)
    w["w_uk"] = ukv[:, :, :MLA_NOPE].reshape(KV_LORA, -1).astype(BF16)
    w["w_uv_t"] = ukv[:, :, MLA_NOPE:].reshape(KV_LORA, -1).T.astype(BF16)
    w["g_mla_q"] = _col_rep(jnp.pad(g_mla_q[i], (0, LANES - MLA_QK))) * (MLA_QK ** -0.5 * LOG2E)
    w["g_mla_k"] = _row(jnp.pad(g_mla_k[i], (0, LANES - MLA_QK)))
    w["g_fox_q"] = _col_rep(jnp.pad(g_fox_q[i], (0, LANES - FOX_DIM))) * (FOX_DIM ** -0.5 * LOG2E)
    w["g_fox_k"] = _row(jnp.pad(g_fox_k[i], (0, LANES - FOX_DIM)))
    bound = lambda gq, gk, dim: (1.02 * LOG2E * dim ** 0.5) * jnp.max(jnp.abs(gq)) * jnp.max(jnp.abs(gk))
    w["bound_mla"] = bound(g_mla_q[i], g_mla_k[i], MLA_QK).astype(F32)
    w["bound_fox"] = bound(g_fox_q[i], g_fox_k[i], FOX_DIM).astype(F32)
    w["g_out_mla"] = _row(g_out_mla[i])
    w["g_out_fox"] = _row(g_out_fox[i])
    wm = HEADS * HEAD_V
    w["w_out_mla"] = w_out[i, :wm].astype(BF16)
    w["w_out_fox"] = w_out[i, wm:].astype(BF16)
    w["g_ffn"] = _row(g_ffn_norm[i])
    wr = jnp.zeros((d, LANES), F32)
    wr = wr.at[:, :N_GROUPS].set(w_router_group[i]).at[:, ROUTE_LANE0:ROUTE_LANE0 + N_EXPERTS].set(
        w_router_expert[i])
    w["w_router_hi"] = wr.astype(BF16)
    w["w_router_lo"] = (wr - w["w_router_hi"].astype(F32)).astype(BF16)
    w["g_ple_norm"] = _row(g_ple_norm[i])
    w["w_ple_gate"] = w_ple_gate[i].astype(BF16)
    w["w_ple_proj"] = w_ple_proj[i].astype(BF16)
    w["g_ple_out"] = _row(g_ple_out[i])
    def tri(t, strict):
        r = lax.broadcasted_iota(jnp.int32, (t, t), 0)
        c = lax.broadcasted_iota(jnp.int32, (t, t), 1)
        return ((c < r) if strict else (c <= r)).astype(BF16)

    w["tri_incl"] = tri(PRE_SUB, False)
    w["tri_strict"] = tri(POST_SUB, True)
    return w


def _route_plan(route_t, counts, n):
    r = EXPERT_BLOCK
    cnt = counts[0, ROUTE_LANE0:ROUTE_LANE0 + N_EXPERTS].astype(jnp.int32)
    padded = ((cnt + r - 1) // r) * r
    pad_end = jnp.cumsum(padded)
    pad_start = pad_end - padded
    e = route_t[0:2].astype(jnp.int32)
    rank = route_t[2:4].astype(jnp.int32)
    onehot = e[:, None, :] == jnp.arange(N_EXPERTS, dtype=jnp.int32)[None, :, None]
    dest = jnp.sum(jnp.where(onehot, pad_start[None, :, None], 0), axis=1) + rank
    n_rows = 2 * n + N_EXPERTS * r
    blk_start = jnp.arange(n_rows // r, dtype=jnp.int32) * r
    blk_e = jnp.sum((blk_start[:, None] >= pad_end[None, :]).astype(jnp.int32), axis=1)
    blk_e = jnp.minimum(blk_e, N_EXPERTS - 1)
    n_used = (pad_end[-1:] // r).astype(jnp.int32)
    return {"dest": dest, "blk_e": blk_e, "n_used": n_used, "n_rows": n_rows,
            "pad_end": pad_end.astype(jnp.int32), "padded": padded.astype(jnp.int32)}


def kernel(x, p, positions, g_attn_norm, w_in, g_q_lora, w_uq, g_kv_lora, w_ukv, g_mla_q, g_mla_k, g_fox_q,
           g_fox_k, b_fox_f, g_out_mla, g_out_fox, w_out, g_ffn_norm, w_router_group, w_router_expert,
           w_exp_gate, w_exp_up, w_exp_down, g_ple_norm, w_ple_gate, w_ple_proj, g_ple_out):
    batch, seq, d = x.shape
    n = batch * seq
    depth = w_in.shape[0]
    params = (g_attn_norm, w_in, g_q_lora, w_uq, g_kv_lora, w_ukv, g_mla_q, g_mla_k, g_fox_q, g_fox_k,
              b_fox_f, g_out_mla, g_out_fox, w_out, g_ffn_norm, w_router_group, w_router_expert,
              w_exp_gate, w_exp_up, w_exp_down, g_ple_norm, w_ple_gate, w_ple_proj, g_ple_out)
    tables = _rope_tables(positions)
    xc = x.reshape(n, d)
    for i in range(depth):
        w = _layer_weights(i, *params)
        qt, k, vt, fqt, fk, fvt, fcum = _pre_attention(xc, tables, w, seq)
        b3 = lambda a: a.reshape(batch, seq, a.shape[-1])
        o_mla = _attention(qt, b3(k), vt, w["bound_mla"], None, chunked=True).reshape(n, -1)
        o_fox = _attention(fqt, b3(fk), fvt, w["bound_fox"], fcum, chunked=False).reshape(n, -1)
        x1, h2, route, route_t, counts = _post_attention(xc, o_mla, o_fox, w)
        plan = _route_plan(route_t, counts, n)
        parts = d // LANES
        xs = _dispatch(h2, plan, parts)
        ys = _experts(xs, plan, i, w_exp_gate, w_exp_up, w_exp_down, parts)
        xc = _combine_ple(x1, route, plan["dest"], ys, p.reshape(depth, n, -1), i, w)
    return xc.reshape(batch, seq, d)
```

```python
import functools
import math

import numpy as np
import jax
import jax.numpy as jnp
from jax import lax
from jax.experimental import pallas as pl
from jax.experimental.pallas import tpu as pltpu

F32 = jnp.float32
BF16 = jnp.bfloat16

EPS = 1e-6
NEG_INF = -1e30
ROPE_THETA = 10000.0
LOG2E = math.log2(math.e)

LANES = 128
CHUNK = 64
HEADS = 8
MLA_NOPE = 64
MLA_ROPE = 32
MLA_QK = MLA_NOPE + MLA_ROPE
HEAD_V = 64
FOX_DIM = 64
Q_LORA = 256
KV_LORA = 128
N_GROUPS = 4
EXPERTS_PER_GROUP = 8
N_EXPERTS = N_GROUPS * EXPERTS_PER_GROUP
EXPERT_FF = 256
ROUTE_LANE0 = N_GROUPS

AUG_Q = FOX_DIM
AUG_K = FOX_DIM + 3
PIECE_STRIDE = 16
ONE_LANE = 3 * PIECE_STRIDE

TOKEN_TILE = 512
PRE_SUB = 512
POST_TILE = 1024
POST_SUB = 256
ATTN_TQ = 1024
ATTN_TK = 1024
ATTN_Q_BLOCKS = 2
SUM_ROWS = 16
EXPERT_BLOCK = 512
EXPERT_SUB = 256
DISPATCH_TILE = 1024
PLE_TILE = 256
ISSUE_UNROLL = 8
VMEM_LIMIT = 56 * 1024 * 1024
SAFE_SCORE_BOUND = 40.0
SKIP_LOG2 = 160.0

_NT = (((1,), (1,)), ((), ()))


def _cparams(n_axes):
    return pltpu.CompilerParams(dimension_semantics=("arbitrary",) * n_axes,
                                vmem_limit_bytes=VMEM_LIMIT)


def _rms(x, g):
    return x * lax.rsqrt(jnp.mean(x * x, axis=-1, keepdims=True) + EPS) * g


def _split3(x):
    hi = x.astype(BF16)
    r1 = x - hi.astype(F32)
    mid = r1.astype(BF16)
    lo = (r1 - mid.astype(F32)).astype(BF16)
    return hi, mid, lo


def _store_row_tiles(ref, v):
    parts = v.shape[1] // LANES
    for s in range(parts):
        ref[pl.ds(s, v.shape[0], stride=parts), :] = v[:, s * LANES:(s + 1) * LANES]


def _load_row_tiles(ref, rows, parts):
    return jnp.concatenate([ref[pl.ds(s, rows, stride=parts), :] for s in range(parts)], axis=1)


def _lane_tile(a, width):
    return jnp.tile(a, (1, width // LANES))


def _rope_kernel(pos_row_ref, invf_rep_ref, cos_ref, sin_ref, cost_ref, sint_ref):
    t = pos_row_ref.shape[-1]
    half = invf_rep_ref.shape[0]
    ang_t = _lane_tile(invf_rep_ref[...], t) * pos_row_ref[0]
    c = jnp.cos(ang_t)
    s = jnp.sin(ang_t)
    cost_ref[...] = c
    sint_ref[...] = s
    ones = jnp.ones((MLA_NOPE, t), F32)
    zeros = jnp.zeros((MLA_NOPE, t), F32)
    pad = LANES - MLA_NOPE - 2 * half
    cos_ref[...] = jnp.concatenate([ones, c, c, ones[:pad]], axis=0).T
    sin_ref[...] = jnp.concatenate([zeros, -s, s, zeros[:pad]], axis=0).T


def _rope_tables(positions):
    n = positions.size
    half = MLA_ROPE // 2
    inv_freq = ROPE_THETA ** (-np.arange(0, MLA_ROPE, 2, dtype=np.float32) / MLA_ROPE)
    invf_rep = np.broadcast_to(inv_freq[:, None], (half, LANES)).astype(np.float32)
    pos = positions.astype(F32)
    t = TOKEN_TILE
    return pl.pallas_call(
        _rope_kernel,
        out_shape=(jax.ShapeDtypeStruct((n, LANES), F32), jax.ShapeDtypeStruct((n, LANES), F32),
                   jax.ShapeDtypeStruct((half, n), F32), jax.ShapeDtypeStruct((half, n), F32)),
        grid=(n // t,),
        in_specs=[pl.BlockSpec((1, 1, t), lambda i: (i, 0, 0)),
                  pl.BlockSpec((half, LANES), lambda i: (0, 0))],
        out_specs=(pl.BlockSpec((t, LANES), lambda i: (i, 0)), pl.BlockSpec((t, LANES), lambda i: (i, 0)),
                   pl.BlockSpec((half, t), lambda i: (0, i)), pl.BlockSpec((half, t), lambda i: (0, i))),
        compiler_params=_cparams(1),
        name="rope_tables",
    )(pos.reshape(n // t, 1, t), jnp.asarray(invf_rep))


def _pre_kernel(x_ref, cos_ref, sin_ref, cost_ref, sint_ref, gattn_ref, wa_ref, wfk_ref, wfqt_ref, wfvt_ref,
                bfl_ref, tri_ref, plk_ref, plq_ref, gql_ref, wuqt_ref, gkvl_ref, wuk_ref, wuvt_ref,
                gq_ref, gk_ref, gfq_ref, gfk_ref,
                qt_out, k_out, vt_out, fqt_out, fk_out, fvt_out, fcum_out, carry_ref, *, tiles_per_seq):
    @pl.when(pl.program_id(0) % tiles_per_seq == 0)
    def _():
        carry_ref[...] = jnp.zeros_like(carry_ref)

    sub = tri_ref.shape[0]
    running = carry_ref[:1, :]
    for sb in range(x_ref.shape[0] // sub):
        running = _pre_rows(slice(sb * sub, (sb + 1) * sub), running, x_ref, cos_ref, sin_ref, cost_ref, sint_ref,
                            gattn_ref, wa_ref, wfk_ref, wfqt_ref, wfvt_ref, bfl_ref, tri_ref, plk_ref, plq_ref,
                            gql_ref, wuqt_ref, gkvl_ref, wuk_ref, wuvt_ref, gq_ref, gk_ref, gfq_ref, gfk_ref,
                            qt_out, k_out, vt_out, fqt_out, fk_out, fvt_out, fcum_out)
    carry_ref[...] = jnp.broadcast_to(running, carry_ref.shape)


def _pre_rows(rs, running, x_ref, cos_ref, sin_ref, cost_ref, sint_ref, gattn_ref, wa_ref, wfk_ref, wfqt_ref,
              wfvt_ref, bfl_ref, tri_ref, plk_ref, plq_ref, gql_ref, wuqt_ref, gkvl_ref, wuk_ref, wuvt_ref,
              gq_ref, gk_ref, gfq_ref, gfk_ref, qt_out, k_out, vt_out, fqt_out, fk_out, fvt_out, fcum_out):
    x = x_ref[rs, :]
    t = x.shape[0]
    h = _rms(x, gattn_ref[...]).astype(BF16)
    pa = jnp.dot(h, wa_ref[...], preferred_element_type=F32)
    fk = jnp.dot(h, wfk_ref[...], preferred_element_type=F32)
    fqt = lax.dot_general(wfqt_ref[...], h, _NT, preferred_element_type=F32)
    fvt_out[0, :, rs] = lax.dot_general(wfvt_ref[...], h, _NT, preferred_element_type=F32).astype(BF16)
    half = MLA_ROPE // 2
    lane = lax.broadcasted_iota(jnp.int32, (1, LANES), 1)
    low = lane < FOX_DIM

    def head_lanes(m, hd):
        v = m[:, (hd // 2) * LANES:(hd // 2 + 1) * LANES]
        if hd % 2:
            v = pltpu.roll(v, LANES // 2, 1)
        return jnp.where(low, v, 0.0)

    qn = _rms(pa[:, :Q_LORA], gql_ref[...]).astype(BF16)
    kvn = _rms(pa[:, Q_LORA:Q_LORA + KV_LORA], gkvl_ref[...]).astype(BF16)
    vt_out[0, :, rs] = lax.dot_general(wuvt_ref[...], kvn, _NT, preferred_element_type=F32).astype(BF16)

    kn = jnp.dot(kvn, wuk_ref[...], preferred_element_type=F32)
    kpe = pa[:, Q_LORA + KV_LORA:Q_LORA + KV_LORA + LANES]
    cos_l = cos_ref[rs, :]
    sin_l = sin_ref[rs, :]
    gk = gk_ref[...]
    for hd in range(HEADS):
        v = head_lanes(kn, hd) + kpe
        v = v * lax.rsqrt(jnp.sum(v * v, axis=-1, keepdims=True) * (1.0 / MLA_QK) + EPS) * gk
        swapped = jnp.where(lane < MLA_NOPE + half, pltpu.roll(v, LANES - half, 1), pltpu.roll(v, half, 1))
        k_out[rs, hd * LANES:(hd + 1) * LANES] = (v * cos_l + swapped * sin_l).astype(BF16)

    qt = lax.dot_general(wuqt_ref[...], qn, _NT, preferred_element_type=F32)
    cos_r = cost_ref[:, rs]
    sin_r = sint_ref[:, rs]
    gq = _lane_tile(gq_ref[...], t)[:MLA_QK]
    pad_q = jnp.zeros((LANES - MLA_QK, t), BF16)
    for hd in range(HEADS):
        blk = qt[hd * MLA_QK:(hd + 1) * MLA_QK]
        r = lax.rsqrt(jnp.sum(blk * blk, axis=0, keepdims=True) * (1.0 / MLA_QK) + EPS)
        blk = blk * r * gq
        x1, x2 = blk[MLA_NOPE:MLA_NOPE + half], blk[MLA_NOPE + half:]
        blk = jnp.concatenate([blk[:MLA_NOPE], x1 * cos_r - x2 * sin_r, x2 * cos_r + x1 * sin_r], axis=0)
        qt_out[0, hd * LANES:(hd + 1) * LANES, rs] = jnp.concatenate([blk.astype(BF16), pad_q], axis=0)


    def by_group(a, b, c):
        return jnp.where(lane < PIECE_STRIDE, a, jnp.where(lane < 2 * PIECE_STRIDE, b, c))

    z = pa[:, Q_LORA + KV_LORA + LANES:] + bfl_ref[...]
    logf = jnp.minimum(z, 0.0) - jnp.log1p(jnp.exp(-jnp.abs(z)))
    parts = jnp.dot(tri_ref[...], by_group(*_split3(logf)), preferred_element_type=F32)
    tot = parts + pltpu.roll(parts, PIECE_STRIDE, 1) + pltpu.roll(parts, 2 * PIECE_STRIDE, 1)
    cum = by_group(pltpu.roll(tot, LANES - 2 * PIECE_STRIDE, 1), pltpu.roll(tot, LANES - PIECE_STRIDE, 1), tot)
    cum = cum + running
    cum2 = cum * LOG2E
    fcum_out[rs, :] = cum2
    packed = by_group(*_split3(cum2))
    packed = jnp.where(lane == ONE_LANE, jnp.ones_like(packed), packed)
    aug_k = jnp.dot(packed, plk_ref[...], preferred_element_type=F32)
    aug_q = lax.dot_general(plq_ref[...], packed, _NT, preferred_element_type=F32)

    gfk = gfk_ref[...]
    for hd in range(HEADS):
        sl = slice(hd * LANES, (hd + 1) * LANES)
        v = head_lanes(fk, hd)
        v = v * lax.rsqrt(jnp.sum(v * v, axis=-1, keepdims=True) * (1.0 / FOX_DIM) + EPS) * gfk
        fk_out[rs, sl] = (v + aug_k[:, sl]).astype(BF16)

    gfq = _lane_tile(gfq_ref[...], t)[:FOX_DIM]
    for hd in range(HEADS):
        blk = fqt[hd * FOX_DIM:(hd + 1) * FOX_DIM]
        r = lax.rsqrt(jnp.sum(blk * blk, axis=0, keepdims=True) * (1.0 / FOX_DIM) + EPS)
        aug = aug_q[hd * LANES + FOX_DIM:(hd + 1) * LANES]
        fqt_out[0, hd * LANES:(hd + 1) * LANES, rs] = jnp.concatenate([blk * r * gfq, aug], axis=0).astype(BF16)
    return cum[t - 1:t, :]


def _pre_attention(x2d, tables, w, seq):
    n, d = x2d.shape
    t = TOKEN_TILE
    tiles_per_seq = seq // t
    batch = n // seq
    cos_l, sin_l, cos_t, sin_t = tables
    half = MLA_ROPE // 2
    row = lambda i: (i, 0)
    seq_t = lambda i: (i // tiles_per_seq, 0, i % tiles_per_seq)

    def full(a):
        return pl.BlockSpec(a.shape, lambda i, nd=a.ndim: (0,) * nd)

    weights = [w["g_attn"], w["w_a"], w["w_fk"], w["w_fq_t"], w["w_fv_t"], w["b_fl"], w["tri_incl"],
               w["place_k"], w["place_q"], w["g_q_lora"], w["w_uq_t"], w["g_kv_lora"], w["w_uk"], w["w_uv_t"],
               w["g_mla_q"], w["g_mla_k"], w["g_fox_q"], w["g_fox_k"]]
    in_specs = [pl.BlockSpec((t, d), row), pl.BlockSpec((t, LANES), row), pl.BlockSpec((t, LANES), row),
                pl.BlockSpec((half, t), lambda i: (0, i)), pl.BlockSpec((half, t), lambda i: (0, i))]
    in_specs += [full(a) for a in weights]
    wq = HEADS * LANES
    wv = HEADS * HEAD_V
    out_shape = (jax.ShapeDtypeStruct((batch, wq, seq), BF16), jax.ShapeDtypeStruct((n, wq), BF16),
                 jax.ShapeDtypeStruct((batch, wv, seq), BF16), jax.ShapeDtypeStruct((batch, wq, seq), BF16),
                 jax.ShapeDtypeStruct((n, wq), BF16), jax.ShapeDtypeStruct((batch, wv, seq), BF16),
                 jax.ShapeDtypeStruct((n, LANES), F32))
    out_specs = (pl.BlockSpec((1, wq, t), seq_t), pl.BlockSpec((t, wq), row), pl.BlockSpec((1, wv, t), seq_t),
                 pl.BlockSpec((1, wq, t), seq_t), pl.BlockSpec((t, wq), row), pl.BlockSpec((1, wv, t), seq_t),
                 pl.BlockSpec((t, LANES), row))
    return pl.pallas_call(
        functools.partial(_pre_kernel, tiles_per_seq=tiles_per_seq),
        out_shape=out_shape,
        grid=(n // t,),
        in_specs=in_specs,
        out_specs=out_specs,
        scratch_shapes=[pltpu.VMEM((8, LANES), F32)],
        compiler_params=_cparams(1),
        name="pre_attention",
    )(x2d, cos_l, sin_l, cos_t, sin_t, *weights)


def _allowed(k0, q0, tk, tq, chunked):
    key = k0 + lax.broadcasted_iota(jnp.int32, (tk, tq), 0)
    qry = q0 + lax.broadcasted_iota(jnp.int32, (tk, tq), 1)
    if chunked:
        return (key // CHUNK) <= (qry // CHUNK)
    return key <= qry


def _rowmax_kernel(qt_ref, k_ref, m_ref, *, chunked, tq, tk):
    qi = pl.program_id(2)
    n_diag = tq // tk

    def block(kb, carry, masked):
        k0 = pl.multiple_of(kb * tk, tk)
        kblk = k_ref[0, pl.ds(k0, tk), :]
        out = []
        for i in range(2):
            s = jnp.dot(kblk[:, i * LANES:(i + 1) * LANES], qt_ref[0, i * LANES:(i + 1) * LANES, :],
                        preferred_element_type=F32)
            if masked:
                s = jnp.where(_allowed(k0, qi * tq, tk, tq, chunked), s, NEG_INF)
            out.append(jnp.maximum(carry[i], jnp.max(s, axis=0, keepdims=True)))
        return tuple(out)

    init = tuple(jnp.full((1, tq), NEG_INF, F32) for _ in range(2))
    carry = lax.fori_loop(0, qi * n_diag, lambda kb, c: block(kb, c, False), init)
    for j in range(n_diag):
        carry = block(qi * n_diag + j, carry, True)
    m_ref[0, 0] = jnp.concatenate(carry, axis=0)


def _row_max(qt, k, *, chunked):
    batch, seq, wk = k.shape
    tq, tk = ATTN_TQ, ATTN_TK
    pairs = wk // (2 * LANES)
    out = pl.pallas_call(
        functools.partial(_rowmax_kernel, chunked=chunked, tq=tq, tk=tk),
        out_shape=jax.ShapeDtypeStruct((batch, pairs, 2, seq), F32),
        grid=(batch, pairs, seq // tq),
        in_specs=[pl.BlockSpec((1, 2 * LANES, tq), lambda b, h, i: (b, h, i)),
                  pl.BlockSpec((1, seq, 2 * LANES), lambda b, h, i: (b, 0, h))],
        out_specs=pl.BlockSpec((1, 1, 2, tq), lambda b, h, i: (b, h, 0, i)),
        compiler_params=_cparams(3),
        name="attn_rowmax",
    )(qt, k)
    return out


def _attn_kernel(start_ref, qt_ref, k_ref, vt_ref, shift_ref, mask_ref, o_ref, acc_ref, *, tq, tk):
    q_blocks = qt_ref.shape[-1] // tq
    for sub in range(q_blocks):
        _attn_q_block(pl.program_id(2) * q_blocks + sub, sub * tq, pl.num_programs(2) * q_blocks,
                      start_ref, qt_ref, k_ref, vt_ref, shift_ref, mask_ref, o_ref, acc_ref, tq=tq, tk=tk)


def _attn_q_block(qi, lo, n_q, start_ref, qt_ref, k_ref, vt_ref, shift_ref, mask_ref, o_ref, acc_ref, *, tq, tk):
    pairs = pl.num_programs(1)
    acc_ref[...] = jnp.zeros_like(acc_ref)

    def block(k0, nk, q_lo, masked):
        qs = slice(q_lo, tq)
        qg = slice(lo + q_lo, lo + tq)
        kblk = k_ref[0, pl.ds(k0, nk), :]
        vt = vt_ref[0, :, pl.ds(k0, nk)]
        ones = jnp.ones((SUM_ROWS, nk), BF16)
        for i in range(2):
            s = jnp.dot(kblk[:, i * LANES:(i + 1) * LANES], qt_ref[0, i * LANES:(i + 1) * LANES, qg],
                        preferred_element_type=F32)
            s = s - shift_ref[0, 0, i:i + 1, qg]
            if masked:
                s = s + mask_ref[q_lo:q_lo + nk, qs]
            p = jnp.exp2(s).astype(BF16)
            lhs = jnp.concatenate([vt[i * HEAD_V:(i + 1) * HEAD_V], ones], axis=0)
            acc_ref[i, :, qs] += jnp.dot(lhs, p, preferred_element_type=F32)

    def full_block(kb):
        block(pl.multiple_of(kb * tk, tk), tk, 0, False)

    first = start_ref[(pl.program_id(0) * pairs + pl.program_id(1)) * n_q + qi]
    count = qi - first

    @pl.when(count % 2 == 1)
    def _():
        full_block(first)

    def body(it, c):
        kb = first + count % 2 + 2 * it
        full_block(kb)
        full_block(kb + 1)
        return c

    lax.fori_loop(0, count // 2, body, 0)
    half = tk // 2
    q0 = pl.multiple_of(qi * tq, tq)
    block(q0, half, 0, True)
    block(pl.multiple_of(q0 + half, half), half, half, True)
    o_t = jnp.concatenate([acc_ref[i, :HEAD_V, :] / acc_ref[i, HEAD_V:HEAD_V + 1, :] for i in range(2)], axis=0)
    o_ref[0, lo:lo + tq, :] = o_t.T.astype(o_ref.dtype)


def _diag_mask(tq, chunked):
    key = np.arange(tq)[:, None]
    qry = np.arange(tq)[None, :]
    ok = (key // CHUNK) <= (qry // CHUNK) if chunked else key <= qry
    return jnp.asarray(np.where(ok, 0.0, NEG_INF).astype(np.float32))


def _first_block(fcum, batch, seq, tq, tk):
    f = fcum.reshape(batch, seq, LANES)[:, :, :HEADS]
    f_q0 = f[:, ::tq, :]
    f_kl = f[:, tk - 1::tk, :]
    dead = (f_q0[:, :, None, :] - f_kl[:, None, :, :]) < -SKIP_LOG2
    dead = dead.reshape(batch, seq // tq, seq // tk, HEADS // 2, 2).all(axis=-1)
    lead = jnp.cumprod(dead.astype(jnp.int32), axis=2).sum(axis=2)
    limit = (jnp.arange(seq // tq, dtype=jnp.int32) * (tq // tk))[None, :, None]
    return jnp.minimum(lead, limit).transpose(0, 2, 1).reshape(-1).astype(jnp.int32)


def _attention(qt, k, vt, bound, fcum, *, chunked):
    batch, seq, wk = k.shape
    tq, tk = ATTN_TQ, ATTN_TK
    pairs = wk // (2 * LANES)
    n_q = seq // tq
    tqs = ATTN_Q_BLOCKS * tq
    fast = bound <= SAFE_SCORE_BOUND * LOG2E
    shift = lax.cond(fast, lambda: jnp.full((batch, pairs, 2, seq), bound, F32),
                     lambda: _row_max(qt, k, chunked=chunked))
    first = jnp.zeros((batch * pairs * n_q,), jnp.int32)
    if fcum is not None:
        first = jnp.where(fast, _first_block(fcum, batch, seq, tq, tk), first)
    return pl.pallas_call(
        functools.partial(_attn_kernel, tq=tq, tk=tk),
        out_shape=jax.ShapeDtypeStruct((batch, seq, pairs * 2 * HEAD_V), BF16),
        grid_spec=pltpu.PrefetchScalarGridSpec(
            num_scalar_prefetch=1,
            grid=(batch, pairs, n_q // ATTN_Q_BLOCKS),
            in_specs=[pl.BlockSpec((1, 2 * LANES, tqs), lambda b, h, i, st: (b, h, i)),
                      pl.BlockSpec((1, seq, 2 * LANES), lambda b, h, i, st: (b, 0, h)),
                      pl.BlockSpec((1, 2 * HEAD_V, seq), lambda b, h, i, st: (b, h, 0)),
                      pl.BlockSpec((1, 1, 2, tqs), lambda b, h, i, st: (b, h, 0, i)),
                      pl.BlockSpec((tq, tq), lambda b, h, i, st: (0, 0))],
            out_specs=pl.BlockSpec((1, tqs, 2 * HEAD_V), lambda b, h, i, st: (b, i, h)),
            scratch_shapes=[pltpu.VMEM((2, HEAD_V + SUM_ROWS, tq), F32)]),
        compiler_params=_cparams(3),
        name="attn_chunk_causal" if chunked else "attn_frame_causal",
    )(first, qt, k, vt, shift, _diag_mask(tq, chunked))


def _post_kernel(x_ref, om_ref, of_ref, gom_ref, gof_ref, wom_ref, wof_ref, gffn_ref, wrh_ref, wrl_ref,
                 tri_ref, x1_out, h2_out, route_out, route_t_out, count_out, carry_ref):
    step = pl.program_id(0)
    sub = tri_ref.shape[0]
    parts = x_ref.shape[1] // LANES

    @pl.when(step == 0)
    def _():
        carry_ref[...] = jnp.zeros_like(carry_ref)

    running = carry_ref[:1, :]
    for sb in range(x_ref.shape[0] // sub):
        running = _post_rows(slice(sb * sub, (sb + 1) * sub), pl.ds(sb * sub * parts, sub * parts), running,
                             x_ref, om_ref, of_ref, gom_ref, gof_ref, wom_ref, wof_ref, gffn_ref, wrh_ref,
                             wrl_ref, tri_ref, x1_out, h2_out, route_out, route_t_out)
    carry_ref[...] = jnp.broadcast_to(running, carry_ref.shape)
    count_out[...] = jnp.broadcast_to(running, count_out.shape)


def _post_rows(rs, tile_rows, running, x_ref, om_ref, of_ref, gom_ref, gof_ref, wom_ref, wof_ref, gffn_ref,
               wrh_ref, wrl_ref, tri_ref, x1_out, h2_out, route_out, route_t_out):
    ym = _rms(om_ref[rs, :].astype(F32), gom_ref[...]).astype(BF16)
    yf = _rms(of_ref[rs, :].astype(F32), gof_ref[...]).astype(BF16)
    x1 = (x_ref[rs, :] + jnp.dot(ym, wom_ref[...], preferred_element_type=F32)
          + jnp.dot(yf, wof_ref[...], preferred_element_type=F32))
    x1_out[rs, :] = x1
    h2 = _rms(x1, gffn_ref[...])
    _store_row_tiles(h2_out.at[tile_rows], h2)

    h_hi = h2.astype(BF16)
    h_lo = (h2 - h_hi.astype(F32)).astype(BF16)
    w_hi = wrh_ref[...]
    logits = (jnp.dot(h_hi, w_hi, preferred_element_type=F32)
              + jnp.dot(h_lo, w_hi, preferred_element_type=F32)
              + jnp.dot(h_hi, wrl_ref[...], preferred_element_type=F32))

    lane = lax.broadcasted_iota(jnp.int32, (1, LANES), 1).astype(F32)
    big = float(LANES)

    def first_argmax(v):
        mx = jnp.max(v, axis=-1, keepdims=True)
        idx = jnp.min(jnp.where(v == mx, lane, big), axis=-1, keepdims=True)
        return mx, idx

    lg = jnp.where(lane < N_GROUPS, logits, NEG_INF)
    mg, g_idx = first_argmax(lg)
    p_g = 1.0 / jnp.sum(jnp.exp(lg - mg), axis=-1, keepdims=True)
    e_lo = ROUTE_LANE0 + EXPERTS_PER_GROUP * g_idx
    le = jnp.where((lane >= e_lo) & (lane < e_lo + EXPERTS_PER_GROUP), logits, NEG_INF)
    m1, i1 = first_argmax(le)
    m2, i2 = first_argmax(jnp.where(lane == i1, NEG_INF, le))
    e2 = jnp.exp(m2 - m1)
    gate1 = p_g / (1.0 + e2)
    gate2 = p_g * e2 / (1.0 + e2)

    oh1 = lane == i1
    oh2 = lane == i2
    oh = (jnp.where(oh1, 1.0, 0.0) + jnp.where(oh2, 1.0, 0.0))
    before = jnp.dot(tri_ref[...], oh.astype(BF16), preferred_element_type=F32)
    base = before + running
    rank1 = jnp.sum(jnp.where(oh1, base, 0.0), axis=-1, keepdims=True)
    rank2 = jnp.sum(jnp.where(oh2, base, 0.0), axis=-1, keepdims=True)

    route = jnp.where(lane == 0, i1 - ROUTE_LANE0, 0.0)
    route = jnp.where(lane == 1, i2 - ROUTE_LANE0, route)
    route = jnp.where(lane == 2, rank1, route)
    route = jnp.where(lane == 3, rank2, route)
    route = jnp.where(lane == 4, gate1, route)
    route = jnp.where(lane == 5, gate2, route)
    route_out[rs, :] = route
    route_t_out[:, rs] = route.T[:route_t_out.shape[0]]
    return running + jnp.sum(oh, axis=0, keepdims=True)


def _post_attention(x2d, o_mla, o_fox, w):
    n, d = x2d.shape
    t = POST_TILE
    parts = d // LANES
    const = lambda i: (0, 0)
    row = lambda i: (i, 0)
    full = lambda a: pl.BlockSpec(a.shape, const)
    ins = [x2d, o_mla, o_fox, w["g_out_mla"], w["g_out_fox"], w["w_out_mla"], w["w_out_fox"],
           w["g_ffn"], w["w_router_hi"], w["w_router_lo"], w["tri_strict"]]
    in_specs = [pl.BlockSpec((t, d), row), pl.BlockSpec((t, o_mla.shape[1]), row),
                pl.BlockSpec((t, o_fox.shape[1]), row)] + [full(a) for a in ins[3:]]
    return pl.pallas_call(
        _post_kernel,
        out_shape=(jax.ShapeDtypeStruct((n, d), F32), jax.ShapeDtypeStruct((n * parts, LANES), F32),
                   jax.ShapeDtypeStruct((n, LANES), F32), jax.ShapeDtypeStruct((8, n), F32),
                   jax.ShapeDtypeStruct((8, LANES), F32)),
        grid=(n // t,),
        in_specs=in_specs,
        out_specs=(pl.BlockSpec((t, d), row), pl.BlockSpec((t * parts, LANES), row),
                   pl.BlockSpec((t, LANES), row), pl.BlockSpec((8, t), lambda i: (0, i)),
                   pl.BlockSpec((8, LANES), const)),
        scratch_shapes=[pltpu.VMEM((8, LANES), F32)],
        compiler_params=_cparams(1),
        name="post_attention",
    )(*ins)


def _tile_rows(dest, tile, parts):
    return (dest * parts).reshape(2, -1, tile).transpose(1, 0, 2).reshape(-1, 1, 2 * tile)


def _dispatch_kernel(pad_end_ref, padded_ref, n_used_ref, dest_ref, h_ref, xs_hbm, zero_ref, sem, zsem, *,
                     tile, parts):
    zrows = zero_ref.shape[0]
    n_blk = xs_hbm.shape[0] // zrows
    rblk = zrows // parts

    def zero_copy(blk):
        return pltpu.make_async_copy(zero_ref, xs_hbm.at[pl.ds(pl.multiple_of(blk * zrows, zrows), zrows)], zsem)

    @pl.when(pl.program_id(0) == 0)
    def _():
        zero_ref[...] = jnp.zeros_like(zero_ref)
        for wait in (False, True):
            for e in range(N_EXPERTS):
                @pl.when(padded_ref[e] > 0)
                def _():
                    cp = zero_copy(pad_end_ref[e] // rblk - 1)
                    cp.wait() if wait else cp.start()

            def tail(blk, c):
                cp = zero_copy(blk)
                cp.wait() if wait else cp.start()
                return c

            lax.fori_loop(n_used_ref[0], n_blk, tail, 0)

    def issue(jo, c):
        for ji in range(ISSUE_UNROLL):
            j = jo * ISSUE_UNROLL + ji
            src = h_ref.at[pl.ds(pl.multiple_of(j * parts, parts), parts)]
            for k in range(2):
                d = pl.multiple_of(dest_ref[0, 0, k * tile + j], parts)
                pltpu.make_async_copy(src, xs_hbm.at[pl.ds(d, parts)], sem).start(priority=k)
        return c

    lax.fori_loop(0, tile // ISSUE_UNROLL, issue, 0)
    for _ in range(2):
        pltpu.make_async_copy(h_ref, xs_hbm.at[pl.ds(0, tile * parts)], sem).wait()


def _dispatch(h2t, plan, parts):
    n = h2t.shape[0] // parts
    tile = DISPATCH_TILE
    dest3 = _tile_rows(plan["dest"], tile, parts)
    return pl.pallas_call(
        functools.partial(_dispatch_kernel, tile=tile, parts=parts),
        out_shape=jax.ShapeDtypeStruct((plan["n_rows"] * parts, LANES), h2t.dtype),
        grid_spec=pltpu.PrefetchScalarGridSpec(
            num_scalar_prefetch=3,
            grid=(n // tile,),
            in_specs=[pl.BlockSpec((1, 1, 2 * tile), lambda i, *_: (i, 0, 0), memory_space=pltpu.SMEM),
                      pl.BlockSpec((tile * parts, LANES), lambda i, *_: (i, 0))],
            out_specs=pl.BlockSpec(memory_space=pl.ANY),
            scratch_shapes=[pltpu.VMEM((EXPERT_BLOCK * parts, LANES), h2t.dtype), pltpu.SemaphoreType.DMA,
                            pltpu.SemaphoreType.DMA]),
        compiler_params=_cparams(1),
        name="moe_dispatch",
    )(plan["pad_end"], plan["padded"], plan["n_used"], dest3, h2t)


def _expert_kernel(blk_e_ref, n_used_ref, x_ref, wg_ref, wu_ref, wdn_ref, y_ref, wgu_bf, wd_bf, *, parts):
    i = pl.program_id(0)
    used = i < n_used_ref[0]
    rows = x_ref.shape[0] // parts
    fresh = jnp.logical_or(i == 0, blk_e_ref[i] != blk_e_ref[jnp.maximum(i - 1, 0)])

    @pl.when(jnp.logical_and(used, fresh))
    def _():
        wgu_bf[:, :EXPERT_FF] = wg_ref[0, 0].astype(BF16)
        wgu_bf[:, EXPERT_FF:] = wu_ref[0, 0].astype(BF16)
        wd_bf[...] = wdn_ref[0, 0].astype(BF16)

    @pl.when(used)
    def _():
        sub = EXPERT_SUB
        for sb in range(rows // sub):
            tiles = pl.ds(sb * sub * parts, sub * parts)
            xb = _load_row_tiles(x_ref.at[tiles], sub, parts).astype(BF16)
            gu = jnp.dot(xb, wgu_bf[...], preferred_element_type=F32)
            g = gu[:, :EXPERT_FF]
            act = (g / (1.0 + jnp.exp(-g))) * gu[:, EXPERT_FF:]
            _store_row_tiles(y_ref.at[tiles], jnp.dot(act.astype(BF16), wd_bf[...], preferred_element_type=F32))

    @pl.when(jnp.logical_not(used))
    def _():
        y_ref[...] = jnp.zeros_like(y_ref)


def _experts(xs, plan, layer, w_gate, w_up, w_down, parts):
    r = EXPERT_BLOCK
    d = parts * LANES
    rows = lambda i, be, nu: (jnp.minimum(i, nu[0] - 1), 0)
    wsel = lambda i, be, nu: (layer, be[i], 0, 0)
    return pl.pallas_call(
        functools.partial(_expert_kernel, parts=parts),
        out_shape=jax.ShapeDtypeStruct(xs.shape, F32),
        grid_spec=pltpu.PrefetchScalarGridSpec(
            num_scalar_prefetch=2,
            grid=(xs.shape[0] // (r * parts),),
            in_specs=[pl.BlockSpec((r * parts, LANES), rows),
                      pl.BlockSpec((1, 1, d, EXPERT_FF), wsel),
                      pl.BlockSpec((1, 1, d, EXPERT_FF), wsel),
                      pl.BlockSpec((1, 1, EXPERT_FF, d), wsel)],
            out_specs=pl.BlockSpec((r * parts, LANES), lambda i, be, nu: (i, 0)),
            scratch_shapes=[pltpu.VMEM((d, 2 * EXPERT_FF), BF16), pltpu.VMEM((EXPERT_FF, d), BF16)]),
        compiler_params=_cparams(1),
        name="moe_experts",
    )(plan["blk_e"], plan["n_used"], xs, w_gate, w_up, w_down)


def _ple_kernel(dest_ref, dest_next_ref, x1_ref, route_ref, p_ref, gnorm_ref, wgate_ref, wproj_ref, gout_ref,
                ys_hbm, x_out, rows_a, rows_b, sem, *, tile, parts):
    step = pl.program_id(0)
    n_steps = pl.num_programs(0)

    def row_copy(d, buf, slot, k, j):
        return pltpu.make_async_copy(ys_hbm.at[pl.ds(pl.multiple_of(d, parts), parts)],
                                     buf.at[k, pl.ds(j * parts, parts)], sem.at[slot])

    def wait_rows(buf, slot):
        for k in range(2):
            pltpu.make_async_copy(ys_hbm.at[pl.ds(0, tile * parts)], buf.at[k], sem.at[slot]).wait()

    @pl.when(step == 0)
    def _():
        def issue(jo, c):
            for ji in range(ISSUE_UNROLL):
                j = jo * ISSUE_UNROLL + ji
                for k in range(2):
                    pltpu.make_async_copy(
                        ys_hbm.at[pl.ds(pl.multiple_of(dest_ref[0, 0, k * tile + j], parts), parts)],
                        rows_a.at[k, pl.ds(pl.multiple_of(j * parts, parts), parts)], sem.at[0]).start(priority=k)
            return c
        lax.fori_loop(0, tile // ISSUE_UNROLL, issue, 0)

    def body(cur, nxt, slot):
        wait_rows(cur, slot)
        for j in range(tile):
            for k in range(2):
                row_copy(dest_next_ref[0, 0, k * tile + j], nxt, 1 - slot, k, j).start(priority=k)
        route = route_ref[...]
        y1 = _load_row_tiles(cur.at[0], tile, parts)
        y2 = _load_row_tiles(cur.at[1], tile, parts)
        x2 = x1_ref[...] + route[:, 4:5] * y1 + route[:, 5:6] * y2
        ple = _rms(jnp.dot(p_ref[0].astype(BF16), wproj_ref[...], preferred_element_type=F32), gout_ref[...])
        z = jnp.dot(_rms(x2, gnorm_ref[...]).astype(BF16), wgate_ref[...], preferred_element_type=F32)
        x_out[...] = x2 + ple / (1.0 + jnp.exp(-z))

        @pl.when(step == n_steps - 1)
        def _():
            wait_rows(nxt, 1 - slot)

    @pl.when(step % 2 == 0)
    def _():
        body(rows_a, rows_b, 0)

    @pl.when(step % 2 == 1)
    def _():
        body(rows_b, rows_a, 1)


def _combine_ple(x1, route, dest, ys, p3d, layer, w):
    n, d = x1.shape
    t = PLE_TILE
    parts = d // LANES
    steps = n // t
    const = lambda i: (0, 0)
    row = lambda i: (i, 0)
    full = lambda a: pl.BlockSpec(a.shape, const)
    dest3 = _tile_rows(dest, t, parts)
    return pl.pallas_call(
        functools.partial(_ple_kernel, tile=t, parts=parts),
        out_shape=jax.ShapeDtypeStruct((n, d), F32),
        grid=(steps,),
        in_specs=[pl.BlockSpec((1, 1, 2 * t), lambda i: (i, 0, 0), memory_space=pltpu.SMEM),
                  pl.BlockSpec((1, 1, 2 * t), lambda i: (jnp.minimum(i + 1, steps - 1), 0, 0),
                               memory_space=pltpu.SMEM),
                  pl.BlockSpec((t, d), row), pl.BlockSpec((t, LANES), row),
                  pl.BlockSpec((1, t, p3d.shape[2]), lambda i: (layer, i, 0)),
                  full(w["g_ple_norm"]), full(w["w_ple_gate"]), full(w["w_ple_proj"]), full(w["g_ple_out"]),
                  pl.BlockSpec(memory_space=pl.ANY)],
        out_specs=pl.BlockSpec((t, d), row),
        scratch_shapes=[pltpu.VMEM((2, t * parts, LANES), F32), pltpu.VMEM((2, t * parts, LANES), F32),
                        pltpu.SemaphoreType.DMA((2,))],
        compiler_params=_cparams(1),
        name="moe_combine_ple",
    )(dest3, dest3, x1, route, p3d, w["g_ple_norm"], w["w_ple_gate"], w["w_ple_proj"], w["g_ple_out"], ys)


def _row(v):
    return v.reshape(1, -1).astype(F32)


def _col_rep(v):
    return jnp.broadcast_to(v.astype(F32)[:, None], (v.shape[0], LANES))


def _pad_heads(wmat, real):
    k = wmat.shape[0]
    return jnp.pad(wmat.reshape(k, HEADS, real), ((0, 0), (0, 0), (0, LANES - real))).reshape(k, HEADS * LANES)


def _placement():
    place_k = np.zeros((LANES, HEADS * LANES), np.float32)
    place_q = np.zeros((HEADS * LANES, LANES), np.float32)
    for hd in range(HEADS):
        base = hd * LANES
        for j in range(3):
            src = j * PIECE_STRIDE + hd
            place_k[src, base + AUG_K + j] = -1.0
            place_q[base + AUG_Q + j, src] = 1.0
            place_k[ONE_LANE, base + AUG_Q + j] = 1.0
            place_q[base + AUG_K + j, ONE_LANE] = 1.0
    return jnp.asarray(place_k, BF16), jnp.asarray(place_q, BF16)


def _layer_weights(i, g_attn_norm, w_in, g_q_lora, w_uq, g_kv_lora, w_ukv, g_mla_q, g_mla_k, g_fox_q,
                   g_fox_k, b_fox_f, g_out_mla, g_out_fox, w_out, g_ffn_norm, w_router_group,
                   w_router_expert, w_exp_gate, w_exp_up, w_exp_down, g_ple_norm, w_ple_gate, w_ple_proj,
                   g_ple_out):
    d = w_in.shape[1]
    wf = HEADS * FOX_DIM
    c_kv = Q_LORA
    c_pe = c_kv + KV_LORA
    c_fq = c_pe + MLA_ROPE
    c_fk = c_fq + wf
    c_fv = c_fk + wf
    c_fl = c_fv + wf
    win = w_in[i]
    pad_pe = jnp.zeros((d, LANES), F32).at[:, MLA_NOPE:MLA_QK].set(win[:, c_pe:c_fq])
    group = lambda v: jnp.concatenate(
        [jnp.pad(v, [(0, 0)] * (v.ndim - 1) + [(0, PIECE_STRIDE - HEADS)])] * 3
        + [jnp.zeros(v.shape[:-1] + (LANES - 3 * PIECE_STRIDE,), v.dtype)], axis=-1)
    pad_fl = group(win[:, c_fl:])
    w = {}
    w["g_attn"] = _row(g_attn_norm[i])
    w["w_a"] = jnp.concatenate([win[:, :c_pe], pad_pe, pad_fl], axis=1).astype(BF16)
    w["w_fk"] = win[:, c_fk:c_fv].astype(BF16)
    w["w_fq_t"] = win[:, c_fq:c_fk].T.astype(BF16)
    w["w_fv_t"] = win[:, c_fv:c_fl].T.astype(BF16)
    w["b_fl"] = group(b_fox_f[i].astype(F32)[None, :])
    w["place_k"], w["place_q"] = _placement()
    w["g_q_lora"] = _row(g_q_lora[i])
    w["w_uq_t"] = w_uq[i].T.astype(BF16)
    w["g_kv_lora"] = _row(g_kv_lora[i])
    ukv = w_ukv[i].reshape(KV_LORA, HEADS, MLA_NOPE + HEAD_V)
    w["w_uk"] = ukv[:, :, :MLA_NOPE].reshape(KV_LORA, -1).astype(BF16)
    w["w_uv_t"] = ukv[:, :, MLA_NOPE:].reshape(KV_LORA, -1).T.astype(BF16)
    w["g_mla_q"] = _col_rep(jnp.pad(g_mla_q[i], (0, LANES - MLA_QK))) * (MLA_QK ** -0.5 * LOG2E)
    w["g_mla_k"] = _row(jnp.pad(g_mla_k[i], (0, LANES - MLA_QK)))
    w["g_fox_q"] = _col_rep(jnp.pad(g_fox_q[i], (0, LANES - FOX_DIM))) * (FOX_DIM ** -0.5 * LOG2E)
    w["g_fox_k"] = _row(jnp.pad(g_fox_k[i], (0, LANES - FOX_DIM)))
    bound = lambda gq, gk, dim: (1.02 * LOG2E * dim ** 0.5) * jnp.max(jnp.abs(gq)) * jnp.max(jnp.abs(gk))
    w["bound_mla"] = bound(g_mla_q[i], g_mla_k[i], MLA_QK).astype(F32)
    w["bound_fox"] = bound(g_fox_q[i], g_fox_k[i], FOX_DIM).astype(F32)
    w["g_out_mla"] = _row(g_out_mla[i])
    w["g_out_fox"] = _row(g_out_fox[i])
    wm = HEADS * HEAD_V
    w["w_out_mla"] = w_out[i, :wm].astype(BF16)
    w["w_out_fox"] = w_out[i, wm:].astype(BF16)
    w["g_ffn"] = _row(g_ffn_norm[i])
    wr = jnp.zeros((d, LANES), F32)
    wr = wr.at[:, :N_GROUPS].set(w_router_group[i]).at[:, ROUTE_LANE0:ROUTE_LANE0 + N_EXPERTS].set(
        w_router_expert[i])
    w["w_router_hi"] = wr.astype(BF16)
    w["w_router_lo"] = (wr - w["w_router_hi"].astype(F32)).astype(BF16)
    w["g_ple_norm"] = _row(g_ple_norm[i])
    w["w_ple_gate"] = w_ple_gate[i].astype(BF16)
    w["w_ple_proj"] = w_ple_proj[i].astype(BF16)
    w["g_ple_out"] = _row(g_ple_out[i])
    def tri(t, strict):
        r = lax.broadcasted_iota(jnp.int32, (t, t), 0)
        c = lax.broadcasted_iota(jnp.int32, (t, t), 1)
        return ((c < r) if strict else (c <= r)).astype(BF16)

    w["tri_incl"] = tri(PRE_SUB, False)
    w["tri_strict"] = tri(POST_SUB, True)
    return w


def _route_plan(route_t, counts, n):
    r = EXPERT_BLOCK
    cnt = counts[0, ROUTE_LANE0:ROUTE_LANE0 + N_EXPERTS].astype(jnp.int32)
    padded = ((cnt + r - 1) // r) * r
    pad_end = jnp.cumsum(padded)
    pad_start = pad_end - padded
    e = route_t[0:2].astype(jnp.int32)
    rank = route_t[2:4].astype(jnp.int32)
    onehot = e[:, None, :] == jnp.arange(N_EXPERTS, dtype=jnp.int32)[None, :, None]
    dest = jnp.sum(jnp.where(onehot, pad_start[None, :, None], 0), axis=1) + rank
    n_rows = 2 * n + N_EXPERTS * r
    blk_start = jnp.arange(n_rows // r, dtype=jnp.int32) * r
    blk_e = jnp.sum((blk_start[:, None] >= pad_end[None, :]).astype(jnp.int32), axis=1)
    blk_e = jnp.minimum(blk_e, N_EXPERTS - 1)
    n_used = (pad_end[-1:] // r).astype(jnp.int32)
    return {"dest": dest, "blk_e": blk_e, "n_used": n_used, "n_rows": n_rows,
            "pad_end": pad_end.astype(jnp.int32), "padded": padded.astype(jnp.int32)}


def kernel(x, p, positions, g_attn_norm, w_in, g_q_lora, w_uq, g_kv_lora, w_ukv, g_mla_q, g_mla_k, g_fox_q,
           g_fox_k, b_fox_f, g_out_mla, g_out_fox, w_out, g_ffn_norm, w_router_group, w_router_expert,
           w_exp_gate, w_exp_up, w_exp_down, g_ple_norm, w_ple_gate, w_ple_proj, g_ple_out):
    batch, seq, d = x.shape
    n = batch * seq
    depth = w_in.shape[0]
    params = (g_attn_norm, w_in, g_q_lora, w_uq, g_kv_lora, w_ukv, g_mla_q, g_mla_k, g_fox_q, g_fox_k,
              b_fox_f, g_out_mla, g_out_fox, w_out, g_ffn_norm, w_router_group, w_router_expert,
              w_exp_gate, w_exp_up, w_exp_down, g_ple_norm, w_ple_gate, w_ple_proj, g_ple_out)
    tables = _rope_tables(positions)
    xc = x.reshape(n, d)
    for i in range(depth):
        w = _layer_weights(i, *params)
        qt, k, vt, fqt, fk, fvt, fcum = _pre_attention(xc, tables, w, seq)
        b3 = lambda a: a.reshape(batch, seq, a.shape[-1])
        o_mla = _attention(qt, b3(k), vt, w["bound_mla"], None, chunked=True).reshape(n, -1)
        o_fox = _attention(fqt, b3(fk), fvt, w["bound_fox"], fcum, chunked=False).reshape(n, -1)
        x1, h2, route, route_t, counts = _post_attention(xc, o_mla, o_fox, w)
        plan = _route_plan(route_t, counts, n)
        parts = d // LANES
        xs = _dispatch(h2, plan, parts)
        ys = _experts(xs, plan, i, w_exp_gate, w_exp_up, w_exp_down, parts)
        xc = _combine_ple(x1, route, plan["dest"], ys, p.reshape(depth, n, -1), i, w)
    return xc.reshape(batch, seq, d)
```

```python
import functools
import math

import numpy as np
import jax
import jax.numpy as jnp
from jax import lax
from jax.experimental import pallas as pl
from jax.experimental.pallas import tpu as pltpu

F32 = jnp.float32
BF16 = jnp.bfloat16

EPS = 1e-6
NEG_INF = -1e30
ROPE_THETA = 10000.0
LOG2E = math.log2(math.e)

LANES = 128
CHUNK = 64
HEADS = 8
MLA_NOPE = 64
MLA_ROPE = 32
MLA_QK = MLA_NOPE + MLA_ROPE
HEAD_V = 64
FOX_DIM = 64
Q_LORA = 256
KV_LORA = 128
N_GROUPS = 4
EXPERTS_PER_GROUP = 8
N_EXPERTS = N_GROUPS * EXPERTS_PER_GROUP
EXPERT_FF = 256
ROUTE_LANE0 = N_GROUPS

AUG_Q = FOX_DIM
AUG_K = FOX_DIM + 3
PIECE_STRIDE = 16
ONE_LANE = 3 * PIECE_STRIDE

TOKEN_TILE = 512
PRE_SUB = 512
POST_TILE = 1024
POST_SUB = 256
ATTN_TQ = 1024
ATTN_TK = 1024
ATTN_Q_BLOCKS = 2
SUM_ROWS = 16
EXPERT_BLOCK = 512
EXPERT_SUB = 256
DISPATCH_TILE = 2048
PLE_TILE = 256
ISSUE_UNROLL = 8
VMEM_LIMIT = 56 * 1024 * 1024
SAFE_SCORE_BOUND = 40.0
SKIP_LOG2 = 160.0

_NT = (((1,), (1,)), ((), ()))


def _cparams(n_axes):
    return pltpu.CompilerParams(dimension_semantics=("arbitrary",) * n_axes,
                                vmem_limit_bytes=VMEM_LIMIT)


def _rms(x, g):
    return x * lax.rsqrt(jnp.mean(x * x, axis=-1, keepdims=True) + EPS) * g


def _split3(x):
    hi = x.astype(BF16)
    r1 = x - hi.astype(F32)
    mid = r1.astype(BF16)
    lo = (r1 - mid.astype(F32)).astype(BF16)
    return hi, mid, lo


def _store_row_tiles(ref, v):
    parts = v.shape[1] // LANES
    for s in range(parts):
        ref[pl.ds(s, v.shape[0], stride=parts), :] = v[:, s * LANES:(s + 1) * LANES]


def _load_row_tiles(ref, rows, parts):
    return jnp.concatenate([ref[pl.ds(s, rows, stride=parts), :] for s in range(parts)], axis=1)


def _lane_tile(a, width):
    return jnp.tile(a, (1, width // LANES))


def _rope_kernel(pos_row_ref, invf_rep_ref, cos_ref, sin_ref, cost_ref, sint_ref):
    t = pos_row_ref.shape[-1]
    half = invf_rep_ref.shape[0]
    ang_t = _lane_tile(invf_rep_ref[...], t) * pos_row_ref[0]
    c = jnp.cos(ang_t)
    s = jnp.sin(ang_t)
    cost_ref[...] = c
    sint_ref[...] = s
    ones = jnp.ones((MLA_NOPE, t), F32)
    zeros = jnp.zeros((MLA_NOPE, t), F32)
    pad = LANES - MLA_NOPE - 2 * half
    cos_ref[...] = jnp.concatenate([ones, c, c, ones[:pad]], axis=0).T
    sin_ref[...] = jnp.concatenate([zeros, -s, s, zeros[:pad]], axis=0).T


def _rope_tables(positions):
    n = positions.size
    half = MLA_ROPE // 2
    inv_freq = ROPE_THETA ** (-np.arange(0, MLA_ROPE, 2, dtype=np.float32) / MLA_ROPE)
    invf_rep = np.broadcast_to(inv_freq[:, None], (half, LANES)).astype(np.float32)
    pos = positions.astype(F32)
    t = TOKEN_TILE
    return pl.pallas_call(
        _rope_kernel,
        out_shape=(jax.ShapeDtypeStruct((n, LANES), F32), jax.ShapeDtypeStruct((n, LANES), F32),
                   jax.ShapeDtypeStruct((half, n), F32), jax.ShapeDtypeStruct((half, n), F32)),
        grid=(n // t,),
        in_specs=[pl.BlockSpec((1, 1, t), lambda i: (i, 0, 0)),
                  pl.BlockSpec((half, LANES), lambda i: (0, 0))],
        out_specs=(pl.BlockSpec((t, LANES), lambda i: (i, 0)), pl.BlockSpec((t, LANES), lambda i: (i, 0)),
                   pl.BlockSpec((half, t), lambda i: (0, i)), pl.BlockSpec((half, t), lambda i: (0, i))),
        compiler_params=_cparams(1),
        name="rope_tables",
    )(pos.reshape(n // t, 1, t), jnp.asarray(invf_rep))


def _pre_kernel(x_ref, cos_ref, sin_ref, cost_ref, sint_ref, gattn_ref, wa_ref, wfk_ref, wfqt_ref, wfvt_ref,
                bfl_ref, tri_ref, plk_ref, plq_ref, gql_ref, wuqt_ref, gkvl_ref, wuk_ref, wuvt_ref,
                gq_ref, gk_ref, gfq_ref, gfk_ref,
                qt_out, k_out, vt_out, fqt_out, fk_out, fvt_out, fcum_out, carry_ref, *, tiles_per_seq):
    @pl.when(pl.program_id(0) % tiles_per_seq == 0)
    def _():
        carry_ref[...] = jnp.zeros_like(carry_ref)

    sub = tri_ref.shape[0]
    running = carry_ref[:1, :]
    for sb in range(x_ref.shape[0] // sub):
        running = _pre_rows(slice(sb * sub, (sb + 1) * sub), running, x_ref, cos_ref, sin_ref, cost_ref, sint_ref,
                            gattn_ref, wa_ref, wfk_ref, wfqt_ref, wfvt_ref, bfl_ref, tri_ref, plk_ref, plq_ref,
                            gql_ref, wuqt_ref, gkvl_ref, wuk_ref, wuvt_ref, gq_ref, gk_ref, gfq_ref, gfk_ref,
                            qt_out, k_out, vt_out, fqt_out, fk_out, fvt_out, fcum_out)
    carry_ref[...] = jnp.broadcast_to(running, carry_ref.shape)


def _pre_rows(rs, running, x_ref, cos_ref, sin_ref, cost_ref, sint_ref, gattn_ref, wa_ref, wfk_ref, wfqt_ref,
              wfvt_ref, bfl_ref, tri_ref, plk_ref, plq_ref, gql_ref, wuqt_ref, gkvl_ref, wuk_ref, wuvt_ref,
              gq_ref, gk_ref, gfq_ref, gfk_ref, qt_out, k_out, vt_out, fqt_out, fk_out, fvt_out, fcum_out):
    x = x_ref[rs, :]
    t = x.shape[0]
    h = _rms(x, gattn_ref[...]).astype(BF16)
    pa = jnp.dot(h, wa_ref[...], preferred_element_type=F32)
    fk = jnp.dot(h, wfk_ref[...], preferred_element_type=F32)
    fqt = lax.dot_general(wfqt_ref[...], h, _NT, preferred_element_type=F32)
    fvt_out[0, :, rs] = lax.dot_general(wfvt_ref[...], h, _NT, preferred_element_type=F32).astype(BF16)
    half = MLA_ROPE // 2
    lane = lax.broadcasted_iota(jnp.int32, (1, LANES), 1)
    low = lane < FOX_DIM

    def head_lanes(m, hd):
        v = m[:, (hd // 2) * LANES:(hd // 2 + 1) * LANES]
        if hd % 2:
            v = pltpu.roll(v, LANES // 2, 1)
        return jnp.where(low, v, 0.0)

    qn = _rms(pa[:, :Q_LORA], gql_ref[...]).astype(BF16)
    kvn = _rms(pa[:, Q_LORA:Q_LORA + KV_LORA], gkvl_ref[...]).astype(BF16)
    vt_out[0, :, rs] = lax.dot_general(wuvt_ref[...], kvn, _NT, preferred_element_type=F32).astype(BF16)

    kn = jnp.dot(kvn, wuk_ref[...], preferred_element_type=F32)
    kpe = pa[:, Q_LORA + KV_LORA:Q_LORA + KV_LORA + LANES]
    cos_l = cos_ref[rs, :]
    sin_l = sin_ref[rs, :]
    gk = gk_ref[...]
    for hd in range(HEADS):
        v = head_lanes(kn, hd) + kpe
        v = v * lax.rsqrt(jnp.sum(v * v, axis=-1, keepdims=True) * (1.0 / MLA_QK) + EPS) * gk
        swapped = jnp.where(lane < MLA_NOPE + half, pltpu.roll(v, LANES - half, 1), pltpu.roll(v, half, 1))
        k_out[rs, hd * LANES:(hd + 1) * LANES] = (v * cos_l + swapped * sin_l).astype(BF16)

    qt = lax.dot_general(wuqt_ref[...], qn, _NT, preferred_element_type=F32)
    cos_r = cost_ref[:, rs]
    sin_r = sint_ref[:, rs]
    gq = _lane_tile(gq_ref[...], t)[:MLA_QK]
    pad_q = jnp.zeros((LANES - MLA_QK, t), BF16)
    for hd in range(HEADS):
        blk = qt[hd * MLA_QK:(hd + 1) * MLA_QK]
        r = lax.rsqrt(jnp.sum(blk * blk, axis=0, keepdims=True) * (1.0 / MLA_QK) + EPS)
        blk = blk * r * gq
        x1, x2 = blk[MLA_NOPE:MLA_NOPE + half], blk[MLA_NOPE + half:]
        blk = jnp.concatenate([blk[:MLA_NOPE], x1 * cos_r - x2 * sin_r, x2 * cos_r + x1 * sin_r], axis=0)
        qt_out[0, hd * LANES:(hd + 1) * LANES, rs] = jnp.concatenate([blk.astype(BF16), pad_q], axis=0)


    def by_group(a, b, c):
        return jnp.where(lane < PIECE_STRIDE, a, jnp.where(lane < 2 * PIECE_STRIDE, b, c))

    z = pa[:, Q_LORA + KV_LORA + LANES:] + bfl_ref[...]
    logf = jnp.minimum(z, 0.0) - jnp.log1p(jnp.exp(-jnp.abs(z)))
    parts = jnp.dot(tri_ref[...], by_group(*_split3(logf)), preferred_element_type=F32)
    tot = parts + pltpu.roll(parts, PIECE_STRIDE, 1) + pltpu.roll(parts, 2 * PIECE_STRIDE, 1)
    cum = by_group(pltpu.roll(tot, LANES - 2 * PIECE_STRIDE, 1), pltpu.roll(tot, LANES - PIECE_STRIDE, 1), tot)
    cum = cum + running
    cum2 = cum * LOG2E
    fcum_out[rs, :] = cum2
    packed = by_group(*_split3(cum2))
    packed = jnp.where(lane == ONE_LANE, jnp.ones_like(packed), packed)
    aug_k = jnp.dot(packed, plk_ref[...], preferred_element_type=F32)
    aug_q = lax.dot_general(plq_ref[...], packed, _NT, preferred_element_type=F32)

    gfk = gfk_ref[...]
    for hd in range(HEADS):
        sl = slice(hd * LANES, (hd + 1) * LANES)
        v = head_lanes(fk, hd)
        v = v * lax.rsqrt(jnp.sum(v * v, axis=-1, keepdims=True) * (1.0 / FOX_DIM) + EPS) * gfk
        fk_out[rs, sl] = (v + aug_k[:, sl]).astype(BF16)

    gfq = _lane_tile(gfq_ref[...], t)[:FOX_DIM]
    for hd in range(HEADS):
        blk = fqt[hd * FOX_DIM:(hd + 1) * FOX_DIM]
        r = lax.rsqrt(jnp.sum(blk * blk, axis=0, keepdims=True) * (1.0 / FOX_DIM) + EPS)
        aug = aug_q[hd * LANES + FOX_DIM:(hd + 1) * LANES]
        fqt_out[0, hd * LANES:(hd + 1) * LANES, rs] = jnp.concatenate([blk * r * gfq, aug], axis=0).astype(BF16)
    return cum[t - 1:t, :]


def _pre_attention(x2d, tables, w, seq):
    n, d = x2d.shape
    t = TOKEN_TILE
    tiles_per_seq = seq // t
    batch = n // seq
    cos_l, sin_l, cos_t, sin_t = tables
    half = MLA_ROPE // 2
    row = lambda i: (i, 0)
    seq_t = lambda i: (i // tiles_per_seq, 0, i % tiles_per_seq)

    def full(a):
        return pl.BlockSpec(a.shape, lambda i, nd=a.ndim: (0,) * nd)

    weights = [w["g_attn"], w["w_a"], w["w_fk"], w["w_fq_t"], w["w_fv_t"], w["b_fl"], w["tri_incl"],
               w["place_k"], w["place_q"], w["g_q_lora"], w["w_uq_t"], w["g_kv_lora"], w["w_uk"], w["w_uv_t"],
               w["g_mla_q"], w["g_mla_k"], w["g_fox_q"], w["g_fox_k"]]
    in_specs = [pl.BlockSpec((t, d), row), pl.BlockSpec((t, LANES), row), pl.BlockSpec((t, LANES), row),
                pl.BlockSpec((half, t), lambda i: (0, i)), pl.BlockSpec((half, t), lambda i: (0, i))]
    in_specs += [full(a) for a in weights]
    wq = HEADS * LANES
    wv = HEADS * HEAD_V
    out_shape = (jax.ShapeDtypeStruct((batch, wq, seq), BF16), jax.ShapeDtypeStruct((n, wq), BF16),
                 jax.ShapeDtypeStruct((batch, wv, seq), BF16), jax.ShapeDtypeStruct((batch, wq, seq), BF16),
                 jax.ShapeDtypeStruct((n, wq), BF16), jax.ShapeDtypeStruct((batch, wv, seq), BF16),
                 jax.ShapeDtypeStruct((n, LANES), F32))
    out_specs = (pl.BlockSpec((1, wq, t), seq_t), pl.BlockSpec((t, wq), row), pl.BlockSpec((1, wv, t), seq_t),
                 pl.BlockSpec((1, wq, t), seq_t), pl.BlockSpec((t, wq), row), pl.BlockSpec((1, wv, t), seq_t),
                 pl.BlockSpec((t, LANES), row))
    return pl.pallas_call(
        functools.partial(_pre_kernel, tiles_per_seq=tiles_per_seq),
        out_shape=out_shape,
        grid=(n // t,),
        in_specs=in_specs,
        out_specs=out_specs,
        scratch_shapes=[pltpu.VMEM((8, LANES), F32)],
        compiler_params=_cparams(1),
        name="pre_attention",
    )(x2d, cos_l, sin_l, cos_t, sin_t, *weights)


def _allowed(k0, q0, tk, tq, chunked):
    key = k0 + lax.broadcasted_iota(jnp.int32, (tk, tq), 0)
    qry = q0 + lax.broadcasted_iota(jnp.int32, (tk, tq), 1)
    if chunked:
        return (key // CHUNK) <= (qry // CHUNK)
    return key <= qry


def _rowmax_kernel(qt_ref, k_ref, m_ref, *, chunked, tq, tk):
    qi = pl.program_id(2)
    n_diag = tq // tk

    def block(kb, carry, masked):
        k0 = pl.multiple_of(kb * tk, tk)
        kblk = k_ref[0, pl.ds(k0, tk), :]
        out = []
        for i in range(2):
            s = jnp.dot(kblk[:, i * LANES:(i + 1) * LANES], qt_ref[0, i * LANES:(i + 1) * LANES, :],
                        preferred_element_type=F32)
            if masked:
                s = jnp.where(_allowed(k0, qi * tq, tk, tq, chunked), s, NEG_INF)
            out.append(jnp.maximum(carry[i], jnp.max(s, axis=0, keepdims=True)))
        return tuple(out)

    init = tuple(jnp.full((1, tq), NEG_INF, F32) for _ in range(2))
    carry = lax.fori_loop(0, qi * n_diag, lambda kb, c: block(kb, c, False), init)
    for j in range(n_diag):
        carry = block(qi * n_diag + j, carry, True)
    m_ref[0, 0] = jnp.concatenate(carry, axis=0)


def _row_max(qt, k, *, chunked):
    batch, seq, wk = k.shape
    tq, tk = ATTN_TQ, ATTN_TK
    pairs = wk // (2 * LANES)
    out = pl.pallas_call(
        functools.partial(_rowmax_kernel, chunked=chunked, tq=tq, tk=tk),
        out_shape=jax.ShapeDtypeStruct((batch, pairs, 2, seq), F32),
        grid=(batch, pairs, seq // tq),
        in_specs=[pl.BlockSpec((1, 2 * LANES, tq), lambda b, h, i: (b, h, i)),
                  pl.BlockSpec((1, seq, 2 * LANES), lambda b, h, i: (b, 0, h))],
        out_specs=pl.BlockSpec((1, 1, 2, tq), lambda b, h, i: (b, h, 0, i)),
        compiler_params=_cparams(3),
        name="attn_rowmax",
    )(qt, k)
    return out


def _attn_kernel(start_ref, qt_ref, k_ref, vt_ref, shift_ref, mask_ref, o_ref, acc_ref, *, tq, tk):
    q_blocks = qt_ref.shape[-1] // tq
    for sub in range(q_blocks):
        _attn_q_block(pl.program_id(2) * q_blocks + sub, sub * tq, pl.num_programs(2) * q_blocks,
                      start_ref, qt_ref, k_ref, vt_ref, shift_ref, mask_ref, o_ref, acc_ref, tq=tq, tk=tk)


def _attn_q_block(qi, lo, n_q, start_ref, qt_ref, k_ref, vt_ref, shift_ref, mask_ref, o_ref, acc_ref, *, tq, tk):
    pairs = pl.num_programs(1)
    acc_ref[...] = jnp.zeros_like(acc_ref)

    def block(k0, nk, q_lo, masked):
        qs = slice(q_lo, tq)
        qg = slice(lo + q_lo, lo + tq)
        kblk = k_ref[0, pl.ds(k0, nk), :]
        vt = vt_ref[0, :, pl.ds(k0, nk)]
        ones = jnp.ones((SUM_ROWS, nk), BF16)
        for i in range(2):
            s = jnp.dot(kblk[:, i * LANES:(i + 1) * LANES], qt_ref[0, i * LANES:(i + 1) * LANES, qg],
                        preferred_element_type=F32)
            s = s - shift_ref[0, 0, i:i + 1, qg]
            if masked:
                s = s + mask_ref[q_lo:q_lo + nk, qs]
            p = jnp.exp2(s).astype(BF16)
            lhs = jnp.concatenate([vt[i * HEAD_V:(i + 1) * HEAD_V], ones], axis=0)
            acc_ref[i, :, qs] += jnp.dot(lhs, p, preferred_element_type=F32)

    def full_block(kb):
        block(pl.multiple_of(kb * tk, tk), tk, 0, False)

    first = start_ref[(pl.program_id(0) * pairs + pl.program_id(1)) * n_q + qi]
    count = qi - first

    @pl.when(count % 2 == 1)
    def _():
        full_block(first)

    def body(it, c):
        kb = first + count % 2 + 2 * it
        full_block(kb)
        full_block(kb + 1)
        return c

    lax.fori_loop(0, count // 2, body, 0)
    half = tk // 2
    q0 = pl.multiple_of(qi * tq, tq)
    block(q0, half, 0, True)
    block(pl.multiple_of(q0 + half, half), half, half, True)
    o_t = jnp.concatenate([acc_ref[i, :HEAD_V, :] / acc_ref[i, HEAD_V:HEAD_V + 1, :] for i in range(2)], axis=0)
    o_ref[0, lo:lo + tq, :] = o_t.T.astype(o_ref.dtype)


def _diag_mask(tq, chunked):
    key = np.arange(tq)[:, None]
    qry = np.arange(tq)[None, :]
    ok = (key // CHUNK) <= (qry // CHUNK) if chunked else key <= qry
    return jnp.asarray(np.where(ok, 0.0, NEG_INF).astype(np.float32))


def _first_block(fcum, batch, seq, tq, tk):
    f = fcum.reshape(batch, seq, LANES)[:, :, :HEADS]
    f_q0 = f[:, ::tq, :]
    f_kl = f[:, tk - 1::tk, :]
    dead = (f_q0[:, :, None, :] - f_kl[:, None, :, :]) < -SKIP_LOG2
    dead = dead.reshape(batch, seq // tq, seq // tk, HEADS // 2, 2).all(axis=-1)
    lead = jnp.cumprod(dead.astype(jnp.int32), axis=2).sum(axis=2)
    limit = (jnp.arange(seq // tq, dtype=jnp.int32) * (tq // tk))[None, :, None]
    return jnp.minimum(lead, limit).transpose(0, 2, 1).reshape(-1).astype(jnp.int32)


def _attention(qt, k, vt, bound, fcum, *, chunked):
    batch, seq, wk = k.shape
    tq, tk = ATTN_TQ, ATTN_TK
    pairs = wk // (2 * LANES)
    n_q = seq // tq
    tqs = ATTN_Q_BLOCKS * tq
    fast = bound <= SAFE_SCORE_BOUND * LOG2E
    shift = lax.cond(fast, lambda: jnp.full((batch, pairs, 2, seq), bound, F32),
                     lambda: _row_max(qt, k, chunked=chunked))
    first = jnp.zeros((batch * pairs * n_q,), jnp.int32)
    if fcum is not None:
        first = jnp.where(fast, _first_block(fcum, batch, seq, tq, tk), first)
    return pl.pallas_call(
        functools.partial(_attn_kernel, tq=tq, tk=tk),
        out_shape=jax.ShapeDtypeStruct((batch, seq, pairs * 2 * HEAD_V), BF16),
        grid_spec=pltpu.PrefetchScalarGridSpec(
            num_scalar_prefetch=1,
            grid=(batch, pairs, n_q // ATTN_Q_BLOCKS),
            in_specs=[pl.BlockSpec((1, 2 * LANES, tqs), lambda b, h, i, st: (b, h, i)),
                      pl.BlockSpec((1, seq, 2 * LANES), lambda b, h, i, st: (b, 0, h)),
                      pl.BlockSpec((1, 2 * HEAD_V, seq), lambda b, h, i, st: (b, h, 0)),
                      pl.BlockSpec((1, 1, 2, tqs), lambda b, h, i, st: (b, h, 0, i)),
                      pl.BlockSpec((tq, tq), lambda b, h, i, st: (0, 0))],
            out_specs=pl.BlockSpec((1, tqs, 2 * HEAD_V), lambda b, h, i, st: (b, i, h)),
            scratch_shapes=[pltpu.VMEM((2, HEAD_V + SUM_ROWS, tq), F32)]),
        compiler_params=_cparams(3),
        name="attn_chunk_causal" if chunked else "attn_frame_causal",
    )(first, qt, k, vt, shift, _diag_mask(tq, chunked))


def _post_kernel(x_ref, om_ref, of_ref, gom_ref, gof_ref, wom_ref, wof_ref, gffn_ref, wrh_ref, wrl_ref,
                 tri_ref, x1_out, h2_out, route_out, route_t_out, count_out, carry_ref):
    step = pl.program_id(0)
    sub = tri_ref.shape[0]
    parts = x_ref.shape[1] // LANES

    @pl.when(step == 0)
    def _():
        carry_ref[...] = jnp.zeros_like(carry_ref)

    running = carry_ref[:1, :]
    for sb in range(x_ref.shape[0] // sub):
        running = _post_rows(slice(sb * sub, (sb + 1) * sub), pl.ds(sb * sub * parts, sub * parts), running,
                             x_ref, om_ref, of_ref, gom_ref, gof_ref, wom_ref, wof_ref, gffn_ref, wrh_ref,
                             wrl_ref, tri_ref, x1_out, h2_out, route_out, route_t_out)
    carry_ref[...] = jnp.broadcast_to(running, carry_ref.shape)
    count_out[...] = jnp.broadcast_to(running, count_out.shape)


def _post_rows(rs, tile_rows, running, x_ref, om_ref, of_ref, gom_ref, gof_ref, wom_ref, wof_ref, gffn_ref,
               wrh_ref, wrl_ref, tri_ref, x1_out, h2_out, route_out, route_t_out):
    ym = _rms(om_ref[rs, :].astype(F32), gom_ref[...]).astype(BF16)
    yf = _rms(of_ref[rs, :].astype(F32), gof_ref[...]).astype(BF16)
    x1 = (x_ref[rs, :] + jnp.dot(ym, wom_ref[...], preferred_element_type=F32)
          + jnp.dot(yf, wof_ref[...], preferred_element_type=F32))
    x1_out[rs, :] = x1
    h2 = _rms(x1, gffn_ref[...])
    _store_row_tiles(h2_out.at[tile_rows], h2)

    h_hi = h2.astype(BF16)
    h_lo = (h2 - h_hi.astype(F32)).astype(BF16)
    w_hi = wrh_ref[...]
    logits = (jnp.dot(h_hi, w_hi, preferred_element_type=F32)
              + jnp.dot(h_lo, w_hi, preferred_element_type=F32)
              + jnp.dot(h_hi, wrl_ref[...], preferred_element_type=F32))

    lane = lax.broadcasted_iota(jnp.int32, (1, LANES), 1).astype(F32)
    big = float(LANES)

    def first_argmax(v):
        mx = jnp.max(v, axis=-1, keepdims=True)
        idx = jnp.min(jnp.where(v == mx, lane, big), axis=-1, keepdims=True)
        return mx, idx

    lg = jnp.where(lane < N_GROUPS, logits, NEG_INF)
    mg, g_idx = first_argmax(lg)
    p_g = 1.0 / jnp.sum(jnp.exp(lg - mg), axis=-1, keepdims=True)
    e_lo = ROUTE_LANE0 + EXPERTS_PER_GROUP * g_idx
    le = jnp.where((lane >= e_lo) & (lane < e_lo + EXPERTS_PER_GROUP), logits, NEG_INF)
    m1, i1 = first_argmax(le)
    m2, i2 = first_argmax(jnp.where(lane == i1, NEG_INF, le))
    e2 = jnp.exp(m2 - m1)
    gate1 = p_g / (1.0 + e2)
    gate2 = p_g * e2 / (1.0 + e2)

    oh1 = lane == i1
    oh2 = lane == i2
    oh = (jnp.where(oh1, 1.0, 0.0) + jnp.where(oh2, 1.0, 0.0))
    before = jnp.dot(tri_ref[...], oh.astype(BF16), preferred_element_type=F32)
    base = before + running
    rank1 = jnp.sum(jnp.where(oh1, base, 0.0), axis=-1, keepdims=True)
    rank2 = jnp.sum(jnp.where(oh2, base, 0.0), axis=-1, keepdims=True)

    route = jnp.where(lane == 0, i1 - ROUTE_LANE0, 0.0)
    route = jnp.where(lane == 1, i2 - ROUTE_LANE0, route)
    route = jnp.where(lane == 2, rank1, route)
    route = jnp.where(lane == 3, rank2, route)
    route = jnp.where(lane == 4, gate1, route)
    route = jnp.where(lane == 5, gate2, route)
    route_out[rs, :] = route
    route_t_out[:, rs] = route.T[:route_t_out.shape[0]]
    return running + jnp.sum(oh, axis=0, keepdims=True)


def _post_attention(x2d, o_mla, o_fox, w):
    n, d = x2d.shape
    t = POST_TILE
    parts = d // LANES
    const = lambda i: (0, 0)
    row = lambda i: (i, 0)
    full = lambda a: pl.BlockSpec(a.shape, const)
    ins = [x2d, o_mla, o_fox, w["g_out_mla"], w["g_out_fox"], w["w_out_mla"], w["w_out_fox"],
           w["g_ffn"], w["w_router_hi"], w["w_router_lo"], w["tri_strict"]]
    in_specs = [pl.BlockSpec((t, d), row), pl.BlockSpec((t, o_mla.shape[1]), row),
                pl.BlockSpec((t, o_fox.shape[1]), row)] + [full(a) for a in ins[3:]]
    return pl.pallas_call(
        _post_kernel,
        out_shape=(jax.ShapeDtypeStruct((n, d), F32), jax.ShapeDtypeStruct((n * parts, LANES), F32),
                   jax.ShapeDtypeStruct((n, LANES), F32), jax.ShapeDtypeStruct((8, n), F32),
                   jax.ShapeDtypeStruct((8, LANES), F32)),
        grid=(n // t,),
        in_specs=in_specs,
        out_specs=(pl.BlockSpec((t, d), row), pl.BlockSpec((t * parts, LANES), row),
                   pl.BlockSpec((t, LANES), row), pl.BlockSpec((8, t), lambda i: (0, i)),
                   pl.BlockSpec((8, LANES), const)),
        scratch_shapes=[pltpu.VMEM((8, LANES), F32)],
        compiler_params=_cparams(1),
        name="post_attention",
    )(*ins)


def _tile_rows(dest, tile, parts):
    return (dest * parts).reshape(2, -1, tile).transpose(1, 0, 2).reshape(-1, 1, 2 * tile)


def _dispatch_kernel(pad_end_ref, padded_ref, n_used_ref, dest_ref, h_ref, xs_hbm, zero_ref, sem, zsem, *,
                     tile, parts):
    zrows = zero_ref.shape[0]
    n_blk = xs_hbm.shape[0] // zrows
    rblk = zrows // parts

    def zero_copy(blk):
        return pltpu.make_async_copy(zero_ref, xs_hbm.at[pl.ds(pl.multiple_of(blk * zrows, zrows), zrows)], zsem)

    @pl.when(pl.program_id(0) == 0)
    def _():
        zero_ref[...] = jnp.zeros_like(zero_ref)
        for wait in (False, True):
            for e in range(N_EXPERTS):
                @pl.when(padded_ref[e] > 0)
                def _():
                    cp = zero_copy(pad_end_ref[e] // rblk - 1)
                    cp.wait() if wait else cp.start()

            def tail(blk, c):
                cp = zero_copy(blk)
                cp.wait() if wait else cp.start()
                return c

            lax.fori_loop(n_used_ref[0], n_blk, tail, 0)

    def issue(jo, c):
        for ji in range(ISSUE_UNROLL):
            j = jo * ISSUE_UNROLL + ji
            src = h_ref.at[pl.ds(pl.multiple_of(j * parts, parts), parts)]
            for k in range(2):
                d = pl.multiple_of(dest_ref[0, 0, k * tile + j], parts)
                pltpu.make_async_copy(src, xs_hbm.at[pl.ds(d, parts)], sem).start(priority=k)
        return c

    lax.fori_loop(0, tile // ISSUE_UNROLL, issue, 0)
    for _ in range(2):
        pltpu.make_async_copy(h_ref, xs_hbm.at[pl.ds(0, tile * parts)], sem).wait()


def _dispatch(h2t, plan, parts):
    n = h2t.shape[0] // parts
    tile = DISPATCH_TILE
    dest3 = _tile_rows(plan["dest"], tile, parts)
    return pl.pallas_call(
        functools.partial(_dispatch_kernel, tile=tile, parts=parts),
        out_shape=jax.ShapeDtypeStruct((plan["n_rows"] * parts, LANES), h2t.dtype),
        grid_spec=pltpu.PrefetchScalarGridSpec(
            num_scalar_prefetch=3,
            grid=(n // tile,),
            in_specs=[pl.BlockSpec((1, 1, 2 * tile), lambda i, *_: (i, 0, 0), memory_space=pltpu.SMEM),
                      pl.BlockSpec((tile * parts, LANES), lambda i, *_: (i, 0))],
            out_specs=pl.BlockSpec(memory_space=pl.ANY),
            scratch_shapes=[pltpu.VMEM((EXPERT_BLOCK * parts, LANES), h2t.dtype), pltpu.SemaphoreType.DMA,
                            pltpu.SemaphoreType.DMA]),
        compiler_params=_cparams(1),
        name="moe_dispatch",
    )(plan["pad_end"], plan["padded"], plan["n_used"], dest3, h2t)


def _expert_kernel(blk_e_ref, n_used_ref, x_ref, wg_ref, wu_ref, wdn_ref, y_ref, wgu_bf, wd_bf, *, parts):
    i = pl.program_id(0)
    used = i < n_used_ref[0]
    rows = x_ref.shape[0] // parts
    fresh = jnp.logical_or(i == 0, blk_e_ref[i] != blk_e_ref[jnp.maximum(i - 1, 0)])

    @pl.when(jnp.logical_and(used, fresh))
    def _():
        wgu_bf[:, :EXPERT_FF] = wg_ref[0, 0].astype(BF16)
        wgu_bf[:, EXPERT_FF:] = wu_ref[0, 0].astype(BF16)
        wd_bf[...] = wdn_ref[0, 0].astype(BF16)

    @pl.when(used)
    def _():
        sub = EXPERT_SUB
        for sb in range(rows // sub):
            tiles = pl.ds(sb * sub * parts, sub * parts)
            xb = _load_row_tiles(x_ref.at[tiles], sub, parts).astype(BF16)
            gu = jnp.dot(xb, wgu_bf[...], preferred_element_type=F32)
            g = gu[:, :EXPERT_FF]
            act = (g / (1.0 + jnp.exp(-g))) * gu[:, EXPERT_FF:]
            _store_row_tiles(y_ref.at[tiles], jnp.dot(act.astype(BF16), wd_bf[...], preferred_element_type=F32))

    @pl.when(jnp.logical_not(used))
    def _():
        y_ref[...] = jnp.zeros_like(y_ref)


def _experts(xs, plan, layer, w_gate, w_up, w_down, parts):
    r = EXPERT_BLOCK
    d = parts * LANES
    rows = lambda i, be, nu: (jnp.minimum(i, nu[0] - 1), 0)
    wsel = lambda i, be, nu: (layer, be[i], 0, 0)
    return pl.pallas_call(
        functools.partial(_expert_kernel, parts=parts),
        out_shape=jax.ShapeDtypeStruct(xs.shape, F32),
        grid_spec=pltpu.PrefetchScalarGridSpec(
            num_scalar_prefetch=2,
            grid=(xs.shape[0] // (r * parts),),
            in_specs=[pl.BlockSpec((r * parts, LANES), rows),
                      pl.BlockSpec((1, 1, d, EXPERT_FF), wsel),
                      pl.BlockSpec((1, 1, d, EXPERT_FF), wsel),
                      pl.BlockSpec((1, 1, EXPERT_FF, d), wsel)],
            out_specs=pl.BlockSpec((r * parts, LANES), lambda i, be, nu: (i, 0)),
            scratch_shapes=[pltpu.VMEM((d, 2 * EXPERT_FF), BF16), pltpu.VMEM((EXPERT_FF, d), BF16)]),
        compiler_params=_cparams(1),
        name="moe_experts",
    )(plan["blk_e"], plan["n_used"], xs, w_gate, w_up, w_down)


def _ple_kernel(dest_ref, dest_next_ref, x1_ref, route_ref, p_ref, gnorm_ref, wgate_ref, wproj_ref, gout_ref,
                ys_hbm, x_out, rows_a, rows_b, sem, *, tile, parts):
    step = pl.program_id(0)
    n_steps = pl.num_programs(0)

    def row_copy(d, buf, slot, k, j):
        return pltpu.make_async_copy(ys_hbm.at[pl.ds(pl.multiple_of(d, parts), parts)],
                                     buf.at[k, pl.ds(j * parts, parts)], sem.at[slot])

    def wait_rows(buf, slot):
        for k in range(2):
            pltpu.make_async_copy(ys_hbm.at[pl.ds(0, tile * parts)], buf.at[k], sem.at[slot]).wait()

    @pl.when(step == 0)
    def _():
        def issue(jo, c):
            for ji in range(ISSUE_UNROLL):
                j = jo * ISSUE_UNROLL + ji
                for k in range(2):
                    pltpu.make_async_copy(
                        ys_hbm.at[pl.ds(pl.multiple_of(dest_ref[0, 0, k * tile + j], parts), parts)],
                        rows_a.at[k, pl.ds(pl.multiple_of(j * parts, parts), parts)], sem.at[0]).start(priority=k)
            return c
        lax.fori_loop(0, tile // ISSUE_UNROLL, issue, 0)

    def body(cur, nxt, slot):
        wait_rows(cur, slot)
        for j in range(tile):
            for k in range(2):
                row_copy(dest_next_ref[0, 0, k * tile + j], nxt, 1 - slot, k, j).start(priority=k)
        route = route_ref[...]
        y1 = _load_row_tiles(cur.at[0], tile, parts)
        y2 = _load_row_tiles(cur.at[1], tile, parts)
        x2 = x1_ref[...] + route[:, 4:5] * y1 + route[:, 5:6] * y2
        ple = _rms(jnp.dot(p_ref[0].astype(BF16), wproj_ref[...], preferred_element_type=F32), gout_ref[...])
        z = jnp.dot(_rms(x2, gnorm_ref[...]).astype(BF16), wgate_ref[...], preferred_element_type=F32)
        x_out[...] = x2 + ple / (1.0 + jnp.exp(-z))

        @pl.when(step == n_steps - 1)
        def _():
            wait_rows(nxt, 1 - slot)

    @pl.when(step % 2 == 0)
    def _():
        body(rows_a, rows_b, 0)

    @pl.when(step % 2 == 1)
    def _():
        body(rows_b, rows_a, 1)


def _combine_ple(x1, route, dest, ys, p3d, layer, w):
    n, d = x1.shape
    t = PLE_TILE
    parts = d // LANES
    steps = n // t
    const = lambda i: (0, 0)
    row = lambda i: (i, 0)
    full = lambda a: pl.BlockSpec(a.shape, const)
    dest3 = _tile_rows(dest, t, parts)
    return pl.pallas_call(
        functools.partial(_ple_kernel, tile=t, parts=parts),
        out_shape=jax.ShapeDtypeStruct((n, d), F32),
        grid=(steps,),
        in_specs=[pl.BlockSpec((1, 1, 2 * t), lambda i: (i, 0, 0), memory_space=pltpu.SMEM),
                  pl.BlockSpec((1, 1, 2 * t), lambda i: (jnp.minimum(i + 1, steps - 1), 0, 0),
                               memory_space=pltpu.SMEM),
                  pl.BlockSpec((t, d), row), pl.BlockSpec((t, LANES), row),
                  pl.BlockSpec((1, t, p3d.shape[2]), lambda i: (layer, i, 0)),
                  full(w["g_ple_norm"]), full(w["w_ple_gate"]), full(w["w_ple_proj"]), full(w["g_ple_out"]),
                  pl.BlockSpec(memory_space=pl.ANY)],
        out_specs=pl.BlockSpec((t, d), row),
        scratch_shapes=[pltpu.VMEM((2, t * parts, LANES), F32), pltpu.VMEM((2, t * parts, LANES), F32),
                        pltpu.SemaphoreType.DMA((2,))],
        compiler_params=_cparams(1),
        name="moe_combine_ple",
    )(dest3, dest3, x1, route, p3d, w["g_ple_norm"], w["w_ple_gate"], w["w_ple_proj"], w["g_ple_out"], ys)


def _row(v):
    return v.reshape(1, -1).astype(F32)


def _col_rep(v):
    return jnp.broadcast_to(v.astype(F32)[:, None], (v.shape[0], LANES))


def _pad_heads(wmat, real):
    k = wmat.shape[0]
    return jnp.pad(wmat.reshape(k, HEADS, real), ((0, 0), (0, 0), (0, LANES - real))).reshape(k, HEADS * LANES)


def _placement():
    place_k = np.zeros((LANES, HEADS * LANES), np.float32)
    place_q = np.zeros((HEADS * LANES, LANES), np.float32)
    for hd in range(HEADS):
        base = hd * LANES
        for j in range(3):
            src = j * PIECE_STRIDE + hd
            place_k[src, base + AUG_K + j] = -1.0
            place_q[base + AUG_Q + j, src] = 1.0
            place_k[ONE_LANE, base + AUG_Q + j] = 1.0
            place_q[base + AUG_K + j, ONE_LANE] = 1.0
    return jnp.asarray(place_k, BF16), jnp.asarray(place_q, BF16)


def _layer_weights(i, g_attn_norm, w_in, g_q_lora, w_uq, g_kv_lora, w_ukv, g_mla_q, g_mla_k, g_fox_q,
                   g_fox_k, b_fox_f, g_out_mla, g_out_fox, w_out, g_ffn_norm, w_router_group,
                   w_router_expert, w_exp_gate, w_exp_up, w_exp_down, g_ple_norm, w_ple_gate, w_ple_proj,
                   g_ple_out):
    d = w_in.shape[1]
    wf = HEADS * FOX_DIM
    c_kv = Q_LORA
    c_pe = c_kv + KV_LORA
    c_fq = c_pe + MLA_ROPE
    c_fk = c_fq + wf
    c_fv = c_fk + wf
    c_fl = c_fv + wf
    win = w_in[i]
    pad_pe = jnp.zeros((d, LANES), F32).at[:, MLA_NOPE:MLA_QK].set(win[:, c_pe:c_fq])
    group = lambda v: jnp.concatenate(
        [jnp.pad(v, [(0, 0)] * (v.ndim - 1) + [(0, PIECE_STRIDE - HEADS)])] * 3
        + [jnp.zeros(v.shape[:-1] + (LANES - 3 * PIECE_STRIDE,), v.dtype)], axis=-1)
    pad_fl = group(win[:, c_fl:])
    w = {}
    w["g_attn"] = _row(g_attn_norm[i])
    w["w_a"] = jnp.concatenate([win[:, :c_pe], pad_pe, pad_fl], axis=1).astype(BF16)
    w["w_fk"] = win[:, c_fk:c_fv].astype(BF16)
    w["w_fq_t"] = win[:, c_fq:c_fk].T.astype(BF16)
    w["w_fv_t"] = win[:, c_fv:c_fl].T.astype(BF16)
    w["b_fl"] = group(b_fox_f[i].astype(F32)[None, :])
    w["place_k"], w["place_q"] = _placement()
    w["g_q_lora"] = _row(g_q_lora[i])
    w["w_uq_t"] = w_uq[i].T.astype(BF16)
    w["g_kv_lora"] = _row(g_kv_lora[i])
    ukv = w_ukv[i].reshape(KV_LORA, HEADS, MLA_NOPE + HEAD_V)
    w["w_uk"] = ukv[:, :, :MLA_NOPE].reshape(KV_LORA, -1).astype(BF16)
    w["w_uv_t"] = ukv[:, :, MLA_NOPE:].reshape(KV_LORA, -1).T.astype(BF16)
    w["g_mla_q"] = _col_rep(jnp.pad(g_mla_q[i], (0, LANES - MLA_QK))) * (MLA_QK ** -0.5 * LOG2E)
    w["g_mla_k"] = _row(jnp.pad(g_mla_k[i], (0, LANES - MLA_QK)))
    w["g_fox_q"] = _col_rep(jnp.pad(g_fox_q[i], (0, LANES - FOX_DIM))) * (FOX_DIM ** -0.5 * LOG2E)
    w["g_fox_k"] = _row(jnp.pad(g_fox_k[i], (0, LANES - FOX_DIM)))
    bound = lambda gq, gk, dim: (1.02 * LOG2E * dim ** 0.5) * jnp.max(jnp.abs(gq)) * jnp.max(jnp.abs(gk))
    w["bound_mla"] = bound(g_mla_q[i], g_mla_k[i], MLA_QK).astype(F32)
    w["bound_fox"] = bound(g_fox_q[i], g_fox_k[i], FOX_DIM).astype(F32)
    w["g_out_mla"] = _row(g_out_mla[i])
    w["g_out_fox"] = _row(g_out_fox[i])
    wm = HEADS * HEAD_V
    w["w_out_mla"] = w_out[i, :wm].astype(BF16)
    w["w_out_fox"] = w_out[i, wm:].astype(BF16)
    w["g_ffn"] = _row(g_ffn_norm[i])
    wr = jnp.zeros((d, LANES), F32)
    wr = wr.at[:, :N_GROUPS].set(w_router_group[i]).at[:, ROUTE_LANE0:ROUTE_LANE0 + N_EXPERTS].set(
        w_router_expert[i])
    w["w_router_hi"] = wr.astype(BF16)
    w["w_router_lo"] = (wr - w["w_router_hi"].astype(F32)).astype(BF16)
    w["g_ple_norm"] = _row(g_ple_norm[i])
    w["w_ple_gate"] = w_ple_gate[i].astype(BF16)
    w["w_ple_proj"] = w_ple_proj[i].astype(BF16)
    w["g_ple_out"] = _row(g_ple_out[i])
    def tri(t, strict):
        r = lax.broadcasted_iota(jnp.int32, (t, t), 0)
        c = lax.broadcasted_iota(jnp.int32, (t, t), 1)
        return ((c < r) if strict else (c <= r)).astype(BF16)

    w["tri_incl"] = tri(PRE_SUB, False)
    w["tri_strict"] = tri(POST_SUB, True)
    return w


def _route_plan(route_t, counts, n):
    r = EXPERT_BLOCK
    cnt = counts[0, ROUTE_LANE0:ROUTE_LANE0 + N_EXPERTS].astype(jnp.int32)
    padded = ((cnt + r - 1) // r) * r
    pad_end = jnp.cumsum(padded)
    pad_start = pad_end - padded
    e = route_t[0:2].astype(jnp.int32)
    rank = route_t[2:4].astype(jnp.int32)
    onehot = e[:, None, :] == jnp.arange(N_EXPERTS, dtype=jnp.int32)[None, :, None]
    dest = jnp.sum(jnp.where(onehot, pad_start[None, :, None], 0), axis=1) + rank
    n_rows = 2 * n + N_EXPERTS * r
    blk_start = jnp.arange(n_rows // r, dtype=jnp.int32) * r
    blk_e = jnp.sum((blk_start[:, None] >= pad_end[None, :]).astype(jnp.int32), axis=1)
    blk_e = jnp.minimum(blk_e, N_EXPERTS - 1)
    n_used = (pad_end[-1:] // r).astype(jnp.int32)
    return {"dest": dest, "blk_e": blk_e, "n_used": n_used, "n_rows": n_rows,
            "pad_end": pad_end.astype(jnp.int32), "padded": padded.astype(jnp.int32)}


def kernel(x, p, positions, g_attn_norm, w_in, g_q_lora, w_uq, g_kv_lora, w_ukv, g_mla_q, g_mla_k, g_fox_q,
           g_fox_k, b_fox_f, g_out_mla, g_out_fox, w_out, g_ffn_norm, w_router_group, w_router_expert,
           w_exp_gate, w_exp_up, w_exp_down, g_ple_norm, w_ple_gate, w_ple_proj, g_ple_out):
    batch, seq, d = x.shape
    n = batch * seq
    depth = w_in.shape[0]
    params = (g_attn_norm, w_in, g_q_lora, w_uq, g_kv_lora, w_ukv, g_mla_q, g_mla_k, g_fox_q, g_fox_k,
              b_fox_f, g_out_mla, g_out_fox, w_out, g_ffn_norm, w_router_group, w_router_expert,
              w_exp_gate, w_exp_up, w_exp_down, g_ple_norm, w_ple_gate, w_ple_proj, g_ple_out)
    tables = _rope_tables(positions)
    xc = x.reshape(n, d)
    for i in range(depth):
        w = _layer_weights(i, *params)
        qt, k, vt, fqt, fk, fvt, fcum = _pre_attention(xc, tables, w, seq)
        b3 = lambda a: a.reshape(batch, seq, a.shape[-1])
        o_mla = _attention(qt, b3(k), vt, w["bound_mla"], None, chunked=True).reshape(n, -1)
        o_fox = _attention(fqt, b3(fk), fvt, w["bound_fox"], fcum, chunked=False).reshape(n, -1)
        x1, h2, route, route_t, counts = _post_attention(xc, o_mla, o_fox, w)
        plan = _route_plan(route_t, counts, n)
        parts = d // LANES
        xs = _dispatch(h2, plan, parts)
        ys = _experts(xs, plan, i, w_exp_gate, w_exp_up, w_exp_down, parts)
        xc = _combine_ple(x1, route, plan["dest"], ys, p.reshape(depth, n, -1), i, w)
    return xc.reshape(batch, seq, d)
```
